```python
import math
import jax, jax.numpy as jnp
from jax import lax
import numpy as np

D_MODEL = 1024
BATCH = 1
SEQ = 16384
DEPTH = 4

N_EVEN = (DEPTH + 1) // 2
N_ODD = DEPTH // 2
MIX_CHUNK = 128
A_WIDTH = D_MODEL // 2
A_GROUPS = 4
A_GROUP_DIM = A_WIDTH // A_GROUPS
B_WIDTH = D_MODEL - A_WIDTH
B_HEADS = 4
B_HEAD_V = B_WIDTH // B_HEADS
B_KEY_WIDTH = B_WIDTH // 2
B_HEAD_K = B_KEY_WIDTH // B_HEADS
GK_RANK = 16
GLA_GATE_NORMALIZER = 16.0
HGRN_EXPAND = 128
HGRN_HEADS = D_MODEL // HGRN_EXPAND
HGRN_KEY_WIDTH = HGRN_HEADS * HGRN_EXPAND
HGRN_HEAD_V = D_MODEL // HGRN_HEADS
LIN_CHUNK = 64
D_FF = ((8 * D_MODEL // 3 + 255) // 256) * 256
PLE_DIM = 256
DEEPNORM_ALPHA = (2.0 * DEPTH) ** 0.25
DEEPNORM_BETA = (8.0 * DEPTH) ** -0.25
LN_EPS = 1e-5
RMS_EPS = 1e-6
EVEN_IN_WIDTH = 2 * A_WIDTH + 2 * B_KEY_WIDTH + 2 * B_WIDTH + GK_RANK
EVEN_SPLIT_POINTS = (A_WIDTH, 2 * A_WIDTH, 2 * A_WIDTH + B_KEY_WIDTH, 2 * A_WIDTH + 2 * B_KEY_WIDTH,
                     2 * A_WIDTH + 2 * B_KEY_WIDTH + B_WIDTH, 2 * A_WIDTH + 2 * B_KEY_WIDTH + 2 * B_WIDTH)
ODD_IN_WIDTH = 2 * HGRN_KEY_WIDTH + 2 * D_MODEL
ODD_SPLIT_POINTS = (HGRN_KEY_WIDTH, 2 * HGRN_KEY_WIDTH, 2 * HGRN_KEY_WIDTH + D_MODEL)

kernel_name = "hybrid_gmlp_gla_hgrn2_deepnorm"


def layer_norm(x, g, b):
    xf = x.astype(jnp.float32)
    mu = jnp.mean(xf, axis=-1, keepdims=True)
    var = jnp.mean(jnp.square(xf - mu), axis=-1, keepdims=True)
    return ((xf - mu) * lax.rsqrt(var + LN_EPS) * g + b).astype(x.dtype)


def rms_norm(x, g):
    xf = x.astype(jnp.float32)
    ms = jnp.mean(jnp.square(xf), axis=-1, keepdims=True)
    return (xf * lax.rsqrt(ms + RMS_EPS) * g).astype(x.dtype)


def chunked_gated_linear_attention(q, k, v, log_f):
    B, S, H, dk = q.shape
    dv = v.shape[-1]
    nc = S // LIN_CHUNK
    out_dtype = v.dtype

    def to_chunks(t):
        return t.astype(jnp.float32).reshape(B, nc, LIN_CHUNK, H, t.shape[-1]).transpose(1, 0, 3, 2, 4)

    qc, kc, vc, gc = to_chunks(q), to_chunks(k), to_chunks(v), to_chunks(log_f)
    causal = jnp.tril(jnp.ones((LIN_CHUNK, LIN_CHUNK), dtype=bool))[:, :, None]

    def step(state, inp):
        qi, ki, vi, gi = inp
        b = jnp.cumsum(gi, axis=-2)
        diff = b[..., :, None, :] - b[..., None, :, :]
        decay = jnp.where(causal, jnp.exp(jnp.where(causal, diff, 0.0)), 0.0)
        scores = jnp.einsum('bhtd,bhsd,bhtsd->bhts', qi, ki, decay)
        o_intra = jnp.einsum('bhts,bhsv->bhtv', scores, vi)
        o_inter = jnp.einsum('bhtd,bhdv->bhtv', qi * jnp.exp(b), state)
        b_last = b[..., -1:, :]
        k_dec = ki * jnp.exp(b_last - b)
        new_state = state * jnp.exp(b_last[..., 0, :])[..., None] + jnp.einsum('bhsd,bhsv->bhdv', k_dec, vi)
        return new_state, o_intra + o_inter

    state0 = jnp.zeros((B, H, dk, dv), jnp.float32)
    _, out = lax.scan(step, state0, (qc, kc, vc, gc))
    return out.transpose(1, 0, 3, 2, 4).reshape(B, S, H, dv).astype(out_dtype)


def spatial_gating_unit(u, v, w_s, b_s, ln_g, ln_b):
    B, S, _ = u.shape
    nc = S // MIX_CHUNK
    v = jax.nn.gelu(v).reshape(B, S, A_GROUPS, A_GROUP_DIM)
    v = layer_norm(v, ln_g.reshape(A_GROUPS, A_GROUP_DIM), ln_b.reshape(A_GROUPS, A_GROUP_DIM))
    v = v.reshape(B, nc, MIX_CHUNK, A_GROUPS, A_GROUP_DIM)
    causal = jnp.tril(jnp.ones((MIX_CHUNK, MIX_CHUNK), dtype=bool))
    w = jnp.where(causal[None], w_s, 0.0)
    mixed = jnp.einsum('gts,bnsgd->bntgd', w, v) + b_s.T[:, :, None]
    return jax.nn.gelu(u) * mixed.reshape(B, S, A_WIDTH)


def gla_mixer(q, k, v, g, gk_low, w_gk_up, b_gk, norm_g):
    B, S, _ = q.shape
    log_a = jax.nn.log_sigmoid((gk_low @ w_gk_up + b_gk).astype(jnp.float32)) / GLA_GATE_NORMALIZER
    q = q.reshape(B, S, B_HEADS, B_HEAD_K) * (B_HEAD_K ** -0.5)
    k = k.reshape(B, S, B_HEADS, B_HEAD_K)
    v = v.reshape(B, S, B_HEADS, B_HEAD_V)
    o = chunked_gated_linear_attention(q, k, v, log_a.reshape(B, S, B_HEADS, B_HEAD_K))
    o = rms_norm(o, norm_g) * jax.nn.silu(g.reshape(B, S, B_HEADS, B_HEAD_V))
    return o.reshape(B, S, B_WIDTH)


def hgrn2_mixer(q, f, i, g, lb, norm_g):
    B, S, _ = q.shape
    f32 = f.astype(jnp.float32)
    log_forget = jnp.logaddexp(jnp.log(lb), jnp.log1p(-lb) + jax.nn.log_sigmoid(f32))
    k_in = (1.0 - lb) * jax.nn.sigmoid(-f32)
    q = jax.nn.silu(q)
    o = chunked_gated_linear_attention(
        q.reshape(B, S, HGRN_HEADS, HGRN_EXPAND),
        k_in.reshape(B, S, HGRN_HEADS, HGRN_EXPAND).astype(q.dtype),
        i.reshape(B, S, HGRN_HEADS, HGRN_HEAD_V),
        log_forget.reshape(B, S, HGRN_HEADS, HGRN_EXPAND))
    o = rms_norm(o, norm_g) * jax.nn.sigmoid(g.reshape(B, S, HGRN_HEADS, HGRN_HEAD_V))
    return o.reshape(B, S, D_MODEL)


def even_mixer(x, w_in, w_s, b_s, sgu_ln_g, sgu_ln_b, w_gk_up, b_gk, gla_norm_g, w_out):
    u, v_a, q, k, v_b, g, gk_low = jnp.split(x @ w_in, EVEN_SPLIT_POINTS, axis=-1)
    y_a = spatial_gating_unit(u, v_a, w_s, b_s, sgu_ln_g, sgu_ln_b)
    y_b = gla_mixer(q, k, v_b, g, gk_low, w_gk_up, b_gk, gla_norm_g)
    return jnp.concatenate([y_a, y_b], axis=-1) @ w_out


def odd_mixer(x, w_in, lb, norm_g, w_out):
    q, f, i, g = jnp.split(x @ w_in, ODD_SPLIT_POINTS, axis=-1)
    return hgrn2_mixer(q, f, i, g, lb, norm_g) @ w_out


def swiglu(x, w_in, w_out):
    gate, up = jnp.split(x @ w_in, 2, axis=-1)
    return (jax.nn.silu(gate) * up) @ w_out


def setup_inputs(seed: int = 0) -> dict:
    key = jax.random.key(seed)
    ks = jax.random.split(key, 32)
    nrm = lambda k, shape, scale: jax.random.normal(k, shape, jnp.float32) * scale
    gain = lambda k, shape: 1.0 + 0.02 * jax.random.normal(k, shape, jnp.float32)
    return {
        "x": nrm(ks[0], (BATCH, SEQ, D_MODEL), 1.0),
        "p": nrm(ks[1], (DEPTH, BATCH, SEQ, PLE_DIM), 1.0),
        "even_w_in": nrm(ks[2], (N_EVEN, D_MODEL, EVEN_IN_WIDTH), D_MODEL ** -0.5),
        "even_w_s": nrm(ks[3], (N_EVEN, A_GROUPS, MIX_CHUNK, MIX_CHUNK), 0.5 * MIX_CHUNK ** -0.5),
        "even_b_s": gain(ks[4], (N_EVEN, A_GROUPS, MIX_CHUNK)),
        "even_sgu_ln_g": gain(ks[5], (N_EVEN, A_WIDTH)),
        "even_sgu_ln_b": nrm(ks[6], (N_EVEN, A_WIDTH), 0.02),
        "even_w_gk_up": nrm(ks[7], (N_EVEN, GK_RANK, B_KEY_WIDTH), GK_RANK ** -0.5),
        "even_b_gk": nrm(ks[8], (N_EVEN, B_KEY_WIDTH), 0.1),
        "even_gla_norm_g": gain(ks[9], (N_EVEN, B_HEAD_V)),
        "even_w_out": nrm(ks[10], (N_EVEN, D_MODEL, D_MODEL), DEEPNORM_BETA * D_MODEL ** -0.5),
        "odd_w_in": nrm(ks[11], (N_ODD, D_MODEL, ODD_IN_WIDTH), D_MODEL ** -0.5),
        "hgrn_lb_logits": nrm(ks[12], (N_ODD, HGRN_KEY_WIDTH), 0.5),
        "odd_hgrn_norm_g": gain(ks[13], (N_ODD, HGRN_HEAD_V)),
        "odd_w_out": nrm(ks[14], (N_ODD, D_MODEL, D_MODEL), DEEPNORM_BETA * D_MODEL ** -0.5),
        "ln_mix_g": gain(ks[15], (DEPTH, D_MODEL)),
        "ln_mix_b": nrm(ks[16], (DEPTH, D_MODEL), 0.02),
        "w_ffn_in": nrm(ks[17], (DEPTH, D_MODEL, 2 * D_FF), D_MODEL ** -0.5),
        "w_ffn_out": nrm(ks[18], (DEPTH, D_FF, D_MODEL), DEEPNORM_BETA * D_FF ** -0.5),
        "ln_ffn_g": gain(ks[19], (DEPTH, D_MODEL)),
        "ln_ffn_b": nrm(ks[20], (DEPTH, D_MODEL), 0.02),
        "w_ple_proj": nrm(ks[21], (DEPTH, PLE_DIM, D_MODEL), PLE_DIM ** -0.5),
        "w_ple_gate": nrm(ks[22], (DEPTH, D_MODEL, D_MODEL), D_MODEL ** -0.5),
    }


def reference(x, p, even_w_in, even_w_s, even_b_s, even_sgu_ln_g, even_sgu_ln_b, even_w_gk_up, even_b_gk,
              even_gla_norm_g, even_w_out, odd_w_in, hgrn_lb_logits, odd_hgrn_norm_g, odd_w_out,
              ln_mix_g, ln_mix_b, w_ffn_in, w_ffn_out, ln_ffn_g, ln_ffn_b, w_ple_proj, w_ple_gate):
    probs = jax.nn.softmax(hgrn_lb_logits.astype(jnp.float32), axis=0)
    cum = jnp.cumsum(probs, axis=0)
    lower_bounds = cum - cum[0]
    for layer in range(DEPTH):
        j = layer // 2
        if layer % 2 == 0:
            mix = even_mixer(x, even_w_in[j], even_w_s[j], even_b_s[j], even_sgu_ln_g[j], even_sgu_ln_b[j],
                             even_w_gk_up[j], even_b_gk[j], even_gla_norm_g[j], even_w_out[j])
        else:
            mix = odd_mixer(x, odd_w_in[j], lower_bounds[j], odd_hgrn_norm_g[j], odd_w_out[j])
        x = layer_norm(DEEPNORM_ALPHA * x + mix, ln_mix_g[layer], ln_mix_b[layer])
        x = layer_norm(DEEPNORM_ALPHA * x + swiglu(x, w_ffn_in[layer], w_ffn_out[layer]),
                       ln_ffn_g[layer], ln_ffn_b[layer])
        x = x + jax.nn.sigmoid(x @ w_ple_gate[layer]) * (p[layer] @ w_ple_proj[layer])
    return x
```

```python
import functools
import math

import jax
import jax.numpy as jnp
from jax import lax
from jax.experimental import pallas as pl
from jax.experimental.pallas import tpu as pltpu

F32 = jnp.float32
BF16 = jnp.bfloat16

LANES = 128
LN_EPS = 1e-5
RMS_EPS = 1e-6
GLA_GATE_NORMALIZER = 16.0
GK_RANK = 16
A_GROUPS = 4
SGU_CHUNK = 128
LIN_CHUNK = 64
MIX_ROWS = 256
POST_ROWS = 512
FF_CHUNK = 256
VMEM_LIMIT = 56 * 1024 * 1024


def _dot(a, b):
    return jnp.dot(a, b, preferred_element_type=F32)


def _dot_nt(a, b):
    return lax.dot_general(a, b, (((1,), (1,)), ((), ())), preferred_element_type=F32)


def _dot_tn(a, b):
    return lax.dot_general(a, b, (((0,), (0,)), ((), ())), preferred_element_type=F32)


def _layer_norm(x, g, b):
    mu = jnp.mean(x, axis=-1, keepdims=True)
    xc = x - mu
    var = jnp.mean(xc * xc, axis=-1, keepdims=True)
    return xc * lax.rsqrt(var + LN_EPS) * g + b


def _log_sigmoid(x):
    return jnp.minimum(x, 0.0) - jnp.log1p(jnp.exp(-jnp.abs(x)))


def _const_spec(shape):
    nd = len(shape)
    return pl.BlockSpec(shape, lambda i: (0,) * nd, pipeline_mode=pl.Buffered(1))


def _cumsum_rows(tril_b, x):
    hi = x.astype(BF16)
    r1 = x - hi.astype(F32)
    mid = r1.astype(BF16)
    lo = (r1 - mid.astype(F32)).astype(BF16)
    return _dot(tril_b, hi) + _dot(tril_b, mid) + _dot(tril_b, lo)


def _linear_attention_chunk(q, k, lf, v_heads, st_ref, heads_per_group):
    c = q.shape[0]
    n_groups = q.shape[1] // LANES
    head_k = LANES // heads_per_group
    rows = lax.broadcasted_iota(jnp.int32, (c, c), 0)
    cols = lax.broadcasted_iota(jnp.int32, (c, c), 1)
    causal = cols <= rows
    tril_b = jnp.where(causal, 1.0, 0.0).astype(BF16)

    b = _cumsum_rows(tril_b, lf)
    b_mid = b[c // 2:c // 2 + 1, :]
    b_last = b[c - 1:c, :]
    qe = (q * jnp.exp(b - b_mid)).astype(BF16)
    ke = (k * jnp.exp(b_mid - b)).astype(BF16)
    qb = (q * jnp.exp(b)).astype(BF16)
    kd = (k * jnp.exp(b_last - b)).astype(BF16)
    dec = jnp.exp(b_last)

    lane = lax.broadcasted_iota(jnp.int32, (c, LANES), 1)
    outs = []
    for gi in range(n_groups):
        sl = slice(gi * LANES, (gi + 1) * LANES)
        st = st_ref[gi]
        st_b = st.astype(BF16)
        upd = None
        for s in range(heads_per_group):
            h = gi * heads_per_group + s
            if heads_per_group == 1:
                qe_h, qb_h, kd_h = qe[:, sl], qb[:, sl], kd[:, sl]
            else:
                in_head = (lane >= s * head_k) & (lane < (s + 1) * head_k)
                zero = jnp.zeros((c, LANES), BF16)
                qe_h = jnp.where(in_head, qe[:, sl], zero)
                qb_h = jnp.where(in_head, qb[:, sl], zero)
                kd_h = jnp.where(in_head, kd[:, sl], zero)
            v_b = v_heads[h].astype(BF16)
            p = jnp.where(causal, _dot_nt(qe_h, ke[:, sl]), 0.0)
            o = _dot_nt(qb_h, st_b) + _dot(p.astype(BF16), v_b)
            outs.append(o)
            u = _dot_tn(v_b, kd_h)
            upd = u if upd is None else upd + u
        st_ref[gi] = st * dec[:, sl] + upd
    return outs


def _hgrn_kernel(x_ref, win_ref, lbl_ref, ng_ref, o_ref, proj_ref, st_ref, *, layer_j, d_model):
    i = pl.program_id(0)

    @pl.when(i == 0)
    def _():
        st_ref[...] = jnp.zeros_like(st_ref)

    logits = lbl_ref[...]
    e = jnp.exp(logits - jnp.max(logits, axis=0, keepdims=True))
    probs = e / jnp.sum(e, axis=0, keepdims=True)
    lb = jnp.zeros((1, d_model), F32)
    for r in range(1, layer_j + 1):
        lb = lb + probs[r:r + 1, :]
    log_lb = jnp.log(lb)
    log_1m_lb = jnp.log1p(-lb)
    one_m_lb = 1.0 - lb

    proj_ref[...] = _dot(x_ref[...].astype(BF16), win_ref[...])
    n_heads = d_model // LANES
    rows = x_ref.shape[0]
    for c0 in range(0, rows, LIN_CHUNK):
        rs = slice(c0, c0 + LIN_CHUNK)
        q = proj_ref[rs, 0:d_model]
        f = proj_ref[rs, d_model:2 * d_model]
        iv = proj_ref[rs, 2 * d_model:3 * d_model]
        g = proj_ref[rs, 3 * d_model:4 * d_model]
        ef = jnp.exp(-jnp.abs(f))
        log_sig = jnp.minimum(f, 0.0) - jnp.log1p(ef)
        sig_neg = jnp.where(f >= 0.0, ef, 1.0) / (1.0 + ef)
        a2 = log_1m_lb + log_sig
        lf = jnp.maximum(log_lb, a2) + jnp.log1p(jnp.exp(-jnp.abs(log_lb - a2)))
        k = one_m_lb * sig_neg
        q = q * jax.nn.sigmoid(q)
        v_heads = [iv[:, h * LANES:(h + 1) * LANES] for h in range(n_heads)]
        outs = _linear_attention_chunk(q, k, lf, v_heads, st_ref, 1)
        ng = ng_ref[...]
        for h in range(n_heads):
            o = outs[h]
            ms = jnp.mean(o * o, axis=-1, keepdims=True)
            gate = jax.nn.sigmoid(g[:, h * LANES:(h + 1) * LANES])
            y = o * lax.rsqrt(ms + RMS_EPS) * ng * gate
            o_ref[rs, h * LANES:(h + 1) * LANES] = y.astype(o_ref.dtype)


def _hgrn_mixer(x2d, w_in_b, lb_logits, norm_g, layer_j):
    seq, d_model = x2d.shape
    n_heads = d_model // LANES
    kern = functools.partial(_hgrn_kernel, layer_j=layer_j, d_model=d_model)
    return pl.pallas_call(
        kern,
        grid=(seq // MIX_ROWS,),
        in_specs=[
            pl.BlockSpec((MIX_ROWS, d_model), lambda i: (i, 0)),
            _const_spec(w_in_b.shape),
            _const_spec(lb_logits.shape),
            _const_spec(norm_g.shape),
        ],
        out_specs=pl.BlockSpec((MIX_ROWS, d_model), lambda i: (i, 0)),
        out_shape=jax.ShapeDtypeStruct((seq, d_model), BF16),
        scratch_shapes=[
            pltpu.VMEM((MIX_ROWS, w_in_b.shape[1]), F32),
            pltpu.VMEM((n_heads, LANES, LANES), F32),
        ],
        compiler_params=pltpu.CompilerParams(
            dimension_semantics=("arbitrary",), vmem_limit_bytes=VMEM_LIMIT),
        name="hgrn_mixer",
    )(x2d, w_in_b, lb_logits, norm_g)


def _even_kernel(x_ref, win_ref, ws_ref, bs_ref, lng_ref, lnb_ref, wgk_ref, bgk_ref, ng_ref,
                 o_ref, proj_ref, st_ref, *, a_width, key_width, b_width):
    i = pl.program_id(0)

    @pl.when(i == 0)
    def _():
        st_ref[...] = jnp.zeros_like(st_ref)

    proj_ref[...] = _dot(x_ref[...].astype(BF16), win_ref[...])
    rows = x_ref.shape[0]
    o_q = 2 * a_width
    o_k = o_q + key_width
    o_v = o_k + key_width
    o_g = o_v + b_width
    o_gk = o_g + b_width
    group_dim = a_width // A_GROUPS

    n_sgu = rows // SGU_CHUNK
    r_i = lax.broadcasted_iota(jnp.int32, (SGU_CHUNK, SGU_CHUNK), 0)
    c_i = lax.broadcasted_iota(jnp.int32, (SGU_CHUNK, SGU_CHUNK), 1)
    sgu_causal = c_i <= r_i
    for gi in range(A_GROUPS):
        ls = slice(gi * group_dim, (gi + 1) * group_dim)
        v = jax.nn.gelu(proj_ref[:, a_width + gi * group_dim:a_width + (gi + 1) * group_dim])
        vn = _layer_norm(v, lng_ref[:, ls], lnb_ref[:, ls]).astype(BF16)
        rhs = jnp.concatenate(
            [vn[n * SGU_CHUNK:(n + 1) * SGU_CHUNK, :] for n in range(n_sgu)], axis=1)
        w = jnp.where(sgu_causal, ws_ref[gi], 0.0).astype(BF16)
        mixed = _dot(w, rhs)
        bias = bs_ref[:, ls]
        for n in range(n_sgu):
            rs = slice(n * SGU_CHUNK, (n + 1) * SGU_CHUNK)
            u = jax.nn.gelu(proj_ref[rs, ls])
            y = u * (mixed[:, n * group_dim:(n + 1) * group_dim] + bias)
            o_ref[rs, ls] = y.astype(o_ref.dtype)

    n_heads = b_width // LANES
    head_k = key_width // n_heads
    heads_per_group = LANES // head_k
    scale = head_k ** -0.5
    for c0 in range(0, rows, LIN_CHUNK):
        rs = slice(c0, c0 + LIN_CHUNK)
        gk_low = proj_ref[rs, o_gk:o_gk + LANES].astype(BF16)
        z = _dot(gk_low, wgk_ref[...]) + bgk_ref[...]
        lf = _log_sigmoid(z) * (1.0 / GLA_GATE_NORMALIZER)
        q = proj_ref[rs, o_q:o_q + key_width] * scale
        k = proj_ref[rs, o_k:o_k + key_width]
        v_heads = [proj_ref[rs, o_v + h * LANES:o_v + (h + 1) * LANES] for h in range(n_heads)]
        outs = _linear_attention_chunk(q, k, lf, v_heads, st_ref, heads_per_group)
        ng = ng_ref[...]
        for h in range(n_heads):
            o = outs[h]
            ms = jnp.mean(o * o, axis=-1, keepdims=True)
            g = proj_ref[rs, o_g + h * LANES:o_g + (h + 1) * LANES]
            y = o * lax.rsqrt(ms + RMS_EPS) * ng * (g * jax.nn.sigmoid(g))
            o_ref[rs, a_width + h * LANES:a_width + (h + 1) * LANES] = y.astype(o_ref.dtype)


def _even_mixer(x2d, w_in_b, w_s, bias_full, ln_g, ln_b, w_gk_b, b_gk, norm_g, a_width, key_width,
                b_width):
    seq, d_model = x2d.shape
    kern = functools.partial(_even_kernel, a_width=a_width, key_width=key_width, b_width=b_width)
    consts = (w_in_b, w_s, bias_full, ln_g, ln_b, w_gk_b, b_gk, norm_g)
    return pl.pallas_call(
        kern,
        grid=(seq // MIX_ROWS,),
        in_specs=[pl.BlockSpec((MIX_ROWS, d_model), lambda i: (i, 0))]
        + [_const_spec(c.shape) for c in consts],
        out_specs=pl.BlockSpec((MIX_ROWS, d_model), lambda i: (i, 0)),
        out_shape=jax.ShapeDtypeStruct((seq, d_model), BF16),
        scratch_shapes=[
            pltpu.VMEM((MIX_ROWS, w_in_b.shape[1]), F32),
            pltpu.VMEM((key_width // LANES, LANES, LANES), F32),
        ],
        compiler_params=pltpu.CompilerParams(
            dimension_semantics=("arbitrary",), vmem_limit_bytes=VMEM_LIMIT),
        name="even_mixer",
    )(x2d, *consts)


def _post_kernel(x_ref, y_ref, p_ref, wout_ref, lnmg_ref, lnmb_ref, wfg_ref, wfu_ref, wfo_ref,
                 lnfg_ref, lnfb_ref, wpg_ref, wpp_ref, o_ref, *, alpha):
    x = x_ref[...]
    x1 = _layer_norm(alpha * x + _dot(y_ref[...], wout_ref[...]), lnmg_ref[...], lnmb_ref[...])
    x1b = x1.astype(BF16)
    acc = jnp.zeros(x.shape, F32)
    for c in range(wfg_ref.shape[0]):
        hg = _dot(x1b, wfg_ref[c])
        hu = _dot(x1b, wfu_ref[c])
        act = (hg * jax.nn.sigmoid(hg) * hu).astype(BF16)
        acc = acc + _dot(act, wfo_ref[c])
    x2 = _layer_norm(alpha * x1 + acc, lnfg_ref[...], lnfb_ref[...])
    gate = jax.nn.sigmoid(_dot(x2.astype(BF16), wpg_ref[...]))
    emb = _dot(p_ref[...].astype(BF16), wpp_ref[...])
    o_ref[...] = x2 + gate * emb


def _post(x2d, y2d, p2d, w_out_b, lnm_g, lnm_b, wfg, wfu, wfo, lnf_g, lnf_b, wpg, wpp, alpha):
    seq, d_model = x2d.shape
    consts1 = (w_out_b, lnm_g, lnm_b, wfg, wfu, wfo, lnf_g, lnf_b, wpg, wpp)
    kern = functools.partial(_post_kernel, alpha=alpha)
    return pl.pallas_call(
        kern,
        grid=(seq // POST_ROWS,),
        in_specs=[
            pl.BlockSpec((POST_ROWS, d_model), lambda i: (i, 0)),
            pl.BlockSpec((POST_ROWS, d_model), lambda i: (i, 0)),
            pl.BlockSpec((POST_ROWS, p2d.shape[1]), lambda i: (i, 0)),
        ] + [_const_spec(c.shape) for c in consts1],
        out_specs=pl.BlockSpec((POST_ROWS, d_model), lambda i: (i, 0)),
        out_shape=jax.ShapeDtypeStruct((seq, d_model), F32),
        compiler_params=pltpu.CompilerParams(
            dimension_semantics=("arbitrary",), vmem_limit_bytes=VMEM_LIMIT),
        name="post",
    )(x2d, y2d, p2d, *consts1)


def kernel(x, p, even_w_in, even_w_s, even_b_s, even_sgu_ln_g, even_sgu_ln_b, even_w_gk_up, even_b_gk,
           even_gla_norm_g, even_w_out, odd_w_in, hgrn_lb_logits, odd_hgrn_norm_g, odd_w_out,
           ln_mix_g, ln_mix_b, w_ffn_in, w_ffn_out, ln_ffn_g, ln_ffn_b, w_ple_proj, w_ple_gate):
    batch, seq, d_model = x.shape
    depth = p.shape[0]
    assert batch == 1 and seq % POST_ROWS == 0 and seq % MIX_ROWS == 0
    alpha = (2.0 * depth) ** 0.25
    a_width = even_sgu_ln_g.shape[1]
    b_width = d_model - a_width
    key_width = even_w_gk_up.shape[2]
    d_ff = w_ffn_out.shape[1]
    assert d_ff % FF_CHUNK == 0
    n_fc = d_ff // FF_CHUNK
    even_in = even_w_in.shape[2]
    even_in_pad = -(-even_in // LANES) * LANES
    row = lambda a: a.reshape(1, -1).astype(F32)

    xs = x.reshape(seq, d_model)
    for layer in range(depth):
        j = layer // 2
        if layer % 2 == 0:
            w_in_b = jnp.pad(even_w_in[j], ((0, 0), (0, even_in_pad - even_in))).astype(BF16)
            w_gk_b = jnp.pad(even_w_gk_up[j], ((0, LANES - GK_RANK), (0, 0))).astype(BF16)
            bias_full = jnp.repeat(even_b_s[j].T, a_width // A_GROUPS, axis=1)
            y = _even_mixer(xs, w_in_b, even_w_s[j], bias_full, row(even_sgu_ln_g[j]),
                            row(even_sgu_ln_b[j]), w_gk_b, row(even_b_gk[j]),
                            row(even_gla_norm_g[j]), a_width, key_width, b_width)
            w_out = even_w_out[j]
        else:
            y = _hgrn_mixer(xs, odd_w_in[j].astype(BF16), hgrn_lb_logits.astype(F32),
                            row(odd_hgrn_norm_g[j]), j)
            w_out = odd_w_out[j]
        w_in_ffn = w_ffn_in[layer].astype(BF16)
        wfg = w_in_ffn[:, :d_ff].reshape(d_model, n_fc, FF_CHUNK).transpose(1, 0, 2)
        wfu = w_in_ffn[:, d_ff:].reshape(d_model, n_fc, FF_CHUNK).transpose(1, 0, 2)
        wfo = w_ffn_out[layer].astype(BF16).reshape(n_fc, FF_CHUNK, d_model)
        xs = _post(xs, y, p[layer].reshape(seq, -1), w_out.astype(BF16), row(ln_mix_g[layer]),
                   row(ln_mix_b[layer]), wfg, wfu, wfo, row(ln_ffn_g[layer]), row(ln_ffn_b[layer]),
                   w_ple_gate[layer].astype(BF16), w_ple_proj[layer].astype(BF16), alpha)
    return xs.reshape(batch, seq, d_model)
```

```python
import functools

import jax
import jax.numpy as jnp
from jax import lax
from jax.experimental import pallas as pl
from jax.experimental.pallas import tpu as pltpu

F32 = jnp.float32
BF16 = jnp.bfloat16

LANES = 128
MXU_WIDTH = 256
LN_EPS = 1e-5
RMS_EPS = 1e-6
LOG2_E = 1.4426950408889634
GLA_GATE_NORMALIZER = 16.0
GK_RANK = 16
A_GROUPS = 4
SGU_CHUNK = 128
LIN_CHUNK = 64
MIX_ROWS = 256
POST_ROWS = 512
FF_CHUNK = 256
PROJ_SLAB = MXU_WIDTH
UNIT = MXU_WIDTH
VMEM_LIMIT = 56 * 1024 * 1024


def _dot(a, b):
    return jnp.dot(a, b, preferred_element_type=F32)


def _dot_nt(a, b):
    return lax.dot_general(a, b, (((1,), (1,)), ((), ())), preferred_element_type=F32)


def _dot_tn(a, b):
    return lax.dot_general(a, b, (((0,), (0,)), ((), ())), preferred_element_type=F32)


def _layer_norm(x, g, b):
    mu = jnp.mean(x, axis=-1, keepdims=True)
    xc = x - mu
    var = jnp.mean(xc * xc, axis=-1, keepdims=True)
    return xc * lax.rsqrt(var + LN_EPS) * g + b


def _log_sigmoid(x):
    return jnp.minimum(x, 0.0) - jnp.log(1.0 + jnp.exp(-jnp.abs(x)))


def _const_spec(shape):
    nd = len(shape)
    return pl.BlockSpec(shape, lambda i: (0,) * nd, pipeline_mode=pl.Buffered(1))


def _causal_mask(c):
    rows = lax.broadcasted_iota(jnp.int32, (c, c), 0)
    cols = lax.broadcasted_iota(jnp.int32, (c, c), 1)
    return cols <= rows


def _cumsum_rows(x):
    tril_b = jnp.where(_causal_mask(x.shape[0]), 1.0, 0.0).astype(BF16)
    hi = x.astype(BF16)
    r1 = x - hi.astype(F32)
    mid = r1.astype(BF16)
    lo = (r1 - mid.astype(F32)).astype(BF16)
    return _dot(tril_b, hi) + _dot(tril_b, mid) + _dot(tril_b, lo)


def _recurrence_prepare(q, k, lf, lf_scale, heads_per_group):
    c, width = q.shape
    b = _cumsum_rows(lf * (lf_scale * LOG2_E))
    b_mid = b[c // 2:c // 2 + 1, :]
    b_last = b[c - 1:c, :]
    qe = (q * jnp.exp2(b - b_mid)).astype(BF16)
    ke = (k * jnp.exp2(b_mid - b)).astype(BF16)
    qb = (q * jnp.exp2(b)).astype(BF16)
    kd = (k * jnp.exp2(b_last - b)).astype(BF16)
    dec = jnp.exp2(b_last)
    head_k = LANES // heads_per_group
    lane = lax.broadcasted_iota(jnp.int32, (c, LANES), 1)
    groups = []
    for gi in range(width // LANES):
        sl = slice(gi * LANES, (gi + 1) * LANES)
        heads = []
        for s in range(heads_per_group):
            if heads_per_group == 1:
                heads.append((qe[:, sl], qb[:, sl], kd[:, sl]))
            else:
                in_head = (lane >= s * head_k) & (lane < (s + 1) * head_k)
                zero = jnp.zeros((c, LANES), BF16)
                heads.append(tuple(jnp.where(in_head, a[:, sl], zero) for a in (qe, qb, kd)))
        groups.append(dict(ke=ke[:, sl], dec=dec[:, sl], heads=heads))
    return groups


def _recurrence_scores(groups, v_heads):
    c = v_heads[0].shape[0]
    causal = _causal_mask(c)
    h = 0
    for grp in groups:
        grp["p"] = []
        upd = None
        for qe_h, _, kd_h in grp["heads"]:
            p = jnp.where(causal, _dot_nt(qe_h, grp["ke"]), 0.0).astype(BF16)
            grp["p"].append(p)
            u = _dot_tn(v_heads[h], kd_h)
            upd = u if upd is None else upd + u
            h += 1
        grp["upd"] = upd


def _recurrence_outputs(groups, v_heads, st_ref, g0):
    outs = []
    h = 0
    for gi, grp in enumerate(groups):
        st = st_ref[g0 + gi]
        st_b = st.astype(BF16)
        for (_, qb_h, _), p in zip(grp["heads"], grp["p"]):
            outs.append(_dot_nt(qb_h, st_b) + _dot(p, v_heads[h]))
            h += 1
        st_ref[g0 + gi] = st * grp["dec"] + grp["upd"]
    return outs


def _run_recurrence(n_chunks, n_units, prepare, scores, outputs, finish, extra, slot):
    for u in range(n_units):
        prepare(0, u)
    for c in range(n_chunks):
        for u in range(n_units):
            if c + 1 < n_chunks:
                prepare(c + 1, u)
            scores(c, u)
            slot()
        extra(c)
        for u in range(n_units):
            outputs(c, u)
            if u > 0:
                finish(c, u - 1)
            slot()
        finish(c, n_units - 1)


def _pipelined_mixer_step(x_ref, win_ref, buf_a, buf_b, st_ref, mix, n_slots):
    i = pl.program_id(0)

    @pl.when(i == 0)
    def _():
        st_ref[...] = jnp.zeros_like(st_ref)
        buf_b[...] = jnp.zeros_like(buf_b)

    def step(w_buf, r_buf):
        xb = x_ref[...].astype(BF16)
        width = w_buf.shape[1]
        pending = list(range(0, width, PROJ_SLAB))
        n_slabs = len(pending)
        seen = [0]

        def project_slab():
            lo = pending.pop(0)
            hi = min(lo + PROJ_SLAB, width)
            w_buf[:, lo:hi] = _dot(xb, win_ref[:, lo:hi])

        def slot():
            seen[0] += 1
            while pending and (n_slabs - len(pending)) * n_slots < seen[0] * n_slabs:
                project_slab()

        mix(r_buf, slot)
        while pending:
            project_slab()

    @pl.when(lax.rem(i, 2) == 0)
    def _():
        step(buf_a, buf_b)

    @pl.when(lax.rem(i, 2) == 1)
    def _():
        step(buf_b, buf_a)


def _mixer_call(kern, name, x2d, consts, proj_width, n_state):
    seq, d_model = x2d.shape
    n_blocks = seq // MIX_ROWS
    return pl.pallas_call(
        kern,
        grid=(n_blocks + 1,),
        in_specs=[pl.BlockSpec((MIX_ROWS, d_model), lambda i: (jnp.minimum(i, n_blocks - 1), 0))]
        + [_const_spec(c.shape) for c in consts],
        out_specs=pl.BlockSpec((MIX_ROWS, d_model), lambda i: (jnp.maximum(i - 1, 0), 0)),
        out_shape=jax.ShapeDtypeStruct((seq, d_model), BF16),
        scratch_shapes=[
            pltpu.VMEM((MIX_ROWS, proj_width), F32),
            pltpu.VMEM((MIX_ROWS, proj_width), F32),
            pltpu.VMEM((n_state, LANES, LANES), F32),
        ],
        compiler_params=pltpu.CompilerParams(
            dimension_semantics=("arbitrary",), vmem_limit_bytes=VMEM_LIMIT),
        name=name,
    )(x2d, *consts)


def _hgrn_kernel(x_ref, win_ref, lbl_ref, ng_ref, o_ref, buf_a, buf_b, st_ref, *, layer_j, d_model):
    logits = lbl_ref[...]
    e = jnp.exp(logits - jnp.max(logits, axis=0, keepdims=True))
    probs = e / jnp.sum(e, axis=0, keepdims=True)
    lb = jnp.zeros((1, d_model), F32)
    for r in range(1, layer_j + 1):
        lb = lb + probs[r:r + 1, :]
    log_lb = jnp.log(lb)
    log_1m_lb = jnp.log1p(-lb)
    one_m_lb = 1.0 - lb
    rows = x_ref.shape[0]
    n_chunks = rows // LIN_CHUNK
    n_units = d_model // UNIT
    heads_per_unit = UNIT // LANES

    def mix(proj_ref, slot):
        work = {}

        def unit_heads(u):
            return range(u * heads_per_unit, (u + 1) * heads_per_unit)

        def prepare(c, u):
            rs = slice(c * LIN_CHUNK, (c + 1) * LIN_CHUNK)
            ls = slice(u * UNIT, (u + 1) * UNIT)
            q = proj_ref[rs, u * UNIT:(u + 1) * UNIT]
            f = proj_ref[rs, d_model + u * UNIT:d_model + (u + 1) * UNIT]
            ef = jnp.exp(-jnp.abs(f))
            one_p_ef = 1.0 + ef
            log_sig = jnp.minimum(f, 0.0) - jnp.log(one_p_ef)
            sig_neg = jnp.where(f >= 0.0, ef, 1.0) / one_p_ef
            a1 = log_lb[:, ls]
            a2 = log_1m_lb[:, ls] + log_sig
            lf = jnp.maximum(a1, a2) + jnp.log(1.0 + jnp.exp(-jnp.abs(a1 - a2)))
            k = one_m_lb[:, ls] * sig_neg
            q = q * jax.nn.sigmoid(q)
            work[c, u] = _recurrence_prepare(q, k, lf, 1.0, 1)

        def values(c, u):
            rs = slice(c * LIN_CHUNK, (c + 1) * LIN_CHUNK)
            return [proj_ref[rs, 2 * d_model + h * LANES:2 * d_model + (h + 1) * LANES].astype(BF16)
                    for h in unit_heads(u)]

        def scores(c, u):
            _recurrence_scores(work[c, u], values(c, u))

        def outputs(c, u):
            work[c, u] = _recurrence_outputs(work[c, u], values(c, u), st_ref, u * heads_per_unit)

        def finish(c, u):
            rs = slice(c * LIN_CHUNK, (c + 1) * LIN_CHUNK)
            ng = ng_ref[...]
            for h, o in zip(unit_heads(u), work.pop((c, u))):
                ms = jnp.mean(o * o, axis=-1, keepdims=True)
                g = proj_ref[rs, 3 * d_model + h * LANES:3 * d_model + (h + 1) * LANES]
                y = o * lax.rsqrt(ms + RMS_EPS) * ng * jax.nn.sigmoid(g)
                o_ref[rs, h * LANES:(h + 1) * LANES] = y.astype(o_ref.dtype)

        _run_recurrence(n_chunks, n_units, prepare, scores, outputs, finish, lambda c: None, slot)

    _pipelined_mixer_step(x_ref, win_ref, buf_a, buf_b, st_ref, mix, 2 * n_chunks * n_units)


def _hgrn_mixer(x2d, w_in_b, lb_logits, norm_g, layer_j):
    d_model = x2d.shape[1]
    kern = functools.partial(_hgrn_kernel, layer_j=layer_j, d_model=d_model)
    return _mixer_call(kern, "hgrn_mixer", x2d, (w_in_b, lb_logits, norm_g), w_in_b.shape[1],
                       d_model // LANES)


def _even_kernel(x_ref, win_ref, ws_ref, bs_ref, lng_ref, lnb_ref, wgk_ref, bgk_ref, ng_ref,
                 o_ref, buf_a, buf_b, st_ref, *, a_width, key_width, b_width):
    rows = x_ref.shape[0]
    o_q = 2 * a_width
    o_k = o_q + key_width
    o_v = o_k + key_width
    o_g = o_v + b_width
    o_gk = o_g + b_width
    group_dim = a_width // A_GROUPS
    n_sgu = rows // SGU_CHUNK
    n_chunks = rows // LIN_CHUNK
    n_heads = b_width // LANES
    head_k = key_width // n_heads
    heads_per_group = LANES // head_k
    scale = head_k ** -0.5
    assert key_width == UNIT and n_chunks >= A_GROUPS

    def mix(proj_ref, slot):
        work = {}

        def sgu_group(gi):
            ls = slice(gi * group_dim, (gi + 1) * group_dim)
            v = jax.nn.gelu(proj_ref[:, a_width + gi * group_dim:a_width + (gi + 1) * group_dim])
            vn = _layer_norm(v, lng_ref[:, ls], lnb_ref[:, ls]).astype(BF16)
            rhs = jnp.concatenate(
                [vn[n * SGU_CHUNK:(n + 1) * SGU_CHUNK, :] for n in range(n_sgu)], axis=1)
            w = jnp.where(_causal_mask(SGU_CHUNK), ws_ref[gi], 0.0).astype(BF16)
            mixed = _dot(w, rhs)
            bias = bs_ref[:, ls]
            for n in range(n_sgu):
                rs = slice(n * SGU_CHUNK, (n + 1) * SGU_CHUNK)
                u = jax.nn.gelu(proj_ref[rs, ls])
                y = u * (mixed[:, n * group_dim:(n + 1) * group_dim] + bias)
                o_ref[rs, ls] = y.astype(o_ref.dtype)

        def extra(c):
            if c < A_GROUPS:
                sgu_group(c)
                slot()

        def prepare(c, u):
            rs = slice(c * LIN_CHUNK, (c + 1) * LIN_CHUNK)
            gk_low = proj_ref[rs, o_gk:o_gk + LANES].astype(BF16)
            lf = _log_sigmoid(_dot(gk_low, wgk_ref[...]) + bgk_ref[...])
            q = proj_ref[rs, o_q:o_q + key_width] * scale
            k = proj_ref[rs, o_k:o_k + key_width]
            work[c] = _recurrence_prepare(q, k, lf, 1.0 / GLA_GATE_NORMALIZER, heads_per_group)

        def values(c):
            rs = slice(c * LIN_CHUNK, (c + 1) * LIN_CHUNK)
            return [proj_ref[rs, o_v + h * LANES:o_v + (h + 1) * LANES].astype(BF16)
                    for h in range(n_heads)]

        def scores(c, u):
            _recurrence_scores(work[c], values(c))

        def outputs(c, u):
            work[c] = _recurrence_outputs(work[c], values(c), st_ref, 0)

        def finish(c, u):
            rs = slice(c * LIN_CHUNK, (c + 1) * LIN_CHUNK)
            ng = ng_ref[...]
            for h, o in enumerate(work.pop(c)):
                ms = jnp.mean(o * o, axis=-1, keepdims=True)
                g = proj_ref[rs, o_g + h * LANES:o_g + (h + 1) * LANES]
                y = o * lax.rsqrt(ms + RMS_EPS) * ng * (g * jax.nn.sigmoid(g))
                o_ref[rs, a_width + h * LANES:a_width + (h + 1) * LANES] = y.astype(o_ref.dtype)

        _run_recurrence(n_chunks, 1, prepare, scores, outputs, finish, extra, slot)

    _pipelined_mixer_step(x_ref, win_ref, buf_a, buf_b, st_ref, mix, 2 * n_chunks + A_GROUPS)


def _even_mixer(x2d, w_in_b, w_s, bias_full, ln_g, ln_b, w_gk_b, b_gk, norm_g, a_width, key_width,
                b_width):
    kern = functools.partial(_even_kernel, a_width=a_width, key_width=key_width, b_width=b_width)
    consts = (w_in_b, w_s, bias_full, ln_g, ln_b, w_gk_b, b_gk, norm_g)
    return _mixer_call(kern, "even_mixer", x2d, consts, w_in_b.shape[1], key_width // LANES)


def _post_kernel(x_ref, y_ref, p_ref, wout_ref, lnmg_ref, lnmb_ref, wfi_ref, wfo_ref,
                 lnfg_ref, lnfb_ref, wpg_ref, wpp_ref, o_ref, *, alpha):
    x = x_ref[...]
    x1 = _layer_norm(alpha * x + _dot(y_ref[...], wout_ref[...]), lnmg_ref[...], lnmb_ref[...])
    x1b = x1.astype(BF16)
    acc = jnp.zeros(x.shape, F32)
    d_ff = wfo_ref.shape[0]
    for c0 in range(0, d_ff, FF_CHUNK):
        hg = _dot(x1b, wfi_ref[:, c0:c0 + FF_CHUNK])
        hu = _dot(x1b, wfi_ref[:, d_ff + c0:d_ff + c0 + FF_CHUNK])
        act = (hg * jax.nn.sigmoid(hg) * hu).astype(BF16)
        acc = acc + _dot(act, wfo_ref[c0:c0 + FF_CHUNK, :])
    x2 = _layer_norm(alpha * x1 + acc, lnfg_ref[...], lnfb_ref[...])
    gate = jax.nn.sigmoid(_dot(x2.astype(BF16), wpg_ref[...]))
    emb = _dot(p_ref[...].astype(BF16), wpp_ref[...])
    o_ref[...] = x2 + gate * emb


def _post(x2d, y2d, p2d, w_out_b, lnm_g, lnm_b, wfi, wfo, lnf_g, lnf_b, wpg, wpp, alpha):
    seq, d_model = x2d.shape
    consts1 = (w_out_b, lnm_g, lnm_b, wfi, wfo, lnf_g, lnf_b, wpg, wpp)
    kern = functools.partial(_post_kernel, alpha=alpha)
    return pl.pallas_call(
        kern,
        grid=(seq // POST_ROWS,),
        in_specs=[
            pl.BlockSpec((POST_ROWS, d_model), lambda i: (i, 0)),
            pl.BlockSpec((POST_ROWS, d_model), lambda i: (i, 0)),
            pl.BlockSpec((POST_ROWS, p2d.shape[1]), lambda i: (i, 0)),
        ] + [_const_spec(c.shape) for c in consts1],
        out_specs=pl.BlockSpec((POST_ROWS, d_model), lambda i: (i, 0)),
        out_shape=jax.ShapeDtypeStruct((seq, d_model), F32),
        compiler_params=pltpu.CompilerParams(
            dimension_semantics=("arbitrary",), vmem_limit_bytes=VMEM_LIMIT),
        name="post",
    )(x2d, y2d, p2d, *consts1)


def kernel(x, p, even_w_in, even_w_s, even_b_s, even_sgu_ln_g, even_sgu_ln_b, even_w_gk_up, even_b_gk,
           even_gla_norm_g, even_w_out, odd_w_in, hgrn_lb_logits, odd_hgrn_norm_g, odd_w_out,
           ln_mix_g, ln_mix_b, w_ffn_in, w_ffn_out, ln_ffn_g, ln_ffn_b, w_ple_proj, w_ple_gate):
    batch, seq, d_model = x.shape
    depth = p.shape[0]
    assert batch == 1 and seq % POST_ROWS == 0 and seq % MIX_ROWS == 0
    alpha = (2.0 * depth) ** 0.25
    a_width = even_sgu_ln_g.shape[1]
    b_width = d_model - a_width
    key_width = even_w_gk_up.shape[2]
    d_ff = w_ffn_out.shape[1]
    assert d_ff % FF_CHUNK == 0
    even_in = even_w_in.shape[2]
    even_in_pad = -(-even_in // LANES) * LANES
    row = lambda a: a.reshape(1, -1).astype(F32)

    xs = x.reshape(seq, d_model)
    for layer in range(depth):
        j = layer // 2
        if layer % 2 == 0:
            w_in_b = jnp.pad(even_w_in[j], ((0, 0), (0, even_in_pad - even_in))).astype(BF16)
            w_gk_b = jnp.pad(even_w_gk_up[j], ((0, LANES - GK_RANK), (0, 0))).astype(BF16)
            bias_full = jnp.repeat(even_b_s[j].T, a_width // A_GROUPS, axis=1)
            y = _even_mixer(xs, w_in_b, even_w_s[j], bias_full, row(even_sgu_ln_g[j]),
                            row(even_sgu_ln_b[j]), w_gk_b, row(even_b_gk[j]),
                            row(even_gla_norm_g[j]), a_width, key_width, b_width)
            w_out = even_w_out[j]
        else:
            y = _hgrn_mixer(xs, odd_w_in[j].astype(BF16), hgrn_lb_logits.astype(F32),
                            row(odd_hgrn_norm_g[j]), j)
            w_out = odd_w_out[j]
        xs = _post(xs, y, p[layer].reshape(seq, -1), w_out.astype(BF16), row(ln_mix_g[layer]),
                   row(ln_mix_b[layer]), w_ffn_in[layer].astype(BF16), w_ffn_out[layer].astype(BF16),
                   row(ln_ffn_g[layer]), row(ln_ffn_b[layer]),
                   w_ple_gate[layer].astype(BF16), w_ple_proj[layer].astype(BF16), alpha)
    return xs.reshape(batch, seq, d_model)
```

```python
import functools

import jax
import jax.numpy as jnp
from jax import lax
from jax.experimental import pallas as pl
from jax.experimental.pallas import tpu as pltpu

F32 = jnp.float32
BF16 = jnp.bfloat16

LANES = 128
MXU_WIDTH = 256
LN_EPS = 1e-5
RMS_EPS = 1e-6
LOG2_E = 1.4426950408889634
GLA_GATE_NORMALIZER = 16.0
GK_RANK = 16
A_GROUPS = 4
SGU_CHUNK = 128
LIN_CHUNK = 64
MIX_ROWS = 256
POST_ROWS = 512
POST_SPLIT = 2
POST_LAG = 2
FF_CHUNK = 256
PROJ_SLAB = MXU_WIDTH
UNIT = MXU_WIDTH
VMEM_LIMIT = 56 * 1024 * 1024


def _dot(a, b):
    return jnp.dot(a, b, preferred_element_type=F32)


def _dot_nt(a, b):
    return lax.dot_general(a, b, (((1,), (1,)), ((), ())), preferred_element_type=F32)


def _dot_tn(a, b):
    return lax.dot_general(a, b, (((0,), (0,)), ((), ())), preferred_element_type=F32)


def _layer_norm(x, g, b):
    mu = jnp.mean(x, axis=-1, keepdims=True)
    xc = x - mu
    var = jnp.mean(xc * xc, axis=-1, keepdims=True)
    return xc * lax.rsqrt(var + LN_EPS) * g + b


def _log_sigmoid(x):
    return jnp.minimum(x, 0.0) - jnp.log(1.0 + jnp.exp(-jnp.abs(x)))


def _const_spec(shape):
    nd = len(shape)
    return pl.BlockSpec(shape, lambda i: (0,) * nd, pipeline_mode=pl.Buffered(1))


def _layer_operand(stacked, layer):
    nd = stacked.ndim
    spec = pl.BlockSpec((None,) + stacked.shape[1:], lambda i: (layer,) + (0,) * (nd - 1),
                        pipeline_mode=pl.Buffered(1))
    return stacked, spec


def _causal_mask(c):
    rows = lax.broadcasted_iota(jnp.int32, (c, c), 0)
    cols = lax.broadcasted_iota(jnp.int32, (c, c), 1)
    return cols <= rows


def _cumsum_rows(x):
    tril_b = jnp.where(_causal_mask(x.shape[0]), 1.0, 0.0).astype(BF16)
    hi = x.astype(BF16)
    r1 = x - hi.astype(F32)
    mid = r1.astype(BF16)
    lo = (r1 - mid.astype(F32)).astype(BF16)
    return _dot(tril_b, hi) + _dot(tril_b, mid) + _dot(tril_b, lo)


def _recurrence_prepare(q, k, lf, lf_scale, heads_per_group):
    c, width = q.shape
    b = _cumsum_rows(lf * (lf_scale * LOG2_E))
    b_mid = b[c // 2:c // 2 + 1, :]
    b_last = b[c - 1:c, :]
    qe = (q * jnp.exp2(b - b_mid)).astype(BF16)
    ke = (k * jnp.exp2(b_mid - b)).astype(BF16)
    qb = (q * jnp.exp2(b)).astype(BF16)
    kd = (k * jnp.exp2(b_last - b)).astype(BF16)
    dec = jnp.exp2(b_last)
    head_k = LANES // heads_per_group
    lane = lax.broadcasted_iota(jnp.int32, (c, LANES), 1)
    groups = []
    for gi in range(width // LANES):
        sl = slice(gi * LANES, (gi + 1) * LANES)
        heads = []
        for s in range(heads_per_group):
            if heads_per_group == 1:
                heads.append((qe[:, sl], qb[:, sl], kd[:, sl]))
            else:
                in_head = (lane >= s * head_k) & (lane < (s + 1) * head_k)
                zero = jnp.zeros((c, LANES), BF16)
                heads.append(tuple(jnp.where(in_head, a[:, sl], zero) for a in (qe, qb, kd)))
        groups.append(dict(ke=ke[:, sl], dec=dec[:, sl], heads=heads))
    return groups


def _recurrence_scores(groups, v_heads):
    c = v_heads[0].shape[0]
    causal = _causal_mask(c)
    h = 0
    for grp in groups:
        grp["p"] = []
        upd = None
        for qe_h, _, kd_h in grp["heads"]:
            p = jnp.where(causal, _dot_nt(qe_h, grp["ke"]), 0.0).astype(BF16)
            grp["p"].append(p)
            u = _dot_tn(v_heads[h], kd_h)
            upd = u if upd is None else upd + u
            h += 1
        grp["upd"] = upd


def _recurrence_outputs(groups, v_heads, st_ref, g0):
    outs = []
    h = 0
    for gi, grp in enumerate(groups):
        st = st_ref[g0 + gi]
        st_b = st.astype(BF16)
        for (_, qb_h, _), p in zip(grp["heads"], grp["p"]):
            outs.append(_dot_nt(qb_h, st_b) + _dot(p, v_heads[h]))
            h += 1
        st_ref[g0 + gi] = st * grp["dec"] + grp["upd"]
    return outs


def _run_recurrence(n_chunks, n_units, prepare, scores, outputs, finish, extra, slot):
    for u in range(n_units):
        prepare(0, u)
    for c in range(n_chunks):
        for u in range(n_units):
            if c + 1 < n_chunks:
                prepare(c + 1, u)
            scores(c, u)
            slot()
        extra(c)
        for u in range(n_units):
            outputs(c, u)
            if u > 0:
                finish(c, u - 1)
            slot()
        finish(c, n_units - 1)


def _pipelined_mixer_step(x_ref, win_ref, buf_a, buf_b, st_ref, mix, n_slots):
    i = pl.program_id(0)

    @pl.when(i == 0)
    def _():
        st_ref[...] = jnp.zeros_like(st_ref)
        buf_b[...] = jnp.zeros_like(buf_b)

    def step(w_buf, r_buf):
        xb = x_ref[...].astype(BF16)
        width = w_buf.shape[1]
        pending = list(range(0, width, PROJ_SLAB))
        n_slabs = len(pending)
        seen = [0]

        def project_slab():
            lo = pending.pop(0)
            hi = min(lo + PROJ_SLAB, width)
            w_buf[:, lo:hi] = _dot(xb, win_ref[:, lo:hi])

        def slot():
            seen[0] += 1
            while pending and (n_slabs - len(pending)) * n_slots < seen[0] * n_slabs:
                project_slab()

        mix(r_buf, slot)
        while pending:
            project_slab()

    @pl.when(lax.rem(i, 2) == 0)
    def _():
        step(buf_a, buf_b)

    @pl.when(lax.rem(i, 2) == 1)
    def _():
        step(buf_b, buf_a)


def _mixer_call(kern, name, x2d, consts, n_state):
    seq, d_model = x2d.shape
    proj_width = consts[0][0].shape[-1]
    n_blocks = seq // MIX_ROWS
    return pl.pallas_call(
        kern,
        grid=(n_blocks + 1,),
        in_specs=[pl.BlockSpec((MIX_ROWS, d_model), lambda i: (jnp.minimum(i, n_blocks - 1), 0))]
        + [spec for _, spec in consts],
        out_specs=pl.BlockSpec((MIX_ROWS, d_model), lambda i: (jnp.maximum(i - 1, 0), 0)),
        out_shape=jax.ShapeDtypeStruct((seq, d_model), BF16),
        scratch_shapes=[
            pltpu.VMEM((MIX_ROWS, proj_width), F32),
            pltpu.VMEM((MIX_ROWS, proj_width), F32),
            pltpu.VMEM((n_state, LANES, LANES), F32),
        ],
        compiler_params=pltpu.CompilerParams(
            dimension_semantics=("arbitrary",), vmem_limit_bytes=VMEM_LIMIT),
        name=name,
    )(x2d, *[arr for arr, _ in consts])


def _hgrn_kernel(x_ref, win_ref, lbl_ref, ng_ref, o_ref, buf_a, buf_b, st_ref, *, layer_j, d_model):
    logits = lbl_ref[...]
    e = jnp.exp(logits - jnp.max(logits, axis=0, keepdims=True))
    probs = e / jnp.sum(e, axis=0, keepdims=True)
    lb = jnp.zeros((1, d_model), F32)
    for r in range(1, layer_j + 1):
        lb = lb + probs[r:r + 1, :]
    log_lb = jnp.log(lb)
    log_1m_lb = jnp.log1p(-lb)
    one_m_lb = 1.0 - lb
    rows = x_ref.shape[0]
    n_chunks = rows // LIN_CHUNK
    n_units = d_model // UNIT
    heads_per_unit = UNIT // LANES

    def mix(proj_ref, slot):
        work = {}

        def unit_heads(u):
            return range(u * heads_per_unit, (u + 1) * heads_per_unit)

        def prepare(c, u):
            rs = slice(c * LIN_CHUNK, (c + 1) * LIN_CHUNK)
            ls = slice(u * UNIT, (u + 1) * UNIT)
            q = proj_ref[rs, u * UNIT:(u + 1) * UNIT]
            f = proj_ref[rs, d_model + u * UNIT:d_model + (u + 1) * UNIT]
            ef = jnp.exp(-jnp.abs(f))
            one_p_ef = 1.0 + ef
            log_sig = jnp.minimum(f, 0.0) - jnp.log(one_p_ef)
            sig_neg = jnp.where(f >= 0.0, ef, 1.0) / one_p_ef
            a1 = log_lb[:, ls]
            a2 = log_1m_lb[:, ls] + log_sig
            lf = jnp.maximum(a1, a2) + jnp.log(1.0 + jnp.exp(-jnp.abs(a1 - a2)))
            k = one_m_lb[:, ls] * sig_neg
            q = q * jax.nn.sigmoid(q)
            work[c, u] = _recurrence_prepare(q, k, lf, 1.0, 1)

        def values(c, u):
            rs = slice(c * LIN_CHUNK, (c + 1) * LIN_CHUNK)
            return [proj_ref[rs, 2 * d_model + h * LANES:2 * d_model + (h + 1) * LANES].astype(BF16)
                    for h in unit_heads(u)]

        def scores(c, u):
            _recurrence_scores(work[c, u], values(c, u))

        def outputs(c, u):
            work[c, u] = _recurrence_outputs(work[c, u], values(c, u), st_ref, u * heads_per_unit)

        def finish(c, u):
            rs = slice(c * LIN_CHUNK, (c + 1) * LIN_CHUNK)
            ng = ng_ref[...]
            for h, o in zip(unit_heads(u), work.pop((c, u))):
                ms = jnp.mean(o * o, axis=-1, keepdims=True)
                g = proj_ref[rs, 3 * d_model + h * LANES:3 * d_model + (h + 1) * LANES]
                y = o * lax.rsqrt(ms + RMS_EPS) * ng * jax.nn.sigmoid(g)
                o_ref[rs, h * LANES:(h + 1) * LANES] = y.astype(o_ref.dtype)

        _run_recurrence(n_chunks, n_units, prepare, scores, outputs, finish, lambda c: None, slot)

    _pipelined_mixer_step(x_ref, win_ref, buf_a, buf_b, st_ref, mix, 2 * n_chunks * n_units)


def _hgrn_mixer(x2d, consts, layer_j):
    d_model = x2d.shape[1]
    kern = functools.partial(_hgrn_kernel, layer_j=layer_j, d_model=d_model)
    return _mixer_call(kern, "hgrn_mixer", x2d, consts, d_model // LANES)


def _even_kernel(x_ref, win_ref, ws_ref, bs_ref, lng_ref, lnb_ref, wgk_ref, bgk_ref, ng_ref,
                 o_ref, buf_a, buf_b, st_ref, *, a_width, key_width, b_width):
    rows = x_ref.shape[0]
    o_q = 2 * a_width
    o_k = o_q + key_width
    o_v = o_k + key_width
    o_g = o_v + b_width
    o_gk = o_g + b_width
    group_dim = a_width // A_GROUPS
    n_sgu = rows // SGU_CHUNK
    n_chunks = rows // LIN_CHUNK
    n_heads = b_width // LANES
    head_k = key_width // n_heads
    heads_per_group = LANES // head_k
    scale = head_k ** -0.5
    assert key_width == UNIT and n_chunks >= A_GROUPS

    def mix(proj_ref, slot):
        work = {}

        def sgu_group(gi):
            ls = slice(gi * group_dim, (gi + 1) * group_dim)
            v = jax.nn.gelu(proj_ref[:, a_width + gi * group_dim:a_width + (gi + 1) * group_dim])
            vn = _layer_norm(v, lng_ref[:, ls], lnb_ref[:, ls]).astype(BF16)
            rhs = jnp.concatenate(
                [vn[n * SGU_CHUNK:(n + 1) * SGU_CHUNK, :] for n in range(n_sgu)], axis=1)
            w = jnp.where(_causal_mask(SGU_CHUNK), ws_ref[gi], 0.0).astype(BF16)
            mixed = _dot(w, rhs)
            bias = bs_ref[:, ls]
            for n in range(n_sgu):
                rs = slice(n * SGU_CHUNK, (n + 1) * SGU_CHUNK)
                u = jax.nn.gelu(proj_ref[rs, ls])
                y = u * (mixed[:, n * group_dim:(n + 1) * group_dim] + bias)
                o_ref[rs, ls] = y.astype(o_ref.dtype)

        def extra(c):
            if c < A_GROUPS:
                sgu_group(c)
                slot()

        def prepare(c, u):
            rs = slice(c * LIN_CHUNK, (c + 1) * LIN_CHUNK)
            gk_low = proj_ref[rs, o_gk:o_gk + LANES].astype(BF16)
            lf = _log_sigmoid(_dot(gk_low, wgk_ref[...]) + bgk_ref[...])
            q = proj_ref[rs, o_q:o_q + key_width] * scale
            k = proj_ref[rs, o_k:o_k + key_width]
            work[c] = _recurrence_prepare(q, k, lf, 1.0 / GLA_GATE_NORMALIZER, heads_per_group)

        def values(c):
            rs = slice(c * LIN_CHUNK, (c + 1) * LIN_CHUNK)
            return [proj_ref[rs, o_v + h * LANES:o_v + (h + 1) * LANES].astype(BF16)
                    for h in range(n_heads)]

        def scores(c, u):
            _recurrence_scores(work[c], values(c))

        def outputs(c, u):
            work[c] = _recurrence_outputs(work[c], values(c), st_ref, 0)

        def finish(c, u):
            rs = slice(c * LIN_CHUNK, (c + 1) * LIN_CHUNK)
            ng = ng_ref[...]
            for h, o in enumerate(work.pop(c)):
                ms = jnp.mean(o * o, axis=-1, keepdims=True)
                g = proj_ref[rs, o_g + h * LANES:o_g + (h + 1) * LANES]
                y = o * lax.rsqrt(ms + RMS_EPS) * ng * (g * jax.nn.sigmoid(g))
                o_ref[rs, a_width + h * LANES:a_width + (h + 1) * LANES] = y.astype(o_ref.dtype)

        _run_recurrence(n_chunks, 1, prepare, scores, outputs, finish, extra, slot)

    _pipelined_mixer_step(x_ref, win_ref, buf_a, buf_b, st_ref, mix, 2 * n_chunks + A_GROUPS)


def _even_mixer(x2d, consts, a_width, key_width, b_width):
    kern = functools.partial(_even_kernel, a_width=a_width, key_width=key_width, b_width=b_width)
    return _mixer_call(kern, "even_mixer", x2d, consts, key_width // LANES)


def _post_kernel(x_ref, y_ref, p_ref, wout_ref, lnmg_ref, lnmb_ref, wfi_ref, wfo_ref,
                 lnfg_ref, lnfb_ref, wpg_ref, wpp_ref, o_ref, *, alpha):
    rows = x_ref.shape[0]
    d_ff = wfo_ref.shape[0]
    chunks = list(range(0, d_ff, FF_CHUNK))
    parts = [slice(n * rows // POST_SPLIT, (n + 1) * rows // POST_SPLIT) for n in range(POST_SPLIT)]
    st = [dict() for _ in parts]

    def out_proj(n):
        st[n]["mixed"] = _dot(y_ref[parts[n], :], wout_ref[...])

    def norm_mix(n):
        x1 = _layer_norm(alpha * x_ref[parts[n], :] + st[n].pop("mixed"), lnmg_ref[...], lnmb_ref[...])
        st[n]["x1"], st[n]["x1b"] = x1, x1.astype(BF16)

    def ffn_in(n, c0):
        x1b = st[n]["x1b"]
        st[n]["h", c0] = (_dot(x1b, wfi_ref[:, c0:c0 + FF_CHUNK]),
                          _dot(x1b, wfi_ref[:, d_ff + c0:d_ff + c0 + FF_CHUNK]))

    def ffn_out(n, k):
        if k + 1 < len(chunks):
            ffn_in(n, chunks[k + 1])
        hg, hu = st[n].pop(("h", chunks[k]))
        act = (hg * jax.nn.sigmoid(hg) * hu).astype(BF16)
        part = _dot(act, wfo_ref[chunks[k]:chunks[k] + FF_CHUNK, :])
        st[n]["acc"] = part if k == 0 else st[n]["acc"] + part

    def norm_ffn(n):
        st[n]["x2"] = _layer_norm(alpha * st[n].pop("x1") + st[n].pop("acc"), lnfg_ref[...],
                                  lnfb_ref[...])

    def embed(n):
        x2 = st[n].pop("x2")
        gate = jax.nn.sigmoid(_dot(x2.astype(BF16), wpg_ref[...]))
        emb = _dot(p_ref[parts[n], :].astype(BF16), wpp_ref[...])
        o_ref[parts[n], :] = x2 + gate * emb

    stages = [out_proj, norm_mix, lambda n: ffn_in(n, chunks[0])]
    stages += [functools.partial(ffn_out, k=k) for k in range(len(chunks))]
    stages += [norm_ffn, embed]
    for t in range(len(stages) + POST_LAG * (POST_SPLIT - 1)):
        for n in range(POST_SPLIT):
            k = t - n * POST_LAG
            if 0 <= k < len(stages):
                stages[k](n)


def _post(x2d, y2d, p3d, layer, consts, alpha):
    seq, d_model = x2d.shape
    kern = functools.partial(_post_kernel, alpha=alpha)
    return pl.pallas_call(
        kern,
        grid=(seq // POST_ROWS,),
        in_specs=[
            pl.BlockSpec((POST_ROWS, d_model), lambda i: (i, 0)),
            pl.BlockSpec((POST_ROWS, d_model), lambda i: (i, 0)),
            pl.BlockSpec((None, POST_ROWS, p3d.shape[2]), lambda i: (layer, i, 0)),
        ] + [spec for _, spec in consts],
        out_specs=pl.BlockSpec((POST_ROWS, d_model), lambda i: (i, 0)),
        out_shape=jax.ShapeDtypeStruct((seq, d_model), F32),
        compiler_params=pltpu.CompilerParams(
            dimension_semantics=("arbitrary",), vmem_limit_bytes=VMEM_LIMIT),
        name="post",
    )(x2d, y2d, p3d, *[arr for arr, _ in consts])


def kernel(x, p, even_w_in, even_w_s, even_b_s, even_sgu_ln_g, even_sgu_ln_b, even_w_gk_up, even_b_gk,
           even_gla_norm_g, even_w_out, odd_w_in, hgrn_lb_logits, odd_hgrn_norm_g, odd_w_out,
           ln_mix_g, ln_mix_b, w_ffn_in, w_ffn_out, ln_ffn_g, ln_ffn_b, w_ple_proj, w_ple_gate):
    batch, seq, d_model = x.shape
    depth = p.shape[0]
    assert batch == 1 and seq % POST_ROWS == 0 and seq % MIX_ROWS == 0
    alpha = (2.0 * depth) ** 0.25
    a_width = even_sgu_ln_g.shape[1]
    b_width = d_model - a_width
    key_width = even_w_gk_up.shape[2]
    assert w_ffn_out.shape[1] % FF_CHUNK == 0
    even_in = even_w_in.shape[2]
    even_in_pad = -(-even_in // LANES) * LANES

    rows3 = lambda a: a.reshape(a.shape[0], 1, -1).astype(F32)
    even_w_in_b = jnp.pad(even_w_in, ((0, 0), (0, 0), (0, even_in_pad - even_in))).astype(BF16)
    even_w_gk_b = jnp.pad(even_w_gk_up, ((0, 0), (0, LANES - GK_RANK), (0, 0))).astype(BF16)
    even_bias = jnp.repeat(even_b_s.transpose(0, 2, 1), a_width // A_GROUPS, axis=2)
    even_params = (even_w_in_b, even_w_s, even_bias, rows3(even_sgu_ln_g), rows3(even_sgu_ln_b),
                   even_w_gk_b, rows3(even_b_gk), rows3(even_gla_norm_g))
    odd_w_in_b = odd_w_in.astype(BF16)
    odd_norm_g = rows3(odd_hgrn_norm_g)
    even_w_out_b, odd_w_out_b = even_w_out.astype(BF16), odd_w_out.astype(BF16)
    post_params = (rows3(ln_mix_g), rows3(ln_mix_b), w_ffn_in.astype(BF16), w_ffn_out.astype(BF16),
                   rows3(ln_ffn_g), rows3(ln_ffn_b), w_ple_gate.astype(BF16), w_ple_proj.astype(BF16))

    xs = x.reshape(seq, d_model)
    p3 = p.reshape(depth, seq, -1)
    for layer in range(depth):
        j = layer // 2
        if layer % 2 == 0:
            y = _even_mixer(xs, [_layer_operand(a, j) for a in even_params], a_width, key_width,
                            b_width)
            w_out = _layer_operand(even_w_out_b, j)
        else:
            y = _hgrn_mixer(xs, [_layer_operand(odd_w_in_b, j),
                                 (hgrn_lb_logits.astype(F32), _const_spec(hgrn_lb_logits.shape)),
                                 _layer_operand(odd_norm_g, j)], j)
            w_out = _layer_operand(odd_w_out_b, j)
        xs = _post(xs, y, p3, layer, [w_out] + [_layer_operand(a, layer) for a in post_params],
                   alpha)
    return xs.reshape(batch, seq, d_model)
```

```python
import functools

import jax
import jax.numpy as jnp
from jax import lax
from jax.experimental import pallas as pl
from jax.experimental.pallas import tpu as pltpu

F32 = jnp.float32
BF16 = jnp.bfloat16

LANES = 128
MXU_WIDTH = 256
LN_EPS = 1e-5
RMS_EPS = 1e-6
LOG2_E = 1.4426950408889634
GLA_GATE_NORMALIZER = 16.0
GK_RANK = 16
A_GROUPS = 4
SGU_CHUNK = 128
LIN_CHUNK = 64
MIX_ROWS = 256
POST_ROWS = 512
POST_SPLIT = 2
POST_LAG = 2
FF_CHUNK = 256
PROJ_SLAB = MXU_WIDTH
UNIT = MXU_WIDTH
MAX_LOG2_SPAN = 100.0
VMEM_LIMIT = 56 * 1024 * 1024


def _dot(a, b):
    return jnp.dot(a, b, preferred_element_type=F32)


def _dot_nt(a, b):
    return lax.dot_general(a, b, (((1,), (1,)), ((), ())), preferred_element_type=F32)


def _dot_tn(a, b):
    return lax.dot_general(a, b, (((0,), (0,)), ((), ())), preferred_element_type=F32)


def _layer_norm(x, g, b):
    mu = jnp.mean(x, axis=-1, keepdims=True)
    xc = x - mu
    var = jnp.mean(xc * xc, axis=-1, keepdims=True)
    return xc * lax.rsqrt(var + LN_EPS) * g + b


def _log_sigmoid(x):
    return jnp.minimum(x, 0.0) - jnp.log(1.0 + jnp.exp(-jnp.abs(x)))


def _const_spec(shape):
    nd = len(shape)
    return pl.BlockSpec(shape, lambda i: (0,) * nd, pipeline_mode=pl.Buffered(1))


def _layer_operand(stacked, layer):
    nd = stacked.ndim
    spec = pl.BlockSpec((None,) + stacked.shape[1:], lambda i: (layer,) + (0,) * (nd - 1),
                        pipeline_mode=pl.Buffered(1))
    return stacked, spec


def _causal_mask(c):
    rows = lax.broadcasted_iota(jnp.int32, (c, c), 0)
    cols = lax.broadcasted_iota(jnp.int32, (c, c), 1)
    return cols <= rows


def _cumsum_rows(x):
    tril_b = jnp.where(_causal_mask(x.shape[0]), 1.0, 0.0).astype(BF16)
    hi = x.astype(BF16)
    r1 = x - hi.astype(F32)
    mid = r1.astype(BF16)
    lo = (r1 - mid.astype(F32)).astype(BF16)
    return _dot(tril_b, hi) + _dot(tril_b, mid) + _dot(tril_b, lo)


def _recurrence_prepare(q, k, lf, lf_scale, heads_per_group):
    c, width = q.shape
    b = _cumsum_rows(lf * (lf_scale * LOG2_E))
    b_mid = b[c // 2:c // 2 + 1, :]
    b_last = b[c - 1:c, :]
    span = jnp.maximum(b[0:1, :] - b_mid, b_mid - b_last)
    qe = (q * jnp.exp2(b - b_mid)).astype(BF16)
    ke = (k * jnp.exp2(b_mid - b)).astype(BF16)
    qb = (q * jnp.exp2(b)).astype(BF16)
    kd = (k * jnp.exp2(b_last - b)).astype(BF16)
    dec = jnp.exp2(b_last)
    head_k = LANES // heads_per_group
    lane = lax.broadcasted_iota(jnp.int32, (c, LANES), 1)
    groups = []
    for gi in range(width // LANES):
        sl = slice(gi * LANES, (gi + 1) * LANES)
        heads = []
        for s in range(heads_per_group):
            if heads_per_group == 1:
                heads.append((qe[:, sl], qb[:, sl], kd[:, sl]))
            else:
                in_head = (lane >= s * head_k) & (lane < (s + 1) * head_k)
                zero = jnp.zeros((c, LANES), BF16)
                heads.append(tuple(jnp.where(in_head, a[:, sl], zero) for a in (qe, qb, kd)))
        groups.append(dict(ke=ke[:, sl], dec=dec[:, sl], heads=heads))
    return groups, span


def _recurrence_scores(groups, v_heads):
    c = v_heads[0].shape[0]
    causal = _causal_mask(c)
    h = 0
    for grp in groups:
        grp["p"] = []
        upd = None
        for qe_h, _, kd_h in grp["heads"]:
            p = jnp.where(causal, _dot_nt(qe_h, grp["ke"]), 0.0).astype(BF16)
            grp["p"].append(p)
            u = _dot_tn(v_heads[h], kd_h)
            upd = u if upd is None else upd + u
            h += 1
        grp["upd"] = upd


def _recurrence_outputs(groups, v_heads, st_ref, g0):
    outs = []
    h = 0
    for gi, grp in enumerate(groups):
        st = st_ref[g0 + gi]
        st_b = st.astype(BF16)
        for (_, qb_h, _), p in zip(grp["heads"], grp["p"]):
            outs.append((_dot_nt(qb_h, st_b), _dot(p, v_heads[h])))
            h += 1
        st_ref[g0 + gi] = st * grp["dec"] + grp["upd"]
    return outs


def _exact_scores(q_ref, k_ref, b_ref, heads_per_group):
    c, width = q_ref.shape
    head_k = LANES // heads_per_group
    q = q_ref[...]
    b = b_ref[...]
    col = lax.broadcasted_iota(jnp.int32, (c, c), 1)
    lane = lax.broadcasted_iota(jnp.int32, (c, LANES), 1)
    n_heads = (width // LANES) * heads_per_group

    def body(s, ps):
        k_s = k_ref[pl.ds(s, 1), :]
        b_s = b_ref[pl.ds(s, 1), :]
        w = q * k_s * jnp.exp2(jnp.minimum(b - b_s, 0.0))
        new = []
        for h in range(n_heads):
            gi, sub = divmod(h, heads_per_group)
            wg = w[:, gi * LANES:(gi + 1) * LANES]
            if heads_per_group > 1:
                wg = jnp.where((lane >= sub * head_k) & (lane < (sub + 1) * head_k), wg, 0.0)
            new.append(jnp.where(col == s, jnp.sum(wg, axis=-1, keepdims=True), ps[h]))
        return tuple(new)

    ps = lax.fori_loop(0, c, body, tuple(jnp.zeros((c, c), F32) for _ in range(n_heads)))
    causal = _causal_mask(c)
    return [jnp.where(causal, p, 0.0).astype(BF16) for p in ps]


def _run_recurrence(n_chunks, n_units, prepare, scores, outputs, finish, extra, slot):
    for u in range(n_units):
        prepare(0, u)
    for c in range(n_chunks):
        for u in range(n_units):
            if c + 1 < n_chunks:
                prepare(c + 1, u)
            scores(c, u)
            slot()
        extra(c)
        for u in range(n_units):
            outputs(c, u)
            if u > 0:
                finish(c, u - 1)
            slot()
        finish(c, n_units - 1)


def _recurrent_mixer(n_chunks, n_units, heads_per_group, gates, values, finish, st_ref, oi_ref,
                     exact_refs, extra, slot, flush):
    heads_per_unit = (UNIT // LANES) * heads_per_group
    work, spans = {}, []

    def prepare(c, u):
        q, k, lf, lf_scale = gates(c, u)
        work[c, u], span = _recurrence_prepare(q, k, lf, lf_scale, heads_per_group)
        spans.append(span)

    def scores(c, u):
        _recurrence_scores(work[c, u], values(c, u))

    def outputs(c, u):
        outs = []
        pairs = _recurrence_outputs(work[c, u], values(c, u), st_ref, u * (UNIT // LANES))
        for n, (o_carried, o_intra) in enumerate(pairs):
            oi_ref[c, u * heads_per_unit + n] = o_carried
            outs.append(o_carried + o_intra)
        work[c, u] = outs

    _run_recurrence(n_chunks, n_units, prepare, scores, outputs,
                    lambda c, u: finish(c, u, work.pop((c, u))), extra, slot)
    flush()

    worst = functools.reduce(jnp.maximum, spans)

    @pl.when(jnp.max(worst) > MAX_LOG2_SPAN)
    def _():
        q_ref, k_ref, b_ref = exact_refs
        for c in range(n_chunks):
            for u in range(n_units):
                q, k, lf, lf_scale = gates(c, u)
                q_ref[...] = q
                k_ref[...] = k
                b_ref[...] = _cumsum_rows(lf * (lf_scale * LOG2_E))
                ps = _exact_scores(q_ref, k_ref, b_ref, heads_per_group)
                vs = values(c, u)
                finish(c, u, [oi_ref[c, u * heads_per_unit + n] + _dot(ps[n], vs[n])
                              for n in range(heads_per_unit)])


def _pipelined_mixer_step(x_ref, win_ref, buf_a, buf_b, st_ref, mix, n_slots):
    i = pl.program_id(0)

    @pl.when(i == 0)
    def _():
        st_ref[...] = jnp.zeros_like(st_ref)
        buf_b[...] = jnp.zeros_like(buf_b)

    def step(w_buf, r_buf):
        xb = x_ref[...].astype(BF16)
        width = w_buf.shape[1]
        pending = list(range(0, width, PROJ_SLAB))
        n_slabs = len(pending)
        seen = [0]

        def project_slab():
            lo = pending.pop(0)
            hi = min(lo + PROJ_SLAB, width)
            w_buf[:, lo:hi] = _dot(xb, win_ref[:, lo:hi])

        def slot():
            seen[0] += 1
            while pending and (n_slabs - len(pending)) * n_slots < seen[0] * n_slabs:
                project_slab()

        def flush():
            while pending:
                project_slab()

        mix(r_buf, slot, flush)

    @pl.when(lax.rem(i, 2) == 0)
    def _():
        step(buf_a, buf_b)

    @pl.when(lax.rem(i, 2) == 1)
    def _():
        step(buf_b, buf_a)


def _mixer_call(kern, name, x2d, consts, n_state, n_heads):
    seq, d_model = x2d.shape
    proj_width = consts[0][0].shape[-1]
    n_blocks = seq // MIX_ROWS
    return pl.pallas_call(
        kern,
        grid=(n_blocks + 1,),
        in_specs=[pl.BlockSpec((MIX_ROWS, d_model), lambda i: (jnp.minimum(i, n_blocks - 1), 0))]
        + [spec for _, spec in consts],
        out_specs=pl.BlockSpec((MIX_ROWS, d_model), lambda i: (jnp.maximum(i - 1, 0), 0)),
        out_shape=jax.ShapeDtypeStruct((seq, d_model), BF16),
        scratch_shapes=[
            pltpu.VMEM((MIX_ROWS, proj_width), F32),
            pltpu.VMEM((MIX_ROWS, proj_width), F32),
            pltpu.VMEM((n_state, LANES, LANES), F32),
            pltpu.VMEM((MIX_ROWS // LIN_CHUNK, n_heads, LIN_CHUNK, LANES), F32),
            pltpu.VMEM((LIN_CHUNK, UNIT), F32),
            pltpu.VMEM((LIN_CHUNK, UNIT), F32),
            pltpu.VMEM((LIN_CHUNK, UNIT), F32),
        ],
        compiler_params=pltpu.CompilerParams(
            dimension_semantics=("arbitrary",), vmem_limit_bytes=VMEM_LIMIT),
        name=name,
    )(x2d, *[arr for arr, _ in consts])


def _hgrn_kernel(x_ref, win_ref, lbl_ref, ng_ref, o_ref, buf_a, buf_b, st_ref, oi_ref,
                 eq_ref, ek_ref, eb_ref, *, layer_j, d_model):
    logits = lbl_ref[...]
    e = jnp.exp(logits - jnp.max(logits, axis=0, keepdims=True))
    probs = e / jnp.sum(e, axis=0, keepdims=True)
    lb = jnp.zeros((1, d_model), F32)
    for r in range(1, layer_j + 1):
        lb = lb + probs[r:r + 1, :]
    log_lb = jnp.log(lb)
    log_1m_lb = jnp.log1p(-lb)
    one_m_lb = 1.0 - lb
    rows = x_ref.shape[0]
    heads_per_unit = UNIT // LANES

    def mix(proj_ref, slot, flush):
        def gates(c, u):
            rs = slice(c * LIN_CHUNK, (c + 1) * LIN_CHUNK)
            ls = slice(u * UNIT, (u + 1) * UNIT)
            q = proj_ref[rs, u * UNIT:(u + 1) * UNIT]
            f = proj_ref[rs, d_model + u * UNIT:d_model + (u + 1) * UNIT]
            ef = jnp.exp(-jnp.abs(f))
            one_p_ef = 1.0 + ef
            log_sig = jnp.minimum(f, 0.0) - jnp.log(one_p_ef)
            sig_neg = jnp.where(f >= 0.0, ef, 1.0) / one_p_ef
            a1 = log_lb[:, ls]
            a2 = log_1m_lb[:, ls] + log_sig
            lf = jnp.maximum(a1, a2) + jnp.log(1.0 + jnp.exp(-jnp.abs(a1 - a2)))
            return q * jax.nn.sigmoid(q), one_m_lb[:, ls] * sig_neg, lf, 1.0

        def values(c, u):
            rs = slice(c * LIN_CHUNK, (c + 1) * LIN_CHUNK)
            return [proj_ref[rs, 2 * d_model + h * LANES:2 * d_model + (h + 1) * LANES].astype(BF16)
                    for h in range(u * heads_per_unit, (u + 1) * heads_per_unit)]

        def finish(c, u, outs):
            rs = slice(c * LIN_CHUNK, (c + 1) * LIN_CHUNK)
            ng = ng_ref[...]
            for n, o in enumerate(outs):
                h = u * heads_per_unit + n
                ms = jnp.mean(o * o, axis=-1, keepdims=True)
                g = proj_ref[rs, 3 * d_model + h * LANES:3 * d_model + (h + 1) * LANES]
                y = o * lax.rsqrt(ms + RMS_EPS) * ng * jax.nn.sigmoid(g)
                o_ref[rs, h * LANES:(h + 1) * LANES] = y.astype(o_ref.dtype)

        _recurrent_mixer(rows // LIN_CHUNK, d_model // UNIT, 1, gates, values, finish, st_ref,
                         oi_ref, (eq_ref, ek_ref, eb_ref), lambda c: None, slot, flush)

    n_slots = 2 * (rows // LIN_CHUNK) * (d_model // UNIT)
    _pipelined_mixer_step(x_ref, win_ref, buf_a, buf_b, st_ref, mix, n_slots)


def _hgrn_mixer(x2d, consts, layer_j):
    d_model = x2d.shape[1]
    kern = functools.partial(_hgrn_kernel, layer_j=layer_j, d_model=d_model)
    return _mixer_call(kern, "hgrn_mixer", x2d, consts, d_model // LANES, d_model // LANES)


def _even_kernel(x_ref, win_ref, ws_ref, bs_ref, lng_ref, lnb_ref, wgk_ref, bgk_ref, ng_ref,
                 o_ref, buf_a, buf_b, st_ref, oi_ref, eq_ref, ek_ref, eb_ref, *, a_width, key_width,
                 b_width):
    rows = x_ref.shape[0]
    o_q = 2 * a_width
    o_k = o_q + key_width
    o_v = o_k + key_width
    o_g = o_v + b_width
    o_gk = o_g + b_width
    group_dim = a_width // A_GROUPS
    n_sgu = rows // SGU_CHUNK
    n_chunks = rows // LIN_CHUNK
    n_heads = b_width // LANES
    head_k = key_width // n_heads
    heads_per_group = LANES // head_k
    scale = head_k ** -0.5
    assert key_width == UNIT and n_chunks >= A_GROUPS

    def mix(proj_ref, slot, flush):
        def sgu_group(gi):
            ls = slice(gi * group_dim, (gi + 1) * group_dim)
            v = jax.nn.gelu(proj_ref[:, a_width + gi * group_dim:a_width + (gi + 1) * group_dim])
            vn = _layer_norm(v, lng_ref[:, ls], lnb_ref[:, ls]).astype(BF16)
            rhs = jnp.concatenate(
                [vn[n * SGU_CHUNK:(n + 1) * SGU_CHUNK, :] for n in range(n_sgu)], axis=1)
            w = jnp.where(_causal_mask(SGU_CHUNK), ws_ref[gi], 0.0).astype(BF16)
            mixed = _dot(w, rhs)
            bias = bs_ref[:, ls]
            for n in range(n_sgu):
                rs = slice(n * SGU_CHUNK, (n + 1) * SGU_CHUNK)
                u = jax.nn.gelu(proj_ref[rs, ls])
                y = u * (mixed[:, n * group_dim:(n + 1) * group_dim] + bias)
                o_ref[rs, ls] = y.astype(o_ref.dtype)

        def extra(c):
            if c < A_GROUPS:
                sgu_group(c)
                slot()

        def gates(c, u):
            rs = slice(c * LIN_CHUNK, (c + 1) * LIN_CHUNK)
            gk_low = proj_ref[rs, o_gk:o_gk + LANES].astype(BF16)
            lf = _log_sigmoid(_dot(gk_low, wgk_ref[...]) + bgk_ref[...])
            q = proj_ref[rs, o_q:o_q + key_width] * scale
            k = proj_ref[rs, o_k:o_k + key_width]
            return q, k, lf, 1.0 / GLA_GATE_NORMALIZER

        def values(c, u):
            rs = slice(c * LIN_CHUNK, (c + 1) * LIN_CHUNK)
            return [proj_ref[rs, o_v + h * LANES:o_v + (h + 1) * LANES].astype(BF16)
                    for h in range(n_heads)]

        def finish(c, u, outs):
            rs = slice(c * LIN_CHUNK, (c + 1) * LIN_CHUNK)
            ng = ng_ref[...]
            for h, o in enumerate(outs):
                ms = jnp.mean(o * o, axis=-1, keepdims=True)
                g = proj_ref[rs, o_g + h * LANES:o_g + (h + 1) * LANES]
                y = o * lax.rsqrt(ms + RMS_EPS) * ng * (g * jax.nn.sigmoid(g))
                o_ref[rs, a_width + h * LANES:a_width + (h + 1) * LANES] = y.astype(o_ref.dtype)

        _recurrent_mixer(n_chunks, 1, heads_per_group, gates, values, finish, st_ref, oi_ref,
                         (eq_ref, ek_ref, eb_ref), extra, slot, flush)

    _pipelined_mixer_step(x_ref, win_ref, buf_a, buf_b, st_ref, mix, 2 * n_chunks + A_GROUPS)


def _even_mixer(x2d, consts, a_width, key_width, b_width):
    kern = functools.partial(_even_kernel, a_width=a_width, key_width=key_width, b_width=b_width)
    return _mixer_call(kern, "even_mixer", x2d, consts, key_width // LANES, b_width // LANES)


def _post_kernel(x_ref, y_ref, p_ref, wout_ref, lnmg_ref, lnmb_ref, wfi_ref, wfo_ref,
                 lnfg_ref, lnfb_ref, wpg_ref, wpp_ref, o_ref, *, alpha):
    rows = x_ref.shape[0]
    d_ff = wfo_ref.shape[0]
    chunks = list(range(0, d_ff, FF_CHUNK))
    parts = [slice(n * rows // POST_SPLIT, (n + 1) * rows // POST_SPLIT) for n in range(POST_SPLIT)]
    st = [dict() for _ in parts]

    def out_proj(n):
        st[n]["mixed"] = _dot(y_ref[parts[n], :], wout_ref[...])

    def norm_mix(n):
        x1 = _layer_norm(alpha * x_ref[parts[n], :] + st[n].pop("mixed"), lnmg_ref[...], lnmb_ref[...])
        st[n]["x1"], st[n]["x1b"] = x1, x1.astype(BF16)

    def ffn_in(n, c0):
        x1b = st[n]["x1b"]
        st[n]["h", c0] = (_dot(x1b, wfi_ref[:, c0:c0 + FF_CHUNK]),
                          _dot(x1b, wfi_ref[:, d_ff + c0:d_ff + c0 + FF_CHUNK]))

    def ffn_out(n, k):
        if k + 1 < len(chunks):
            ffn_in(n, chunks[k + 1])
        hg, hu = st[n].pop(("h", chunks[k]))
        act = (hg * jax.nn.sigmoid(hg) * hu).astype(BF16)
        part = _dot(act, wfo_ref[chunks[k]:chunks[k] + FF_CHUNK, :])
        st[n]["acc"] = part if k == 0 else st[n]["acc"] + part

    def norm_ffn(n):
        st[n]["x2"] = _layer_norm(alpha * st[n].pop("x1") + st[n].pop("acc"), lnfg_ref[...],
                                  lnfb_ref[...])

    def embed(n):
        x2 = st[n].pop("x2")
        gate = jax.nn.sigmoid(_dot(x2.astype(BF16), wpg_ref[...]))
        emb = _dot(p_ref[parts[n], :].astype(BF16), wpp_ref[...])
        o_ref[parts[n], :] = x2 + gate * emb

    stages = [out_proj, norm_mix, lambda n: ffn_in(n, chunks[0])]
    stages += [functools.partial(ffn_out, k=k) for k in range(len(chunks))]
    stages += [norm_ffn, embed]
    for t in range(len(stages) + POST_LAG * (POST_SPLIT - 1)):
        for n in range(POST_SPLIT):
            k = t - n * POST_LAG
            if 0 <= k < len(stages):
                stages[k](n)


def _post(x2d, y2d, p3d, layer, consts, alpha):
    seq, d_model = x2d.shape
    kern = functools.partial(_post_kernel, alpha=alpha)
    return pl.pallas_call(
        kern,
        grid=(seq // POST_ROWS,),
        in_specs=[
            pl.BlockSpec((POST_ROWS, d_model), lambda i: (i, 0)),
            pl.BlockSpec((POST_ROWS, d_model), lambda i: (i, 0)),
            pl.BlockSpec((None, POST_ROWS, p3d.shape[2]), lambda i: (layer, i, 0)),
        ] + [spec for _, spec in consts],
        out_specs=pl.BlockSpec((POST_ROWS, d_model), lambda i: (i, 0)),
        out_shape=jax.ShapeDtypeStruct((seq, d_model), F32),
        compiler_params=pltpu.CompilerParams(
            dimension_semantics=("arbitrary",), vmem_limit_bytes=VMEM_LIMIT),
        name="post",
    )(x2d, y2d, p3d, *[arr for arr, _ in consts])


def kernel(x, p, even_w_in, even_w_s, even_b_s, even_sgu_ln_g, even_sgu_ln_b, even_w_gk_up, even_b_gk,
           even_gla_norm_g, even_w_out, odd_w_in, hgrn_lb_logits, odd_hgrn_norm_g, odd_w_out,
           ln_mix_g, ln_mix_b, w_ffn_in, w_ffn_out, ln_ffn_g, ln_ffn_b, w_ple_proj, w_ple_gate):
    batch, seq, d_model = x.shape
    depth = p.shape[0]
    assert batch == 1 and seq % POST_ROWS == 0 and seq % MIX_ROWS == 0
    alpha = (2.0 * depth) ** 0.25
    a_width = even_sgu_ln_g.shape[1]
    b_width = d_model - a_width
    key_width = even_w_gk_up.shape[2]
    assert w_ffn_out.shape[1] % FF_CHUNK == 0
    even_in = even_w_in.shape[2]
    even_in_pad = -(-even_in // LANES) * LANES

    rows3 = lambda a: a.reshape(a.shape[0], 1, -1).astype(F32)
    even_w_in_b = jnp.pad(even_w_in, ((0, 0), (0, 0), (0, even_in_pad - even_in))).astype(BF16)
    even_w_gk_b = jnp.pad(even_w_gk_up, ((0, 0), (0, LANES - GK_RANK), (0, 0))).astype(BF16)
    even_bias = jnp.repeat(even_b_s.transpose(0, 2, 1), a_width // A_GROUPS, axis=2)
    even_params = (even_w_in_b, even_w_s, even_bias, rows3(even_sgu_ln_g), rows3(even_sgu_ln_b),
                   even_w_gk_b, rows3(even_b_gk), rows3(even_gla_norm_g))
    odd_w_in_b = odd_w_in.astype(BF16)
    odd_norm_g = rows3(odd_hgrn_norm_g)
    even_w_out_b, odd_w_out_b = even_w_out.astype(BF16), odd_w_out.astype(BF16)
    post_params = (rows3(ln_mix_g), rows3(ln_mix_b), w_ffn_in.astype(BF16), w_ffn_out.astype(BF16),
                   rows3(ln_ffn_g), rows3(ln_ffn_b), w_ple_gate.astype(BF16), w_ple_proj.astype(BF16))

    xs = x.reshape(seq, d_model)
    p3 = p.reshape(depth, seq, -1)
    for layer in range(depth):
        j = layer // 2
        if layer % 2 == 0:
            y = _even_mixer(xs, [_layer_operand(a, j) for a in even_params], a_width, key_width,
                            b_width)
            w_out = _layer_operand(even_w_out_b, j)
        else:
            y = _hgrn_mixer(xs, [_layer_operand(odd_w_in_b, j),
                                 (hgrn_lb_logits.astype(F32), _const_spec(hgrn_lb_logits.shape)),
                                 _layer_operand(odd_norm_g, j)], j)
            w_out = _layer_operand(odd_w_out_b, j)
        xs = _post(xs, y, p3, layer, [w_out] + [_layer_operand(a, layer) for a in post_params],
                   alpha)
    return xs.reshape(batch, seq, d_model)
```

```python
import functools

import jax
import jax.numpy as jnp
from jax import lax
from jax.experimental import pallas as pl
from jax.experimental.pallas import tpu as pltpu

F32 = jnp.float32
BF16 = jnp.bfloat16

LANES = 128
MXU_WIDTH = 256
LN_EPS = 1e-5
RMS_EPS = 1e-6
LOG2_E = 1.4426950408889634
GLA_GATE_NORMALIZER = 16.0
GK_RANK = 16
A_GROUPS = 4
SGU_CHUNK = 128
LIN_CHUNK = 64
MIX_ROWS = 256
POST_ROWS = 512
POST_SPLIT = 2
POST_LAG = 2
FF_CHUNK = 256
PROJ_SLAB = MXU_WIDTH
UNIT = MXU_WIDTH
MAX_LOG2_SPAN = 100.0
VMEM_LIMIT = 56 * 1024 * 1024


def _dot(a, b):
    return jnp.dot(a, b, preferred_element_type=F32)


def _dot_nt(a, b):
    return lax.dot_general(a, b, (((1,), (1,)), ((), ())), preferred_element_type=F32)


def _dot_tn(a, b):
    return lax.dot_general(a, b, (((0,), (0,)), ((), ())), preferred_element_type=F32)


def _layer_norm(x, g, b):
    mu = jnp.mean(x, axis=-1, keepdims=True)
    xc = x - mu
    var = jnp.mean(xc * xc, axis=-1, keepdims=True)
    return xc * lax.rsqrt(var + LN_EPS) * g + b


def _log_sigmoid(x):
    return jnp.minimum(x, 0.0) - jnp.log(1.0 + jnp.exp(-jnp.abs(x)))


def _const_spec(shape):
    nd = len(shape)
    return pl.BlockSpec(shape, lambda i: (0,) * nd, pipeline_mode=pl.Buffered(1))


def _layer_operand(stacked, layer):
    nd = stacked.ndim
    spec = pl.BlockSpec((None,) + stacked.shape[1:], lambda i: (layer,) + (0,) * (nd - 1),
                        pipeline_mode=pl.Buffered(1))
    return stacked, spec


def _causal_mask(c):
    rows = lax.broadcasted_iota(jnp.int32, (c, c), 0)
    cols = lax.broadcasted_iota(jnp.int32, (c, c), 1)
    return cols <= rows


def _cumsum_rows(x):
    tril_b = jnp.where(_causal_mask(x.shape[0]), 1.0, 0.0).astype(BF16)
    hi = x.astype(BF16)
    lo = (x - hi.astype(F32)).astype(BF16)
    return _dot(tril_b, hi) + _dot(tril_b, lo)


def _recurrence_prepare(q, k, lf, lf_scale):
    c = q.shape[0]
    b = _cumsum_rows(lf * (lf_scale * LOG2_E))
    b_mid = b[c // 2:c // 2 + 1, :]
    b_last = b[c - 1:c, :]
    span = jnp.maximum(b[0:1, :] - b_mid, b_mid - b_last)
    work = dict(
        qe=(q * jnp.exp2(b - b_mid)).astype(BF16),
        ke=(k * jnp.exp2(b_mid - b)).astype(BF16),
        qb=(q * jnp.exp2(b)).astype(BF16),
        kd=(k * jnp.exp2(b_last - b)).astype(BF16),
        dec=jnp.exp2(b_last))
    return work, span


def _only_head(a, h, heads_per_group):
    width = a.shape[1]
    head_k = LANES // heads_per_group
    if heads_per_group == 1:
        zeros = jnp.zeros((a.shape[0], LANES), a.dtype)
        return jnp.concatenate([a[:, g * LANES:(g + 1) * LANES] if g == h else zeros
                                for g in range(width // LANES)], axis=1)
    lane = lax.broadcasted_iota(jnp.int32, a.shape, 1)
    return jnp.where((lane >= h * head_k) & (lane < (h + 1) * head_k), a, jnp.zeros_like(a))


def _block_diagonal(blocks):
    n = len(blocks)
    zeros = jnp.zeros_like(blocks[0])
    return jnp.concatenate(
        [jnp.concatenate([blk if j == i else zeros for j in range(n)], axis=1)
         for i, blk in enumerate(blocks)], axis=0)


def _recurrence_scores(work, v_heads, heads_per_group):
    c = v_heads[0].shape[0]
    n_heads = len(v_heads)
    kblk = jnp.concatenate([_only_head(work["ke"], h, heads_per_group) for h in range(n_heads)],
                           axis=0)
    rows = lax.broadcasted_iota(jnp.int32, (c, n_heads * c), 0)
    cols = lax.broadcasted_iota(jnp.int32, (c, n_heads * c), 1)
    causal = (cols & (c - 1)) <= rows
    work["p"] = jnp.where(causal, _dot_nt(work["qe"], kblk), 0.0).astype(BF16)
    work["upd"] = []
    for gi in range(work["kd"].shape[1] // LANES):
        upd = None
        for s in range(heads_per_group):
            h = gi * heads_per_group + s
            kd_h = _only_head(work["kd"], h, heads_per_group)[:, gi * LANES:(gi + 1) * LANES]
            u = _dot_tn(v_heads[h], kd_h)
            upd = u if upd is None else upd + u
        work["upd"].append(upd)


def _recurrence_outputs(work, v_heads, st_ref, g0, heads_per_group):
    n_heads = len(v_heads)
    n_groups = n_heads // heads_per_group
    states = [st_ref[g0 + gi] for gi in range(n_groups)]
    st_unit = jnp.concatenate([st.astype(BF16) for st in states], axis=1)
    stblk = jnp.concatenate([_only_head(st_unit, h, heads_per_group) for h in range(n_heads)],
                            axis=0)
    o_carried = _dot_nt(work["qb"], stblk)
    o_intra = _dot(work["p"], _block_diagonal(v_heads))
    for gi, st in enumerate(states):
        st_ref[g0 + gi] = st * work["dec"][:, gi * LANES:(gi + 1) * LANES] + work["upd"][gi]
    return [(o_carried[:, h * LANES:(h + 1) * LANES], o_intra[:, h * LANES:(h + 1) * LANES])
            for h in range(n_heads)]


def _exact_scores(q_ref, k_ref, b_ref, heads_per_group):
    c, width = q_ref.shape
    head_k = LANES // heads_per_group
    q = q_ref[...]
    b = b_ref[...]
    col = lax.broadcasted_iota(jnp.int32, (c, c), 1)
    lane = lax.broadcasted_iota(jnp.int32, (c, LANES), 1)
    n_heads = (width // LANES) * heads_per_group

    def body(s, ps):
        k_s = k_ref[pl.ds(s, 1), :]
        b_s = b_ref[pl.ds(s, 1), :]
        w = q * k_s * jnp.exp2(jnp.minimum(b - b_s, 0.0))
        new = []
        for h in range(n_heads):
            gi, sub = divmod(h, heads_per_group)
            wg = w[:, gi * LANES:(gi + 1) * LANES]
            if heads_per_group > 1:
                wg = jnp.where((lane >= sub * head_k) & (lane < (sub + 1) * head_k), wg, 0.0)
            new.append(jnp.where(col == s, jnp.sum(wg, axis=-1, keepdims=True), ps[h]))
        return tuple(new)

    ps = lax.fori_loop(0, c, body, tuple(jnp.zeros((c, c), F32) for _ in range(n_heads)))
    causal = _causal_mask(c)
    return [jnp.where(causal, p, 0.0).astype(BF16) for p in ps]


def _run_recurrence(n_chunks, n_units, prepare, scores, outputs, finish, extra, slot):
    for u in range(n_units):
        prepare(0, u)
    for c in range(n_chunks):
        for u in range(n_units):
            if c + 1 < n_chunks:
                prepare(c + 1, u)
            scores(c, u)
            slot()
        extra(c)
        for u in range(n_units):
            outputs(c, u)
            if u > 0:
                finish(c, u - 1)
            slot()
        finish(c, n_units - 1)


def _recurrent_mixer(n_chunks, n_units, heads_per_group, gates, values, finish, st_ref, oi_ref,
                     exact_refs, extra, slot, flush):
    heads_per_unit = (UNIT // LANES) * heads_per_group
    work, spans = {}, []

    def prepare(c, u):
        q, k, lf, lf_scale = gates(c, u)
        work[c, u], span = _recurrence_prepare(q, k, lf, lf_scale)
        spans.append(span)

    def scores(c, u):
        _recurrence_scores(work[c, u], values(c, u), heads_per_group)

    def outputs(c, u):
        outs = []
        pairs = _recurrence_outputs(work[c, u], values(c, u), st_ref, u * (UNIT // LANES),
                                    heads_per_group)
        for n, (o_carried, o_intra) in enumerate(pairs):
            oi_ref[c, u * heads_per_unit + n] = o_carried
            outs.append(o_carried + o_intra)
        work[c, u] = outs

    _run_recurrence(n_chunks, n_units, prepare, scores, outputs,
                    lambda c, u: finish(c, u, work.pop((c, u))), extra, slot)
    flush()

    worst = functools.reduce(jnp.maximum, spans)

    @pl.when(jnp.max(worst) > MAX_LOG2_SPAN)
    def _():
        q_ref, k_ref, b_ref = exact_refs
        for c in range(n_chunks):
            for u in range(n_units):
                q, k, lf, lf_scale = gates(c, u)
                q_ref[...] = q
                k_ref[...] = k
                b_ref[...] = _cumsum_rows(lf * (lf_scale * LOG2_E))
                ps = _exact_scores(q_ref, k_ref, b_ref, heads_per_group)
                vs = values(c, u)
                finish(c, u, [oi_ref[c, u * heads_per_unit + n] + _dot(ps[n], vs[n])
                              for n in range(heads_per_unit)])


def _pipelined_mixer_step(x_ref, win_ref, buf_a, buf_b, st_ref, mix, n_slots):
    i = pl.program_id(0)

    @pl.when(i == 0)
    def _():
        st_ref[...] = jnp.zeros_like(st_ref)
        buf_b[...] = jnp.zeros_like(buf_b)

    def step(w_buf, r_buf):
        xb = x_ref[...].astype(BF16)
        width = w_buf.shape[1]
        pending = list(range(0, width, PROJ_SLAB))
        n_slabs = len(pending)
        seen = [0]

        def project_slab():
            lo = pending.pop(0)
            hi = min(lo + PROJ_SLAB, width)
            w_buf[:, lo:hi] = _dot(xb, win_ref[:, lo:hi])

        def slot():
            seen[0] += 1
            while pending and (n_slabs - len(pending)) * n_slots < seen[0] * n_slabs:
                project_slab()

        def flush():
            while pending:
                project_slab()

        mix(r_buf, slot, flush)

    @pl.when(lax.rem(i, 2) == 0)
    def _():
        step(buf_a, buf_b)

    @pl.when(lax.rem(i, 2) == 1)
    def _():
        step(buf_b, buf_a)


def _mixer_call(kern, name, x2d, consts, n_state, n_heads):
    seq, d_model = x2d.shape
    proj_width = consts[0][0].shape[-1]
    n_blocks = seq // MIX_ROWS
    return pl.pallas_call(
        kern,
        grid=(n_blocks + 1,),
        in_specs=[pl.BlockSpec((MIX_ROWS, d_model), lambda i: (jnp.minimum(i, n_blocks - 1), 0))]
        + [spec for _, spec in consts],
        out_specs=pl.BlockSpec((MIX_ROWS, d_model), lambda i: (jnp.maximum(i - 1, 0), 0)),
        out_shape=jax.ShapeDtypeStruct((seq, d_model), BF16),
        scratch_shapes=[
            pltpu.VMEM((MIX_ROWS, proj_width), F32),
            pltpu.VMEM((MIX_ROWS, proj_width), F32),
            pltpu.VMEM((n_state, LANES, LANES), F32),
            pltpu.VMEM((MIX_ROWS // LIN_CHUNK, n_heads, LIN_CHUNK, LANES), F32),
            pltpu.VMEM((LIN_CHUNK, UNIT), F32),
            pltpu.VMEM((LIN_CHUNK, UNIT), F32),
            pltpu.VMEM((LIN_CHUNK, UNIT), F32),
        ],
        compiler_params=pltpu.CompilerParams(
            dimension_semantics=("arbitrary",), vmem_limit_bytes=VMEM_LIMIT),
        name=name,
    )(x2d, *[arr for arr, _ in consts])


def _hgrn_kernel(x_ref, win_ref, lbl_ref, ng_ref, o_ref, buf_a, buf_b, st_ref, oi_ref,
                 eq_ref, ek_ref, eb_ref, *, layer_j, d_model):
    logits = lbl_ref[...]
    e = jnp.exp(logits - jnp.max(logits, axis=0, keepdims=True))
    probs = e / jnp.sum(e, axis=0, keepdims=True)
    lb = jnp.zeros((1, d_model), F32)
    for r in range(1, layer_j + 1):
        lb = lb + probs[r:r + 1, :]
    log_lb = jnp.log(lb)
    log_1m_lb = jnp.log1p(-lb)
    one_m_lb = 1.0 - lb
    rows = x_ref.shape[0]
    heads_per_unit = UNIT // LANES

    def mix(proj_ref, slot, flush):
        def gates(c, u):
            rs = slice(c * LIN_CHUNK, (c + 1) * LIN_CHUNK)
            ls = slice(u * UNIT, (u + 1) * UNIT)
            q = proj_ref[rs, u * UNIT:(u + 1) * UNIT]
            f = proj_ref[rs, d_model + u * UNIT:d_model + (u + 1) * UNIT]
            ef = jnp.exp(-jnp.abs(f))
            one_p_ef = 1.0 + ef
            log_sig = jnp.minimum(f, 0.0) - jnp.log(one_p_ef)
            sig_neg = jnp.where(f >= 0.0, ef, 1.0) / one_p_ef
            a1 = log_lb[:, ls]
            a2 = log_1m_lb[:, ls] + log_sig
            lf = jnp.maximum(a1, a2) + jnp.log(1.0 + jnp.exp(-jnp.abs(a1 - a2)))
            return q * jax.nn.sigmoid(q), one_m_lb[:, ls] * sig_neg, lf, 1.0

        def values(c, u):
            rs = slice(c * LIN_CHUNK, (c + 1) * LIN_CHUNK)
            return [proj_ref[rs, 2 * d_model + h * LANES:2 * d_model + (h + 1) * LANES].astype(BF16)
                    for h in range(u * heads_per_unit, (u + 1) * heads_per_unit)]

        def finish(c, u, outs):
            rs = slice(c * LIN_CHUNK, (c + 1) * LIN_CHUNK)
            ng = ng_ref[...]
            for n, o in enumerate(outs):
                h = u * heads_per_unit + n
                ms = jnp.mean(o * o, axis=-1, keepdims=True)
                g = proj_ref[rs, 3 * d_model + h * LANES:3 * d_model + (h + 1) * LANES]
                y = o * lax.rsqrt(ms + RMS_EPS) * ng * jax.nn.sigmoid(g)
                o_ref[rs, h * LANES:(h + 1) * LANES] = y.astype(o_ref.dtype)

        _recurrent_mixer(rows // LIN_CHUNK, d_model // UNIT, 1, gates, values, finish, st_ref,
                         oi_ref, (eq_ref, ek_ref, eb_ref), lambda c: None, slot, flush)

    n_slots = 2 * (rows // LIN_CHUNK) * (d_model // UNIT)
    _pipelined_mixer_step(x_ref, win_ref, buf_a, buf_b, st_ref, mix, n_slots)


def _hgrn_mixer(x2d, consts, layer_j):
    d_model = x2d.shape[1]
    kern = functools.partial(_hgrn_kernel, layer_j=layer_j, d_model=d_model)
    return _mixer_call(kern, "hgrn_mixer", x2d, consts, d_model // LANES, d_model // LANES)


def _even_kernel(x_ref, win_ref, ws_ref, bs_ref, lng_ref, lnb_ref, wgk_ref, bgk_ref, ng_ref,
                 o_ref, buf_a, buf_b, st_ref, oi_ref, eq_ref, ek_ref, eb_ref, *, a_width, key_width,
                 b_width):
    rows = x_ref.shape[0]
    o_q = 2 * a_width
    o_k = o_q + key_width
    o_v = o_k + key_width
    o_g = o_v + b_width
    o_gk = o_g + b_width
    group_dim = a_width // A_GROUPS
    n_sgu = rows // SGU_CHUNK
    n_chunks = rows // LIN_CHUNK
    n_heads = b_width // LANES
    head_k = key_width // n_heads
    heads_per_group = LANES // head_k
    scale = head_k ** -0.5
    assert key_width == UNIT and n_chunks >= A_GROUPS

    def mix(proj_ref, slot, flush):
        def sgu_group(gi):
            ls = slice(gi * group_dim, (gi + 1) * group_dim)
            v = jax.nn.gelu(proj_ref[:, a_width + gi * group_dim:a_width + (gi + 1) * group_dim])
            vn = _layer_norm(v, lng_ref[:, ls], lnb_ref[:, ls]).astype(BF16)
            rhs = jnp.concatenate(
                [vn[n * SGU_CHUNK:(n + 1) * SGU_CHUNK, :] for n in range(n_sgu)], axis=1)
            w = jnp.where(_causal_mask(SGU_CHUNK), ws_ref[gi], 0.0).astype(BF16)
            mixed = _dot(w, rhs)
            bias = bs_ref[:, ls]
            for n in range(n_sgu):
                rs = slice(n * SGU_CHUNK, (n + 1) * SGU_CHUNK)
                u = jax.nn.gelu(proj_ref[rs, ls])
                y = u * (mixed[:, n * group_dim:(n + 1) * group_dim] + bias)
                o_ref[rs, ls] = y.astype(o_ref.dtype)

        def extra(c):
            if c < A_GROUPS:
                sgu_group(c)
                slot()

        def gates(c, u):
            rs = slice(c * LIN_CHUNK, (c + 1) * LIN_CHUNK)
            gk_low = proj_ref[rs, o_gk:o_gk + LANES].astype(BF16)
            lf = _log_sigmoid(_dot(gk_low, wgk_ref[...]) + bgk_ref[...])
            q = proj_ref[rs, o_q:o_q + key_width] * scale
            k = proj_ref[rs, o_k:o_k + key_width]
            return q, k, lf, 1.0 / GLA_GATE_NORMALIZER

        def values(c, u):
            rs = slice(c * LIN_CHUNK, (c + 1) * LIN_CHUNK)
            return [proj_ref[rs, o_v + h * LANES:o_v + (h + 1) * LANES].astype(BF16)
                    for h in range(n_heads)]

        def finish(c, u, outs):
            rs = slice(c * LIN_CHUNK, (c + 1) * LIN_CHUNK)
            ng = ng_ref[...]
            for h, o in enumerate(outs):
                ms = jnp.mean(o * o, axis=-1, keepdims=True)
                g = proj_ref[rs, o_g + h * LANES:o_g + (h + 1) * LANES]
                y = o * lax.rsqrt(ms + RMS_EPS) * ng * (g * jax.nn.sigmoid(g))
                o_ref[rs, a_width + h * LANES:a_width + (h + 1) * LANES] = y.astype(o_ref.dtype)

        _recurrent_mixer(n_chunks, 1, heads_per_group, gates, values, finish, st_ref, oi_ref,
                         (eq_ref, ek_ref, eb_ref), extra, slot, flush)

    _pipelined_mixer_step(x_ref, win_ref, buf_a, buf_b, st_ref, mix, 2 * n_chunks + A_GROUPS)


def _even_mixer(x2d, consts, a_width, key_width, b_width):
    kern = functools.partial(_even_kernel, a_width=a_width, key_width=key_width, b_width=b_width)
    return _mixer_call(kern, "even_mixer", x2d, consts, key_width // LANES, b_width // LANES)


def _post_kernel(x_ref, y_ref, p_ref, wout_ref, lnmg_ref, lnmb_ref, wfi_ref, wfo_ref,
                 lnfg_ref, lnfb_ref, wpg_ref, wpp_ref, o_ref, *, alpha):
    rows = x_ref.shape[0]
    d_ff = wfo_ref.shape[0]
    chunks = list(range(0, d_ff, FF_CHUNK))
    parts = [slice(n * rows // POST_SPLIT, (n + 1) * rows // POST_SPLIT) for n in range(POST_SPLIT)]
    st = [dict() for _ in parts]

    def out_proj(n):
        st[n]["mixed"] = _dot(y_ref[parts[n], :], wout_ref[...])

    def norm_mix(n):
        x1 = _layer_norm(alpha * x_ref[parts[n], :] + st[n].pop("mixed"), lnmg_ref[...], lnmb_ref[...])
        st[n]["x1"], st[n]["x1b"] = x1, x1.astype(BF16)

    def ffn_in(n, c0):
        x1b = st[n]["x1b"]
        st[n]["h", c0] = (_dot(x1b, wfi_ref[:, c0:c0 + FF_CHUNK]),
                          _dot(x1b, wfi_ref[:, d_ff + c0:d_ff + c0 + FF_CHUNK]))

    def ffn_out(n, k):
        if k + 1 < len(chunks):
            ffn_in(n, chunks[k + 1])
        hg, hu = st[n].pop(("h", chunks[k]))
        act = (hg * jax.nn.sigmoid(hg) * hu).astype(BF16)
        part = _dot(act, wfo_ref[chunks[k]:chunks[k] + FF_CHUNK, :])
        st[n]["acc"] = part if k == 0 else st[n]["acc"] + part

    def norm_ffn(n):
        st[n]["x2"] = _layer_norm(alpha * st[n].pop("x1") + st[n].pop("acc"), lnfg_ref[...],
                                  lnfb_ref[...])

    def embed(n):
        x2 = st[n].pop("x2")
        gate = jax.nn.sigmoid(_dot(x2.astype(BF16), wpg_ref[...]))
        emb = _dot(p_ref[parts[n], :].astype(BF16), wpp_ref[...])
        o_ref[parts[n], :] = x2 + gate * emb

    stages = [out_proj, norm_mix, lambda n: ffn_in(n, chunks[0])]
    stages += [functools.partial(ffn_out, k=k) for k in range(len(chunks))]
    stages += [norm_ffn, embed]
    for t in range(len(stages) + POST_LAG * (POST_SPLIT - 1)):
        for n in range(POST_SPLIT):
            k = t - n * POST_LAG
            if 0 <= k < len(stages):
                stages[k](n)


def _post(x2d, y2d, p3d, layer, consts, alpha):
    seq, d_model = x2d.shape
    kern = functools.partial(_post_kernel, alpha=alpha)
    return pl.pallas_call(
        kern,
        grid=(seq // POST_ROWS,),
        in_specs=[
            pl.BlockSpec((POST_ROWS, d_model), lambda i: (i, 0)),
            pl.BlockSpec((POST_ROWS, d_model), lambda i: (i, 0)),
            pl.BlockSpec((None, POST_ROWS, p3d.shape[2]), lambda i: (layer, i, 0)),
        ] + [spec for _, spec in consts],
        out_specs=pl.BlockSpec((POST_ROWS, d_model), lambda i: (i, 0)),
        out_shape=jax.ShapeDtypeStruct((seq, d_model), F32),
        compiler_params=pltpu.CompilerParams(
            dimension_semantics=("arbitrary",), vmem_limit_bytes=VMEM_LIMIT),
        name="post",
    )(x2d, y2d, p3d, *[arr for arr, _ in consts])


def kernel(x, p, even_w_in, even_w_s, even_b_s, even_sgu_ln_g, even_sgu_ln_b, even_w_gk_up, even_b_gk,
           even_gla_norm_g, even_w_out, odd_w_in, hgrn_lb_logits, odd_hgrn_norm_g, odd_w_out,
           ln_mix_g, ln_mix_b, w_ffn_in, w_ffn_out, ln_ffn_g, ln_ffn_b, w_ple_proj, w_ple_gate):
    batch, seq, d_model = x.shape
    depth = p.shape[0]
    assert batch == 1 and seq % POST_ROWS == 0 and seq % MIX_ROWS == 0
    alpha = (2.0 * depth) ** 0.25
    a_width = even_sgu_ln_g.shape[1]
    b_width = d_model - a_width
    key_width = even_w_gk_up.shape[2]
    assert w_ffn_out.shape[1] % FF_CHUNK == 0
    even_in = even_w_in.shape[2]
    even_in_pad = -(-even_in // LANES) * LANES

    rows3 = lambda a: a.reshape(a.shape[0], 1, -1).astype(F32)
    even_w_in_b = jnp.pad(even_w_in, ((0, 0), (0, 0), (0, even_in_pad - even_in))).astype(BF16)
    even_w_gk_b = jnp.pad(even_w_gk_up, ((0, 0), (0, LANES - GK_RANK), (0, 0))).astype(BF16)
    even_bias = jnp.repeat(even_b_s.transpose(0, 2, 1), a_width // A_GROUPS, axis=2)
    even_params = (even_w_in_b, even_w_s, even_bias, rows3(even_sgu_ln_g), rows3(even_sgu_ln_b),
                   even_w_gk_b, rows3(even_b_gk), rows3(even_gla_norm_g))
    odd_w_in_b = odd_w_in.astype(BF16)
    odd_norm_g = rows3(odd_hgrn_norm_g)
    even_w_out_b, odd_w_out_b = even_w_out.astype(BF16), odd_w_out.astype(BF16)
    post_params = (rows3(ln_mix_g), rows3(ln_mix_b), w_ffn_in.astype(BF16), w_ffn_out.astype(BF16),
                   rows3(ln_ffn_g), rows3(ln_ffn_b), w_ple_gate.astype(BF16), w_ple_proj.astype(BF16))

    xs = x.reshape(seq, d_model)
    p3 = p.reshape(depth, seq, -1)
    for layer in range(depth):
        j = layer // 2
        if layer % 2 == 0:
            y = _even_mixer(xs, [_layer_operand(a, j) for a in even_params], a_width, key_width,
                            b_width)
            w_out = _layer_operand(even_w_out_b, j)
        else:
            y = _hgrn_mixer(xs, [_layer_operand(odd_w_in_b, j),
                                 (hgrn_lb_logits.astype(F32), _const_spec(hgrn_lb_logits.shape)),
                                 _layer_operand(odd_norm_g, j)], j)
            w_out = _layer_operand(odd_w_out_b, j)
        xs = _post(xs, y, p3, layer, [w_out] + [_layer_operand(a, layer) for a in post_params],
                   alpha)
    return xs.reshape(batch, seq, d_model)
```

```python
import functools

import jax
import jax.numpy as jnp
from jax import lax
from jax.experimental import pallas as pl
from jax.experimental.pallas import tpu as pltpu

F32 = jnp.float32
BF16 = jnp.bfloat16

LANES = 128
MXU_WIDTH = 256
LN_EPS = 1e-5
RMS_EPS = 1e-6
LOG2_E = 1.4426950408889634
GLA_GATE_NORMALIZER = 16.0
GK_RANK = 16
A_GROUPS = 4
SGU_CHUNK = 128
LIN_CHUNK = 64
MIX_ROWS = 256
POST_ROWS = 1024
POST_SPLIT = 4
POST_LAG = 2
FF_CHUNK = 256
PROJ_SLAB = MXU_WIDTH
UNIT = MXU_WIDTH
MAX_LOG2_SPAN = 100.0
VMEM_LIMIT = 58 * 1024 * 1024


def _dot(a, b):
    return jnp.dot(a, b, preferred_element_type=F32)


def _dot_nt(a, b):
    return lax.dot_general(a, b, (((1,), (1,)), ((), ())), preferred_element_type=F32)


def _dot_tn(a, b):
    return lax.dot_general(a, b, (((0,), (0,)), ((), ())), preferred_element_type=F32)


def _layer_norm(x, g, b):
    mu = jnp.mean(x, axis=-1, keepdims=True)
    xc = x - mu
    var = jnp.mean(xc * xc, axis=-1, keepdims=True)
    return xc * lax.rsqrt(var + LN_EPS) * g + b


def _log_sigmoid(x):
    return jnp.minimum(x, 0.0) - jnp.log(1.0 + jnp.exp(-jnp.abs(x)))


def _const_spec(shape):
    nd = len(shape)
    return pl.BlockSpec(shape, lambda i: (0,) * nd, pipeline_mode=pl.Buffered(1))


def _layer_operand(stacked, layer):
    nd = stacked.ndim
    spec = pl.BlockSpec((None,) + stacked.shape[1:], lambda i: (layer,) + (0,) * (nd - 1),
                        pipeline_mode=pl.Buffered(1))
    return stacked, spec


def _causal_mask(c):
    rows = lax.broadcasted_iota(jnp.int32, (c, c), 0)
    cols = lax.broadcasted_iota(jnp.int32, (c, c), 1)
    return cols <= rows


def _cumsum_rows(x):
    tril_b = jnp.where(_causal_mask(x.shape[0]), 1.0, 0.0).astype(BF16)
    hi = x.astype(BF16)
    lo = (x - hi.astype(F32)).astype(BF16)
    return _dot(tril_b, hi) + _dot(tril_b, lo)


def _recurrence_prepare(q, k, lf, lf_scale):
    c = q.shape[0]
    b = _cumsum_rows(lf * (lf_scale * LOG2_E))
    b_mid = b[c // 2:c // 2 + 1, :]
    b_last = b[c - 1:c, :]
    span = jnp.maximum(b[0:1, :] - b_mid, b_mid - b_last)
    work = dict(
        qe=(q * jnp.exp2(b - b_mid)).astype(BF16),
        ke=(k * jnp.exp2(b_mid - b)).astype(BF16),
        qb=(q * jnp.exp2(b)).astype(BF16),
        kd=(k * jnp.exp2(b_last - b)).astype(BF16),
        dec=jnp.exp2(b_last))
    return work, span


def _only_head(a, h, heads_per_group):
    width = a.shape[1]
    head_k = LANES // heads_per_group
    if heads_per_group == 1:
        zeros = jnp.zeros((a.shape[0], LANES), a.dtype)
        return jnp.concatenate([a[:, g * LANES:(g + 1) * LANES] if g == h else zeros
                                for g in range(width // LANES)], axis=1)
    lane = lax.broadcasted_iota(jnp.int32, a.shape, 1)
    return jnp.where((lane >= h * head_k) & (lane < (h + 1) * head_k), a, jnp.zeros_like(a))


def _block_diagonal(blocks):
    n = len(blocks)
    zeros = jnp.zeros_like(blocks[0])
    return jnp.concatenate(
        [jnp.concatenate([blk if j == i else zeros for j in range(n)], axis=1)
         for i, blk in enumerate(blocks)], axis=0)


def _recurrence_scores(work, v_heads, heads_per_group):
    c = v_heads[0].shape[0]
    n_heads = len(v_heads)
    kblk = jnp.concatenate([_only_head(work["ke"], h, heads_per_group) for h in range(n_heads)],
                           axis=0)
    rows = lax.broadcasted_iota(jnp.int32, (c, n_heads * c), 0)
    cols = lax.broadcasted_iota(jnp.int32, (c, n_heads * c), 1)
    causal = (cols & (c - 1)) <= rows
    work["p"] = jnp.where(causal, _dot_nt(work["qe"], kblk), 0.0).astype(BF16)
    work["upd"] = []
    for gi in range(work["kd"].shape[1] // LANES):
        upd = None
        for s in range(heads_per_group):
            h = gi * heads_per_group + s
            kd_h = _only_head(work["kd"], h, heads_per_group)[:, gi * LANES:(gi + 1) * LANES]
            u = _dot_tn(v_heads[h], kd_h)
            upd = u if upd is None else upd + u
        work["upd"].append(upd)


def _recurrence_outputs(work, v_heads, st_ref, g0, heads_per_group):
    n_heads = len(v_heads)
    n_groups = n_heads // heads_per_group
    states = [st_ref[g0 + gi] for gi in range(n_groups)]
    st_unit = jnp.concatenate([st.astype(BF16) for st in states], axis=1)
    stblk = jnp.concatenate([_only_head(st_unit, h, heads_per_group) for h in range(n_heads)],
                            axis=0)
    o_carried = _dot_nt(work["qb"], stblk)
    o_intra = _dot(work["p"], _block_diagonal(v_heads))
    for gi, st in enumerate(states):
        st_ref[g0 + gi] = st * work["dec"][:, gi * LANES:(gi + 1) * LANES] + work["upd"][gi]
    return [(o_carried[:, h * LANES:(h + 1) * LANES], o_intra[:, h * LANES:(h + 1) * LANES])
            for h in range(n_heads)]


def _exact_scores(q_ref, k_ref, b_ref, heads_per_group):
    c, width = q_ref.shape
    head_k = LANES // heads_per_group
    q = q_ref[...]
    b = b_ref[...]
    col = lax.broadcasted_iota(jnp.int32, (c, c), 1)
    lane = lax.broadcasted_iota(jnp.int32, (c, LANES), 1)
    n_heads = (width // LANES) * heads_per_group

    def body(s, ps):
        k_s = k_ref[pl.ds(s, 1), :]
        b_s = b_ref[pl.ds(s, 1), :]
        w = q * k_s * jnp.exp2(jnp.minimum(b - b_s, 0.0))
        new = []
        for h in range(n_heads):
            gi, sub = divmod(h, heads_per_group)
            wg = w[:, gi * LANES:(gi + 1) * LANES]
            if heads_per_group > 1:
                wg = jnp.where((lane >= sub * head_k) & (lane < (sub + 1) * head_k), wg, 0.0)
            new.append(jnp.where(col == s, jnp.sum(wg, axis=-1, keepdims=True), ps[h]))
        return tuple(new)

    ps = lax.fori_loop(0, c, body, tuple(jnp.zeros((c, c), F32) for _ in range(n_heads)))
    causal = _causal_mask(c)
    return [jnp.where(causal, p, 0.0).astype(BF16) for p in ps]


def _run_recurrence(n_chunks, n_units, prepare, scores, outputs, finish, extra, slot):
    for u in range(n_units):
        prepare(0, u)
    for c in range(n_chunks):
        for u in range(n_units):
            if c + 1 < n_chunks:
                prepare(c + 1, u)
            scores(c, u)
            slot()
        extra(c)
        for u in range(n_units):
            outputs(c, u)
            if u > 0:
                finish(c, u - 1)
            slot()
        finish(c, n_units - 1)


def _recurrent_mixer(n_chunks, n_units, heads_per_group, gates, values, finish, st_ref, oi_ref,
                     exact_refs, extra, slot, flush):
    heads_per_unit = (UNIT // LANES) * heads_per_group
    work, spans = {}, []

    def prepare(c, u):
        q, k, lf, lf_scale = gates(c, u)
        work[c, u], span = _recurrence_prepare(q, k, lf, lf_scale)
        spans.append(span)

    def scores(c, u):
        _recurrence_scores(work[c, u], values(c, u), heads_per_group)

    def outputs(c, u):
        outs = []
        pairs = _recurrence_outputs(work[c, u], values(c, u), st_ref, u * (UNIT // LANES),
                                    heads_per_group)
        for n, (o_carried, o_intra) in enumerate(pairs):
            oi_ref[c, u * heads_per_unit + n] = o_carried
            outs.append(o_carried + o_intra)
        work[c, u] = outs

    _run_recurrence(n_chunks, n_units, prepare, scores, outputs,
                    lambda c, u: finish(c, u, work.pop((c, u))), extra, slot)
    flush()

    worst = functools.reduce(jnp.maximum, spans)

    @pl.when(jnp.max(worst) > MAX_LOG2_SPAN)
    def _():
        q_ref, k_ref, b_ref = exact_refs
        for c in range(n_chunks):
            for u in range(n_units):
                q, k, lf, lf_scale = gates(c, u)
                q_ref[...] = q
                k_ref[...] = k
                b_ref[...] = _cumsum_rows(lf * (lf_scale * LOG2_E))
                ps = _exact_scores(q_ref, k_ref, b_ref, heads_per_group)
                vs = values(c, u)
                finish(c, u, [oi_ref[c, u * heads_per_unit + n] + _dot(ps[n], vs[n])
                              for n in range(heads_per_unit)])


def _pipelined_mixer_step(x_ref, win_ref, buf_a, buf_b, st_ref, mix, n_slots):
    i = pl.program_id(0)

    @pl.when(i == 0)
    def _():
        st_ref[...] = jnp.zeros_like(st_ref)
        buf_b[...] = jnp.zeros_like(buf_b)

    def step(w_buf, r_buf):
        xb = x_ref[...].astype(BF16)
        width = w_buf.shape[1]
        pending = list(range(0, width, PROJ_SLAB))
        n_slabs = len(pending)
        seen = [0]

        def project_slab():
            lo = pending.pop(0)
            hi = min(lo + PROJ_SLAB, width)
            w_buf[:, lo:hi] = _dot(xb, win_ref[:, lo:hi])

        def slot():
            seen[0] += 1
            while pending and (n_slabs - len(pending)) * n_slots < seen[0] * n_slabs:
                project_slab()

        def flush():
            while pending:
                project_slab()

        mix(r_buf, slot, flush)

    @pl.when(lax.rem(i, 2) == 0)
    def _():
        step(buf_a, buf_b)

    @pl.when(lax.rem(i, 2) == 1)
    def _():
        step(buf_b, buf_a)


def _mixer_call(kern, name, x2d, consts, n_state, n_heads):
    seq, d_model = x2d.shape
    proj_width = consts[0][0].shape[-1]
    n_blocks = seq // MIX_ROWS
    return pl.pallas_call(
        kern,
        grid=(n_blocks + 1,),
        in_specs=[pl.BlockSpec((MIX_ROWS, d_model), lambda i: (jnp.minimum(i, n_blocks - 1), 0))]
        + [spec for _, spec in consts],
        out_specs=pl.BlockSpec((MIX_ROWS, d_model), lambda i: (jnp.maximum(i - 1, 0), 0)),
        out_shape=jax.ShapeDtypeStruct((seq, d_model), BF16),
        scratch_shapes=[
            pltpu.VMEM((MIX_ROWS, proj_width), F32),
            pltpu.VMEM((MIX_ROWS, proj_width), F32),
            pltpu.VMEM((n_state, LANES, LANES), F32),
            pltpu.VMEM((MIX_ROWS // LIN_CHUNK, n_heads, LIN_CHUNK, LANES), F32),
            pltpu.VMEM((LIN_CHUNK, UNIT), F32),
            pltpu.VMEM((LIN_CHUNK, UNIT), F32),
            pltpu.VMEM((LIN_CHUNK, UNIT), F32),
        ],
        compiler_params=pltpu.CompilerParams(
            dimension_semantics=("arbitrary",), vmem_limit_bytes=VMEM_LIMIT),
        name=name,
    )(x2d, *[arr for arr, _ in consts])


def _hgrn_kernel(x_ref, win_ref, lbl_ref, ng_ref, o_ref, buf_a, buf_b, st_ref, oi_ref,
                 eq_ref, ek_ref, eb_ref, *, layer_j, d_model):
    logits = lbl_ref[...]
    e = jnp.exp(logits - jnp.max(logits, axis=0, keepdims=True))
    probs = e / jnp.sum(e, axis=0, keepdims=True)
    lb = jnp.zeros((1, d_model), F32)
    for r in range(1, layer_j + 1):
        lb = lb + probs[r:r + 1, :]
    log_lb = jnp.log(lb)
    log_1m_lb = jnp.log1p(-lb)
    one_m_lb = 1.0 - lb
    rows = x_ref.shape[0]
    heads_per_unit = UNIT // LANES

    def mix(proj_ref, slot, flush):
        def gates(c, u):
            rs = slice(c * LIN_CHUNK, (c + 1) * LIN_CHUNK)
            ls = slice(u * UNIT, (u + 1) * UNIT)
            q = proj_ref[rs, u * UNIT:(u + 1) * UNIT]
            f = proj_ref[rs, d_model + u * UNIT:d_model + (u + 1) * UNIT]
            ef = jnp.exp(-jnp.abs(f))
            one_p_ef = 1.0 + ef
            log_sig = jnp.minimum(f, 0.0) - jnp.log(one_p_ef)
            sig_neg = jnp.where(f >= 0.0, ef, 1.0) / one_p_ef
            a1 = log_lb[:, ls]
            a2 = log_1m_lb[:, ls] + log_sig
            lf = jnp.maximum(a1, a2) + jnp.log(1.0 + jnp.exp(-jnp.abs(a1 - a2)))
            return q * jax.nn.sigmoid(q), one_m_lb[:, ls] * sig_neg, lf, 1.0

        def values(c, u):
            rs = slice(c * LIN_CHUNK, (c + 1) * LIN_CHUNK)
            return [proj_ref[rs, 2 * d_model + h * LANES:2 * d_model + (h + 1) * LANES].astype(BF16)
                    for h in range(u * heads_per_unit, (u + 1) * heads_per_unit)]

        def finish(c, u, outs):
            rs = slice(c * LIN_CHUNK, (c + 1) * LIN_CHUNK)
            ng = ng_ref[...]
            for n, o in enumerate(outs):
                h = u * heads_per_unit + n
                ms = jnp.mean(o * o, axis=-1, keepdims=True)
                g = proj_ref[rs, 3 * d_model + h * LANES:3 * d_model + (h + 1) * LANES]
                y = o * lax.rsqrt(ms + RMS_EPS) * ng * jax.nn.sigmoid(g)
                o_ref[rs, h * LANES:(h + 1) * LANES] = y.astype(o_ref.dtype)

        _recurrent_mixer(rows // LIN_CHUNK, d_model // UNIT, 1, gates, values, finish, st_ref,
                         oi_ref, (eq_ref, ek_ref, eb_ref), lambda c: None, slot, flush)

    n_slots = 2 * (rows // LIN_CHUNK) * (d_model // UNIT)
    _pipelined_mixer_step(x_ref, win_ref, buf_a, buf_b, st_ref, mix, n_slots)


def _hgrn_mixer(x2d, consts, layer_j):
    d_model = x2d.shape[1]
    kern = functools.partial(_hgrn_kernel, layer_j=layer_j, d_model=d_model)
    return _mixer_call(kern, "hgrn_mixer", x2d, consts, d_model // LANES, d_model // LANES)


def _even_kernel(x_ref, win_ref, ws_ref, bs_ref, lng_ref, lnb_ref, wgk_ref, bgk_ref, ng_ref,
                 o_ref, buf_a, buf_b, st_ref, oi_ref, eq_ref, ek_ref, eb_ref, *, a_width, key_width,
                 b_width):
    rows = x_ref.shape[0]
    o_q = 2 * a_width
    o_k = o_q + key_width
    o_v = o_k + key_width
    o_g = o_v + b_width
    o_gk = o_g + b_width
    group_dim = a_width // A_GROUPS
    n_sgu = rows // SGU_CHUNK
    n_chunks = rows // LIN_CHUNK
    n_heads = b_width // LANES
    head_k = key_width // n_heads
    heads_per_group = LANES // head_k
    scale = head_k ** -0.5
    assert key_width == UNIT and n_chunks >= A_GROUPS

    def mix(proj_ref, slot, flush):
        def sgu_group(gi):
            ls = slice(gi * group_dim, (gi + 1) * group_dim)
            v = jax.nn.gelu(proj_ref[:, a_width + gi * group_dim:a_width + (gi + 1) * group_dim])
            vn = _layer_norm(v, lng_ref[:, ls], lnb_ref[:, ls]).astype(BF16)
            rhs = jnp.concatenate(
                [vn[n * SGU_CHUNK:(n + 1) * SGU_CHUNK, :] for n in range(n_sgu)], axis=1)
            w = jnp.where(_causal_mask(SGU_CHUNK), ws_ref[gi], 0.0).astype(BF16)
            mixed = _dot(w, rhs)
            bias = bs_ref[:, ls]
            for n in range(n_sgu):
                rs = slice(n * SGU_CHUNK, (n + 1) * SGU_CHUNK)
                u = jax.nn.gelu(proj_ref[rs, ls])
                y = u * (mixed[:, n * group_dim:(n + 1) * group_dim] + bias)
                o_ref[rs, ls] = y.astype(o_ref.dtype)

        def extra(c):
            if c < A_GROUPS:
                sgu_group(c)
                slot()

        def gates(c, u):
            rs = slice(c * LIN_CHUNK, (c + 1) * LIN_CHUNK)
            gk_low = proj_ref[rs, o_gk:o_gk + LANES].astype(BF16)
            lf = _log_sigmoid(_dot(gk_low, wgk_ref[...]) + bgk_ref[...])
            q = proj_ref[rs, o_q:o_q + key_width] * scale
            k = proj_ref[rs, o_k:o_k + key_width]
            return q, k, lf, 1.0 / GLA_GATE_NORMALIZER

        def values(c, u):
            rs = slice(c * LIN_CHUNK, (c + 1) * LIN_CHUNK)
            return [proj_ref[rs, o_v + h * LANES:o_v + (h + 1) * LANES].astype(BF16)
                    for h in range(n_heads)]

        def finish(c, u, outs):
            rs = slice(c * LIN_CHUNK, (c + 1) * LIN_CHUNK)
            ng = ng_ref[...]
            for h, o in enumerate(outs):
                ms = jnp.mean(o * o, axis=-1, keepdims=True)
                g = proj_ref[rs, o_g + h * LANES:o_g + (h + 1) * LANES]
                y = o * lax.rsqrt(ms + RMS_EPS) * ng * (g * jax.nn.sigmoid(g))
                o_ref[rs, a_width + h * LANES:a_width + (h + 1) * LANES] = y.astype(o_ref.dtype)

        _recurrent_mixer(n_chunks, 1, heads_per_group, gates, values, finish, st_ref, oi_ref,
                         (eq_ref, ek_ref, eb_ref), extra, slot, flush)

    _pipelined_mixer_step(x_ref, win_ref, buf_a, buf_b, st_ref, mix, 2 * n_chunks + A_GROUPS)


def _even_mixer(x2d, consts, a_width, key_width, b_width):
    kern = functools.partial(_even_kernel, a_width=a_width, key_width=key_width, b_width=b_width)
    return _mixer_call(kern, "even_mixer", x2d, consts, key_width // LANES, b_width // LANES)


def _post_kernel(x_ref, y_ref, p_ref, wout_ref, lnmg_ref, lnmb_ref, wfi_ref, wfo_ref,
                 lnfg_ref, lnfb_ref, wpg_ref, wpp_ref, o_ref, *, alpha):
    rows = x_ref.shape[0]
    d_ff = wfo_ref.shape[0]
    chunks = list(range(0, d_ff, FF_CHUNK))
    parts = [slice(n * rows // POST_SPLIT, (n + 1) * rows // POST_SPLIT) for n in range(POST_SPLIT)]
    st = [dict() for _ in parts]

    def out_proj(n):
        st[n]["mixed"] = _dot(y_ref[parts[n], :], wout_ref[...])

    def norm_mix(n):
        x1 = _layer_norm(alpha * x_ref[parts[n], :] + st[n].pop("mixed"), lnmg_ref[...], lnmb_ref[...])
        st[n]["x1"], st[n]["x1b"] = x1, x1.astype(BF16)

    def ffn_in(n, c0):
        x1b = st[n]["x1b"]
        st[n]["h", c0] = (_dot(x1b, wfi_ref[:, c0:c0 + FF_CHUNK]),
                          _dot(x1b, wfi_ref[:, d_ff + c0:d_ff + c0 + FF_CHUNK]))

    def ffn_out(n, k):
        if k + 1 < len(chunks):
            ffn_in(n, chunks[k + 1])
        hg, hu = st[n].pop(("h", chunks[k]))
        act = (hg * jax.nn.sigmoid(hg) * hu).astype(BF16)
        part = _dot(act, wfo_ref[chunks[k]:chunks[k] + FF_CHUNK, :])
        st[n]["acc"] = part if k == 0 else st[n]["acc"] + part

    def norm_ffn(n):
        st[n]["x2"] = _layer_norm(alpha * st[n].pop("x1") + st[n].pop("acc"), lnfg_ref[...],
                                  lnfb_ref[...])

    def embed(n):
        x2 = st[n].pop("x2")
        gate = jax.nn.sigmoid(_dot(x2.astype(BF16), wpg_ref[...]))
        emb = _dot(p_ref[parts[n], :].astype(BF16), wpp_ref[...])
        o_ref[parts[n], :] = x2 + gate * emb

    stages = [out_proj, norm_mix, lambda n: ffn_in(n, chunks[0])]
    stages += [functools.partial(ffn_out, k=k) for k in range(len(chunks))]
    stages += [norm_ffn, embed]
    for t in range(len(stages) + POST_LAG * (POST_SPLIT - 1)):
        for n in range(POST_SPLIT):
            k = t - n * POST_LAG
            if 0 <= k < len(stages):
                stages[k](n)


def _post(x2d, y2d, p3d, layer, consts, alpha):
    seq, d_model = x2d.shape
    kern = functools.partial(_post_kernel, alpha=alpha)
    return pl.pallas_call(
        kern,
        grid=(seq // POST_ROWS,),
        in_specs=[
            pl.BlockSpec((POST_ROWS, d_model), lambda i: (i, 0)),
            pl.BlockSpec((POST_ROWS, d_model), lambda i: (i, 0)),
            pl.BlockSpec((None, POST_ROWS, p3d.shape[2]), lambda i: (layer, i, 0)),
        ] + [spec for _, spec in consts],
        out_specs=pl.BlockSpec((POST_ROWS, d_model), lambda i: (i, 0)),
        out_shape=jax.ShapeDtypeStruct((seq, d_model), F32),
        compiler_params=pltpu.CompilerParams(
            dimension_semantics=("arbitrary",), vmem_limit_bytes=VMEM_LIMIT),
        name="post",
    )(x2d, y2d, p3d, *[arr for arr, _ in consts])


def kernel(x, p, even_w_in, even_w_s, even_b_s, even_sgu_ln_g, even_sgu_ln_b, even_w_gk_up, even_b_gk,
           even_gla_norm_g, even_w_out, odd_w_in, hgrn_lb_logits, odd_hgrn_norm_g, odd_w_out,
           ln_mix_g, ln_mix_b, w_ffn_in, w_ffn_out, ln_ffn_g, ln_ffn_b, w_ple_proj, w_ple_gate):
    batch, seq, d_model = x.shape
    depth = p.shape[0]
    assert batch == 1 and seq % POST_ROWS == 0 and seq % MIX_ROWS == 0
    alpha = (2.0 * depth) ** 0.25
    a_width = even_sgu_ln_g.shape[1]
    b_width = d_model - a_width
    key_width = even_w_gk_up.shape[2]
    assert w_ffn_out.shape[1] % FF_CHUNK == 0
    even_in = even_w_in.shape[2]
    even_in_pad = -(-even_in // LANES) * LANES

    rows3 = lambda a: a.reshape(a.shape[0], 1, -1).astype(F32)
    even_w_in_b = jnp.pad(even_w_in, ((0, 0), (0, 0), (0, even_in_pad - even_in))).astype(BF16)
    even_w_gk_b = jnp.pad(even_w_gk_up, ((0, 0), (0, LANES - GK_RANK), (0, 0))).astype(BF16)
    even_bias = jnp.repeat(even_b_s.transpose(0, 2, 1), a_width // A_GROUPS, axis=2)
    even_params = (even_w_in_b, even_w_s, even_bias, rows3(even_sgu_ln_g), rows3(even_sgu_ln_b),
                   even_w_gk_b, rows3(even_b_gk), rows3(even_gla_norm_g))
    odd_w_in_b = odd_w_in.astype(BF16)
    odd_norm_g = rows3(odd_hgrn_norm_g)
    even_w_out_b, odd_w_out_b = even_w_out.astype(BF16), odd_w_out.astype(BF16)
    post_params = (rows3(ln_mix_g), rows3(ln_mix_b), w_ffn_in.astype(BF16), w_ffn_out.astype(BF16),
                   rows3(ln_ffn_g), rows3(ln_ffn_b), w_ple_gate.astype(BF16), w_ple_proj.astype(BF16))

    xs = x.reshape(seq, d_model)
    p3 = p.reshape(depth, seq, -1)
    for layer in range(depth):
        j = layer // 2
        if layer % 2 == 0:
            y = _even_mixer(xs, [_layer_operand(a, j) for a in even_params], a_width, key_width,
                            b_width)
            w_out = _layer_operand(even_w_out_b, j)
        else:
            y = _hgrn_mixer(xs, [_layer_operand(odd_w_in_b, j),
                                 (hgrn_lb_logits.astype(F32), _const_spec(hgrn_lb_logits.shape)),
                                 _layer_operand(odd_norm_g, j)], j)
            w_out = _layer_operand(odd_w_out_b, j)
        xs = _post(xs, y, p3, layer, [w_out] + [_layer_operand(a, layer) for a in post_params],
                   alpha)
    return xs.reshape(batch, seq, d_model)
```

```python
import functools

import jax
import jax.numpy as jnp
from jax import lax
from jax.experimental import pallas as pl
from jax.experimental.pallas import tpu as pltpu

F32 = jnp.float32
BF16 = jnp.bfloat16

LANES = 128
MXU_WIDTH = 256
LN_EPS = 1e-5
RMS_EPS = 1e-6
LOG2_E = 1.4426950408889634
GLA_GATE_NORMALIZER = 16.0
GK_RANK = 16
A_GROUPS = 4
SGU_CHUNK = 128
LIN_CHUNK = 64
ROWS = 256
FF_CHUNK = 256
PROJ_SLAB = MXU_WIDTH
UNIT = MXU_WIDTH
MAX_LOG2_SPAN = 100.0
VMEM_LIMIT = 56 * 1024 * 1024


def _dot(a, b):
    return jnp.dot(a, b, preferred_element_type=F32)


def _dot_nt(a, b):
    return lax.dot_general(a, b, (((1,), (1,)), ((), ())), preferred_element_type=F32)


def _dot_tn(a, b):
    return lax.dot_general(a, b, (((0,), (0,)), ((), ())), preferred_element_type=F32)


def _layer_norm(x, g, b):
    mu = jnp.mean(x, axis=-1, keepdims=True)
    xc = x - mu
    var = jnp.mean(xc * xc, axis=-1, keepdims=True)
    return xc * lax.rsqrt(var + LN_EPS) * g + b


def _log_sigmoid(x):
    return jnp.minimum(x, 0.0) - jnp.log(1.0 + jnp.exp(-jnp.abs(x)))


def _const_spec(shape):
    nd = len(shape)
    return pl.BlockSpec(shape, lambda i: (0,) * nd, pipeline_mode=pl.Buffered(1))


def _layer_operand(stacked, layer):
    nd = stacked.ndim
    spec = pl.BlockSpec((None,) + stacked.shape[1:], lambda i: (layer,) + (0,) * (nd - 1),
                        pipeline_mode=pl.Buffered(1))
    return stacked, spec


def _causal_mask(c):
    rows = lax.broadcasted_iota(jnp.int32, (c, c), 0)
    cols = lax.broadcasted_iota(jnp.int32, (c, c), 1)
    return cols <= rows


def _cumsum_rows(x):
    tril_b = jnp.where(_causal_mask(x.shape[0]), 1.0, 0.0).astype(BF16)
    hi = x.astype(BF16)
    lo = (x - hi.astype(F32)).astype(BF16)
    return _dot(tril_b, hi) + _dot(tril_b, lo)


def _recurrence_prepare(q, k, lf, lf_scale):
    c = q.shape[0]
    b = _cumsum_rows(lf * (lf_scale * LOG2_E))
    b_mid = b[c // 2:c // 2 + 1, :]
    b_last = b[c - 1:c, :]
    span = jnp.maximum(b[0:1, :] - b_mid, b_mid - b_last)
    work = dict(
        qe=(q * jnp.exp2(b - b_mid)).astype(BF16),
        ke=(k * jnp.exp2(b_mid - b)).astype(BF16),
        qb=(q * jnp.exp2(b)).astype(BF16),
        kd=(k * jnp.exp2(b_last - b)).astype(BF16),
        dec=jnp.exp2(b_last))
    return work, span


def _only_head(a, h, heads_per_group):
    width = a.shape[1]
    head_k = LANES // heads_per_group
    if heads_per_group == 1:
        zeros = jnp.zeros((a.shape[0], LANES), a.dtype)
        return jnp.concatenate([a[:, g * LANES:(g + 1) * LANES] if g == h else zeros
                                for g in range(width // LANES)], axis=1)
    lane = lax.broadcasted_iota(jnp.int32, a.shape, 1)
    return jnp.where((lane >= h * head_k) & (lane < (h + 1) * head_k), a, jnp.zeros_like(a))


def _block_diagonal(blocks):
    n = len(blocks)
    zeros = jnp.zeros_like(blocks[0])
    return jnp.concatenate(
        [jnp.concatenate([blk if j == i else zeros for j in range(n)], axis=1)
         for i, blk in enumerate(blocks)], axis=0)


def _recurrence_scores(work, v_heads, heads_per_group):
    c = v_heads[0].shape[0]
    n_heads = len(v_heads)
    kblk = jnp.concatenate([_only_head(work["ke"], h, heads_per_group) for h in range(n_heads)],
                           axis=0)
    rows = lax.broadcasted_iota(jnp.int32, (c, n_heads * c), 0)
    cols = lax.broadcasted_iota(jnp.int32, (c, n_heads * c), 1)
    causal = (cols & (c - 1)) <= rows
    work["p"] = jnp.where(causal, _dot_nt(work["qe"], kblk), 0.0).astype(BF16)
    work["upd"] = []
    for gi in range(work["kd"].shape[1] // LANES):
        upd = None
        for s in range(heads_per_group):
            h = gi * heads_per_group + s
            kd_h = _only_head(work["kd"], h, heads_per_group)[:, gi * LANES:(gi + 1) * LANES]
            u = _dot_tn(v_heads[h], kd_h)
            upd = u if upd is None else upd + u
        work["upd"].append(upd)


def _recurrence_outputs(work, v_heads, st_ref, g0, heads_per_group):
    n_heads = len(v_heads)
    n_groups = n_heads // heads_per_group
    states = [st_ref[g0 + gi] for gi in range(n_groups)]
    st_unit = jnp.concatenate([st.astype(BF16) for st in states], axis=1)
    stblk = jnp.concatenate([_only_head(st_unit, h, heads_per_group) for h in range(n_heads)],
                            axis=0)
    o_carried = _dot_nt(work["qb"], stblk)
    o_intra = _dot(work["p"], _block_diagonal(v_heads))
    for gi, st in enumerate(states):
        st_ref[g0 + gi] = st * work["dec"][:, gi * LANES:(gi + 1) * LANES] + work["upd"][gi]
    return [(o_carried[:, h * LANES:(h + 1) * LANES], o_intra[:, h * LANES:(h + 1) * LANES])
            for h in range(n_heads)]


def _exact_scores(q_ref, k_ref, b_ref, heads_per_group):
    c, width = q_ref.shape
    head_k = LANES // heads_per_group
    q = q_ref[...]
    b = b_ref[...]
    col = lax.broadcasted_iota(jnp.int32, (c, c), 1)
    lane = lax.broadcasted_iota(jnp.int32, (c, LANES), 1)
    n_heads = (width // LANES) * heads_per_group

    def body(s, ps):
        k_s = k_ref[pl.ds(s, 1), :]
        b_s = b_ref[pl.ds(s, 1), :]
        w = q * k_s * jnp.exp2(jnp.minimum(b - b_s, 0.0))
        new = []
        for h in range(n_heads):
            gi, sub = divmod(h, heads_per_group)
            wg = w[:, gi * LANES:(gi + 1) * LANES]
            if heads_per_group > 1:
                wg = jnp.where((lane >= sub * head_k) & (lane < (sub + 1) * head_k), wg, 0.0)
            new.append(jnp.where(col == s, jnp.sum(wg, axis=-1, keepdims=True), ps[h]))
        return tuple(new)

    ps = lax.fori_loop(0, c, body, tuple(jnp.zeros((c, c), F32) for _ in range(n_heads)))
    causal = _causal_mask(c)
    return [jnp.where(causal, p, 0.0).astype(BF16) for p in ps]


def _run_recurrence(n_chunks, n_units, prepare, scores, outputs, finish, extra, slot):
    for u in range(n_units):
        prepare(0, u)
    for c in range(n_chunks):
        for u in range(n_units):
            if c + 1 < n_chunks:
                prepare(c + 1, u)
            scores(c, u)
            slot()
        extra(c)
        for u in range(n_units):
            outputs(c, u)
            if u > 0:
                finish(c, u - 1)
            slot()
        finish(c, n_units - 1)


def _recurrent_mixer(n_chunks, n_units, heads_per_group, gates, values, finish, st_ref, oi_ref,
                     exact_refs, extra, slot, flush):
    heads_per_unit = (UNIT // LANES) * heads_per_group
    work, spans = {}, []

    def prepare(c, u):
        q, k, lf, lf_scale = gates(c, u)
        work[c, u], span = _recurrence_prepare(q, k, lf, lf_scale)
        spans.append(span)

    def scores(c, u):
        _recurrence_scores(work[c, u], values(c, u), heads_per_group)

    def outputs(c, u):
        outs = []
        pairs = _recurrence_outputs(work[c, u], values(c, u), st_ref, u * (UNIT // LANES),
                                    heads_per_group)
        for n, (o_carried, o_intra) in enumerate(pairs):
            oi_ref[c, u * heads_per_unit + n] = o_carried
            outs.append(o_carried + o_intra)
        work[c, u] = outs

    _run_recurrence(n_chunks, n_units, prepare, scores, outputs,
                    lambda c, u: finish(c, u, work.pop((c, u))), extra, slot)
    flush()

    worst = functools.reduce(jnp.maximum, spans)

    @pl.when(jnp.max(worst) > MAX_LOG2_SPAN)
    def _():
        q_ref, k_ref, b_ref = exact_refs
        for c in range(n_chunks):
            for u in range(n_units):
                q, k, lf, lf_scale = gates(c, u)
                q_ref[...] = q
                k_ref[...] = k
                b_ref[...] = _cumsum_rows(lf * (lf_scale * LOG2_E))
                ps = _exact_scores(q_ref, k_ref, b_ref, heads_per_group)
                vs = values(c, u)
                finish(c, u, [oi_ref[c, u * heads_per_unit + n] + _dot(ps[n], vs[n])
                              for n in range(heads_per_unit)])


def _post_stages(x_ref, y_ref, p_ref, post_refs, o_ref, alpha):
    wout_ref, lnmg_ref, lnmb_ref, wfi_ref, wfo_ref, lnfg_ref, lnfb_ref, wpg_ref, wpp_ref = post_refs
    d_ff = wfo_ref.shape[0]
    chunks = list(range(0, d_ff, FF_CHUNK))
    st = {}

    def out_proj():
        st["mixed"] = _dot(y_ref[...], wout_ref[...])

    def norm_mix():
        x1 = _layer_norm(alpha * x_ref[...] + st.pop("mixed"), lnmg_ref[...], lnmb_ref[...])
        st["x1"], st["x1b"] = x1, x1.astype(BF16)

    def ffn_in(c0):
        st["h", c0] = (_dot(st["x1b"], wfi_ref[:, c0:c0 + FF_CHUNK]),
                       _dot(st["x1b"], wfi_ref[:, d_ff + c0:d_ff + c0 + FF_CHUNK]))

    def ffn_out(k):
        if k + 1 < len(chunks):
            ffn_in(chunks[k + 1])
        hg, hu = st.pop(("h", chunks[k]))
        act = (hg * jax.nn.sigmoid(hg) * hu).astype(BF16)
        part = _dot(act, wfo_ref[chunks[k]:chunks[k] + FF_CHUNK, :])
        st["acc"] = part if k == 0 else st["acc"] + part

    def norm_ffn():
        st["x2"] = _layer_norm(alpha * st.pop("x1") + st.pop("acc"), lnfg_ref[...], lnfb_ref[...])

    def embed():
        x2 = st.pop("x2")
        gate = jax.nn.sigmoid(_dot(x2.astype(BF16), wpg_ref[...]))
        emb = _dot(p_ref[...].astype(BF16), wpp_ref[...])
        o_ref[...] = x2 + gate * emb

    stages = [out_proj, norm_mix, lambda: ffn_in(chunks[0])]
    stages += [functools.partial(ffn_out, k) for k in range(len(chunks))]
    return stages + [norm_ffn, embed]


def _pipelined_layer_step(x_ref, xlag_ref, p_ref, win_ref, post_refs, o_ref, bufs, ybufs, st_ref,
                          mix, n_slots, alpha):
    i = pl.program_id(0)

    @pl.when(i == 0)
    def _():
        st_ref[...] = jnp.zeros_like(st_ref)
        bufs[1][...] = jnp.zeros_like(bufs[1])
        ybufs[1][...] = jnp.zeros_like(ybufs[1])

    def step(w_buf, r_buf, y_w, y_r):
        xb = x_ref[...].astype(BF16)
        width = w_buf.shape[1]

        def project_slab(lo):
            hi = min(lo + PROJ_SLAB, width)
            w_buf[:, lo:hi] = _dot(xb, win_ref[:, lo:hi])

        slabs = [functools.partial(project_slab, lo) for lo in range(0, width, PROJ_SLAB)]
        stages = _post_stages(xlag_ref, y_r, p_ref, post_refs, o_ref, alpha)
        fillers = sorted([((n + 0.5) / len(slabs), 0, n, f) for n, f in enumerate(slabs)]
                         + [((n + 0.5) / len(stages), 1, n, f) for n, f in enumerate(stages)],
                         key=lambda t: t[:3])
        pending = [f for _, _, _, f in fillers]
        n_fill = len(pending)
        seen = [0]

        def slot():
            seen[0] += 1
            while pending and (n_fill - len(pending)) * n_slots < seen[0] * n_fill:
                pending.pop(0)()

        def flush():
            while pending:
                pending.pop(0)()

        mix(r_buf, y_w, slot, flush)

    @pl.when(lax.rem(i, 2) == 0)
    def _():
        step(bufs[0], bufs[1], ybufs[0], ybufs[1])

    @pl.when(lax.rem(i, 2) == 1)
    def _():
        step(bufs[1], bufs[0], ybufs[1], ybufs[0])


N_POST_CONSTS = 9


def _layer_call(kern, name, x2d, p3d, layer, consts, n_state, n_heads):
    seq, d_model = x2d.shape
    proj_width = consts[0][0].shape[-1]
    n_blocks = seq // ROWS
    lag = lambda i: jnp.clip(i - 2, 0, n_blocks - 1)
    return pl.pallas_call(
        kern,
        grid=(n_blocks + 2,),
        in_specs=[
            pl.BlockSpec((ROWS, d_model), lambda i: (jnp.minimum(i, n_blocks - 1), 0)),
            pl.BlockSpec((ROWS, d_model), lambda i: (lag(i), 0)),
            pl.BlockSpec((None, ROWS, p3d.shape[2]), lambda i: (layer, lag(i), 0)),
        ] + [spec for _, spec in consts],
        out_specs=pl.BlockSpec((ROWS, d_model), lambda i: (lag(i), 0)),
        out_shape=jax.ShapeDtypeStruct((seq, d_model), F32),
        scratch_shapes=[
            pltpu.VMEM((ROWS, proj_width), F32),
            pltpu.VMEM((ROWS, proj_width), F32),
            pltpu.VMEM((ROWS, d_model), BF16),
            pltpu.VMEM((ROWS, d_model), BF16),
            pltpu.VMEM((n_state, LANES, LANES), F32),
            pltpu.VMEM((ROWS // LIN_CHUNK, n_heads, LIN_CHUNK, LANES), F32),
            pltpu.VMEM((LIN_CHUNK, UNIT), F32),
            pltpu.VMEM((LIN_CHUNK, UNIT), F32),
            pltpu.VMEM((LIN_CHUNK, UNIT), F32),
        ],
        compiler_params=pltpu.CompilerParams(
            dimension_semantics=("arbitrary",), vmem_limit_bytes=VMEM_LIMIT),
        name=name,
    )(x2d, x2d, p3d, *[arr for arr, _ in consts])


def _hgrn_kernel(x_ref, xlag_ref, p_ref, win_ref, lbl_ref, ng_ref, *rest, layer_j, d_model, alpha):
    post_refs = rest[:N_POST_CONSTS]
    o_ref, buf_a, buf_b, y_a, y_b, st_ref, oi_ref, eq_ref, ek_ref, eb_ref = rest[N_POST_CONSTS:]
    logits = lbl_ref[...]
    e = jnp.exp(logits - jnp.max(logits, axis=0, keepdims=True))
    probs = e / jnp.sum(e, axis=0, keepdims=True)
    lb = jnp.zeros((1, d_model), F32)
    for r in range(1, layer_j + 1):
        lb = lb + probs[r:r + 1, :]
    log_lb = jnp.log(lb)
    log_1m_lb = jnp.log1p(-lb)
    one_m_lb = 1.0 - lb
    rows = x_ref.shape[0]
    heads_per_unit = UNIT // LANES

    def mix(proj_ref, y_ref, slot, flush):
        def gates(c, u):
            rs = slice(c * LIN_CHUNK, (c + 1) * LIN_CHUNK)
            ls = slice(u * UNIT, (u + 1) * UNIT)
            q = proj_ref[rs, u * UNIT:(u + 1) * UNIT]
            f = proj_ref[rs, d_model + u * UNIT:d_model + (u + 1) * UNIT]
            ef = jnp.exp(-jnp.abs(f))
            one_p_ef = 1.0 + ef
            log_sig = jnp.minimum(f, 0.0) - jnp.log(one_p_ef)
            sig_neg = jnp.where(f >= 0.0, ef, 1.0) / one_p_ef
            a1 = log_lb[:, ls]
            a2 = log_1m_lb[:, ls] + log_sig
            lf = jnp.maximum(a1, a2) + jnp.log(1.0 + jnp.exp(-jnp.abs(a1 - a2)))
            return q * jax.nn.sigmoid(q), one_m_lb[:, ls] * sig_neg, lf, 1.0

        def values(c, u):
            rs = slice(c * LIN_CHUNK, (c + 1) * LIN_CHUNK)
            return [proj_ref[rs, 2 * d_model + h * LANES:2 * d_model + (h + 1) * LANES].astype(BF16)
                    for h in range(u * heads_per_unit, (u + 1) * heads_per_unit)]

        def finish(c, u, outs):
            rs = slice(c * LIN_CHUNK, (c + 1) * LIN_CHUNK)
            ng = ng_ref[...]
            for n, o in enumerate(outs):
                h = u * heads_per_unit + n
                ms = jnp.mean(o * o, axis=-1, keepdims=True)
                g = proj_ref[rs, 3 * d_model + h * LANES:3 * d_model + (h + 1) * LANES]
                y = o * lax.rsqrt(ms + RMS_EPS) * ng * jax.nn.sigmoid(g)
                y_ref[rs, h * LANES:(h + 1) * LANES] = y.astype(y_ref.dtype)

        _recurrent_mixer(rows // LIN_CHUNK, d_model // UNIT, 1, gates, values, finish, st_ref,
                         oi_ref, (eq_ref, ek_ref, eb_ref), lambda c: None, slot, flush)

    n_slots = 2 * (rows // LIN_CHUNK) * (d_model // UNIT)
    _pipelined_layer_step(x_ref, xlag_ref, p_ref, win_ref, post_refs, o_ref, (buf_a, buf_b),
                          (y_a, y_b), st_ref, mix, n_slots, alpha)


def _hgrn_layer(x2d, p3d, layer, consts, layer_j, alpha):
    d_model = x2d.shape[1]
    kern = functools.partial(_hgrn_kernel, layer_j=layer_j, d_model=d_model, alpha=alpha)
    return _layer_call(kern, "hgrn_layer", x2d, p3d, layer, consts, d_model // LANES,
                       d_model // LANES)


def _even_kernel(x_ref, xlag_ref, p_ref, win_ref, ws_ref, bs_ref, lng_ref, lnb_ref, wgk_ref, bgk_ref,
                 ng_ref, *rest, a_width, key_width, b_width, alpha):
    post_refs = rest[:N_POST_CONSTS]
    o_ref, buf_a, buf_b, y_a, y_b, st_ref, oi_ref, eq_ref, ek_ref, eb_ref = rest[N_POST_CONSTS:]
    rows = x_ref.shape[0]
    o_q = 2 * a_width
    o_k = o_q + key_width
    o_v = o_k + key_width
    o_g = o_v + b_width
    o_gk = o_g + b_width
    group_dim = a_width // A_GROUPS
    n_sgu = rows // SGU_CHUNK
    n_chunks = rows // LIN_CHUNK
    n_heads = b_width // LANES
    head_k = key_width // n_heads
    heads_per_group = LANES // head_k
    scale = head_k ** -0.5
    assert key_width == UNIT and n_chunks >= A_GROUPS

    def mix(proj_ref, y_ref, slot, flush):
        def sgu_group(gi):
            ls = slice(gi * group_dim, (gi + 1) * group_dim)
            v = jax.nn.gelu(proj_ref[:, a_width + gi * group_dim:a_width + (gi + 1) * group_dim])
            vn = _layer_norm(v, lng_ref[:, ls], lnb_ref[:, ls]).astype(BF16)
            rhs = jnp.concatenate(
                [vn[n * SGU_CHUNK:(n + 1) * SGU_CHUNK, :] for n in range(n_sgu)], axis=1)
            w = jnp.where(_causal_mask(SGU_CHUNK), ws_ref[gi], 0.0).astype(BF16)
            mixed = _dot(w, rhs)
            bias = bs_ref[:, ls]
            for n in range(n_sgu):
                rs = slice(n * SGU_CHUNK, (n + 1) * SGU_CHUNK)
                u = jax.nn.gelu(proj_ref[rs, ls])
                y = u * (mixed[:, n * group_dim:(n + 1) * group_dim] + bias)
                y_ref[rs, ls] = y.astype(y_ref.dtype)

        def extra(c):
            if c < A_GROUPS:
                sgu_group(c)
                slot()

        def gates(c, u):
            rs = slice(c * LIN_CHUNK, (c + 1) * LIN_CHUNK)
            gk_low = proj_ref[rs, o_gk:o_gk + LANES].astype(BF16)
            lf = _log_sigmoid(_dot(gk_low, wgk_ref[...]) + bgk_ref[...])
            q = proj_ref[rs, o_q:o_q + key_width] * scale
            k = proj_ref[rs, o_k:o_k + key_width]
            return q, k, lf, 1.0 / GLA_GATE_NORMALIZER

        def values(c, u):
            rs = slice(c * LIN_CHUNK, (c + 1) * LIN_CHUNK)
            return [proj_ref[rs, o_v + h * LANES:o_v + (h + 1) * LANES].astype(BF16)
                    for h in range(n_heads)]

        def finish(c, u, outs):
            rs = slice(c * LIN_CHUNK, (c + 1) * LIN_CHUNK)
            ng = ng_ref[...]
            for h, o in enumerate(outs):
                ms = jnp.mean(o * o, axis=-1, keepdims=True)
                g = proj_ref[rs, o_g + h * LANES:o_g + (h + 1) * LANES]
                y = o * lax.rsqrt(ms + RMS_EPS) * ng * (g * jax.nn.sigmoid(g))
                y_ref[rs, a_width + h * LANES:a_width + (h + 1) * LANES] = y.astype(y_ref.dtype)

        _recurrent_mixer(n_chunks, 1, heads_per_group, gates, values, finish, st_ref, oi_ref,
                         (eq_ref, ek_ref, eb_ref), extra, slot, flush)

    _pipelined_layer_step(x_ref, xlag_ref, p_ref, win_ref, post_refs, o_ref, (buf_a, buf_b),
                          (y_a, y_b), st_ref, mix, 2 * n_chunks + A_GROUPS, alpha)


def _even_layer(x2d, p3d, layer, consts, a_width, key_width, b_width, alpha):
    kern = functools.partial(_even_kernel, a_width=a_width, key_width=key_width, b_width=b_width,
                             alpha=alpha)
    return _layer_call(kern, "even_layer", x2d, p3d, layer, consts, key_width // LANES,
                       b_width // LANES)


def kernel(x, p, even_w_in, even_w_s, even_b_s, even_sgu_ln_g, even_sgu_ln_b, even_w_gk_up, even_b_gk,
           even_gla_norm_g, even_w_out, odd_w_in, hgrn_lb_logits, odd_hgrn_norm_g, odd_w_out,
           ln_mix_g, ln_mix_b, w_ffn_in, w_ffn_out, ln_ffn_g, ln_ffn_b, w_ple_proj, w_ple_gate):
    batch, seq, d_model = x.shape
    depth = p.shape[0]
    assert batch == 1 and seq % ROWS == 0
    alpha = (2.0 * depth) ** 0.25
    a_width = even_sgu_ln_g.shape[1]
    b_width = d_model - a_width
    key_width = even_w_gk_up.shape[2]
    assert w_ffn_out.shape[1] % FF_CHUNK == 0
    even_in = even_w_in.shape[2]
    even_in_pad = -(-even_in // LANES) * LANES

    rows3 = lambda a: a.reshape(a.shape[0], 1, -1).astype(F32)
    even_w_in_b = jnp.pad(even_w_in, ((0, 0), (0, 0), (0, even_in_pad - even_in))).astype(BF16)
    even_w_gk_b = jnp.pad(even_w_gk_up, ((0, 0), (0, LANES - GK_RANK), (0, 0))).astype(BF16)
    even_bias = jnp.repeat(even_b_s.transpose(0, 2, 1), a_width // A_GROUPS, axis=2)
    even_params = (even_w_in_b, even_w_s, even_bias, rows3(even_sgu_ln_g), rows3(even_sgu_ln_b),
                   even_w_gk_b, rows3(even_b_gk), rows3(even_gla_norm_g))
    odd_w_in_b = odd_w_in.astype(BF16)
    odd_norm_g = rows3(odd_hgrn_norm_g)
    even_w_out_b, odd_w_out_b = even_w_out.astype(BF16), odd_w_out.astype(BF16)
    post_params = (rows3(ln_mix_g), rows3(ln_mix_b), w_ffn_in.astype(BF16), w_ffn_out.astype(BF16),
                   rows3(ln_ffn_g), rows3(ln_ffn_b), w_ple_gate.astype(BF16), w_ple_proj.astype(BF16))

    xs = x.reshape(seq, d_model)
    p3 = p.reshape(depth, seq, -1)
    for layer in range(depth):
        j = layer // 2
        post = [_layer_operand(a, layer) for a in post_params]
        if layer % 2 == 0:
            consts = ([_layer_operand(a, j) for a in even_params]
                      + [_layer_operand(even_w_out_b, j)] + post)
            xs = _even_layer(xs, p3, layer, consts, a_width, key_width, b_width, alpha)
        else:
            consts = ([_layer_operand(odd_w_in_b, j),
                       (hgrn_lb_logits.astype(F32), _const_spec(hgrn_lb_logits.shape)),
                       _layer_operand(odd_norm_g, j), _layer_operand(odd_w_out_b, j)] + post)
            xs = _hgrn_layer(xs, p3, layer, consts, j, alpha)
    return xs.reshape(batch, seq, d_model)
```

```python
import functools

import jax
import jax.numpy as jnp
from jax import lax
from jax.experimental import pallas as pl
from jax.experimental.pallas import tpu as pltpu

F32 = jnp.float32
BF16 = jnp.bfloat16

LANES = 128
BF16_ROWS = 16
MXU_WIDTH = 256
LN_EPS = 1e-5
RMS_EPS = 1e-6
LOG2_E = 1.4426950408889634
GLA_GATE_NORMALIZER = 16.0
GK_RANK = 16
A_GROUPS = 4
SGU_CHUNK = 128
LIN_CHUNK = 64
MIX_ROWS = 256
POST_ROWS = 512
POST_SPLIT = 2
POST_LAG = 2
FF_CHUNK = 256
PROJ_SLAB = MXU_WIDTH
UNIT = MXU_WIDTH
MAX_LOG2_SPAN = 100.0
VMEM_LIMIT = 56 * 1024 * 1024


def _dot(a, b):
    return jnp.dot(a, b, preferred_element_type=F32)


def _dot_nt(a, b):
    return lax.dot_general(a, b, (((1,), (1,)), ((), ())), preferred_element_type=F32)


def _dot_tn(a, b):
    return lax.dot_general(a, b, (((0,), (0,)), ((), ())), preferred_element_type=F32)


def _layer_norm(x, g, b):
    mu = jnp.mean(x, axis=-1, keepdims=True)
    xc = x - mu
    var = jnp.mean(xc * xc, axis=-1, keepdims=True)
    return xc * lax.rsqrt(var + LN_EPS) * g + b


def _log_sigmoid(x):
    return jnp.minimum(x, 0.0) - jnp.log(1.0 + jnp.exp(-jnp.abs(x)))


def _const_spec(shape):
    nd = len(shape)
    return pl.BlockSpec(shape, lambda i: (0,) * nd, pipeline_mode=pl.Buffered(1))


def _layer_operand(stacked, layer):
    nd = stacked.ndim
    spec = pl.BlockSpec((None,) + stacked.shape[1:], lambda i: (layer,) + (0,) * (nd - 1),
                        pipeline_mode=pl.Buffered(1))
    return stacked, spec


def _causal_mask(c):
    rows = lax.broadcasted_iota(jnp.int32, (c, c), 0)
    cols = lax.broadcasted_iota(jnp.int32, (c, c), 1)
    return cols <= rows


def _cumsum_rows(x):
    tril_b = jnp.where(_causal_mask(x.shape[0]), 1.0, 0.0).astype(BF16)
    hi = x.astype(BF16)
    lo = (x - hi.astype(F32)).astype(BF16)
    return _dot(tril_b, hi) + _dot(tril_b, lo)


def _recurrence_prepare(q, k, lf, lf_scale):
    c = q.shape[0]
    b = _cumsum_rows(lf * (lf_scale * LOG2_E))
    b_mid = b[c // 2:c // 2 + 1, :]
    b_last = b[c - 1:c, :]
    span = jnp.maximum(b[0:1, :] - b_mid, b_mid - b_last)
    work = dict(
        qe=(q * jnp.exp2(b - b_mid)).astype(BF16),
        ke=(k * jnp.exp2(b_mid - b)).astype(BF16),
        qb=(q * jnp.exp2(b)).astype(BF16),
        kd=(k * jnp.exp2(b_last - b)).astype(BF16),
        dec=jnp.exp2(b_last))
    return work, span


def _only_head(a, h, heads_per_group):
    width = a.shape[1]
    head_k = LANES // heads_per_group
    if heads_per_group == 1:
        zeros = jnp.zeros((a.shape[0], LANES), a.dtype)
        return jnp.concatenate([a[:, g * LANES:(g + 1) * LANES] if g == h else zeros
                                for g in range(width // LANES)], axis=1)
    lane = lax.broadcasted_iota(jnp.int32, a.shape, 1)
    return jnp.where((lane >= h * head_k) & (lane < (h + 1) * head_k), a, jnp.zeros_like(a))


def _block_diagonal(blocks):
    n = len(blocks)
    zeros = jnp.zeros_like(blocks[0])
    return jnp.concatenate(
        [jnp.concatenate([blk if j == i else zeros for j in range(n)], axis=1)
         for i, blk in enumerate(blocks)], axis=0)


def _recurrence_scores(work, v_heads, heads_per_group):
    c = v_heads[0].shape[0]
    n_heads = len(v_heads)
    kblk = jnp.concatenate([_only_head(work["ke"], h, heads_per_group) for h in range(n_heads)],
                           axis=0)
    rows = lax.broadcasted_iota(jnp.int32, (c, n_heads * c), 0)
    cols = lax.broadcasted_iota(jnp.int32, (c, n_heads * c), 1)
    causal = (cols & (c - 1)) <= rows
    work["p"] = jnp.where(causal, _dot_nt(work["qe"], kblk), 0.0).astype(BF16)
    work["upd"] = []
    for gi in range(work["kd"].shape[1] // LANES):
        upd = None
        for s in range(heads_per_group):
            h = gi * heads_per_group + s
            kd_h = _only_head(work["kd"], h, heads_per_group)[:, gi * LANES:(gi + 1) * LANES]
            u = _dot_tn(v_heads[h], kd_h)
            upd = u if upd is None else upd + u
        work["upd"].append(upd)


def _recurrence_outputs(work, v_heads, st_ref, g0, heads_per_group):
    n_heads = len(v_heads)
    n_groups = n_heads // heads_per_group
    states = [st_ref[g0 + gi] for gi in range(n_groups)]
    st_unit = jnp.concatenate([st.astype(BF16) for st in states], axis=1)
    stblk = jnp.concatenate([_only_head(st_unit, h, heads_per_group) for h in range(n_heads)],
                            axis=0)
    o_carried = _dot_nt(work["qb"], stblk)
    o_intra = _dot(work["p"], _block_diagonal(v_heads))
    for gi, st in enumerate(states):
        st_ref[g0 + gi] = st * work["dec"][:, gi * LANES:(gi + 1) * LANES] + work["upd"][gi]
    return [(o_carried[:, h * LANES:(h + 1) * LANES], o_intra[:, h * LANES:(h + 1) * LANES])
            for h in range(n_heads)]


def _exact_scores(q_ref, k_ref, b_ref, heads_per_group):
    c, width = q_ref.shape
    head_k = LANES // heads_per_group
    q = q_ref[...]
    b = b_ref[...]
    col = lax.broadcasted_iota(jnp.int32, (c, c), 1)
    lane = lax.broadcasted_iota(jnp.int32, (c, LANES), 1)
    n_heads = (width // LANES) * heads_per_group

    def body(s, ps):
        k_s = k_ref[pl.ds(s, 1), :]
        b_s = b_ref[pl.ds(s, 1), :]
        w = q * k_s * jnp.exp2(jnp.minimum(b - b_s, 0.0))
        new = []
        for h in range(n_heads):
            gi, sub = divmod(h, heads_per_group)
            wg = w[:, gi * LANES:(gi + 1) * LANES]
            if heads_per_group > 1:
                wg = jnp.where((lane >= sub * head_k) & (lane < (sub + 1) * head_k), wg, 0.0)
            new.append(jnp.where(col == s, jnp.sum(wg, axis=-1, keepdims=True), ps[h]))
        return tuple(new)

    ps = lax.fori_loop(0, c, body, tuple(jnp.zeros((c, c), F32) for _ in range(n_heads)))
    causal = _causal_mask(c)
    return [jnp.where(causal, p, 0.0).astype(BF16) for p in ps]


def _run_recurrence(n_chunks, n_units, prepare, scores, outputs, finish, extra, slot):
    for u in range(n_units):
        prepare(0, u)
    for c in range(n_chunks):
        for u in range(n_units):
            if c + 1 < n_chunks:
                prepare(c + 1, u)
            scores(c, u)
            slot()
        extra(c)
        for u in range(n_units):
            outputs(c, u)
            if u > 0:
                finish(c, u - 1)
            slot()
        finish(c, n_units - 1)


def _recurrent_mixer(n_chunks, n_units, heads_per_group, gates, values, finish, st_ref, oi_ref,
                     exact_refs, extra, slot, flush):
    heads_per_unit = (UNIT // LANES) * heads_per_group
    work, spans = {}, []

    def prepare(c, u):
        q, k, lf, lf_scale = gates(c, u)
        work[c, u], span = _recurrence_prepare(q, k, lf, lf_scale)
        spans.append(span)

    def scores(c, u):
        _recurrence_scores(work[c, u], values(c, u), heads_per_group)

    def outputs(c, u):
        outs = []
        pairs = _recurrence_outputs(work[c, u], values(c, u), st_ref, u * (UNIT // LANES),
                                    heads_per_group)
        for n, (o_carried, o_intra) in enumerate(pairs):
            oi_ref[c, u * heads_per_unit + n] = o_carried
            outs.append(o_carried + o_intra)
        work[c, u] = outs

    _run_recurrence(n_chunks, n_units, prepare, scores, outputs,
                    lambda c, u: finish(c, u, work.pop((c, u))), extra, slot)
    flush()

    worst = functools.reduce(jnp.maximum, spans)

    @pl.when(jnp.max(worst) > MAX_LOG2_SPAN)
    def _():
        q_ref, k_ref, b_ref = exact_refs
        for c in range(n_chunks):
            for u in range(n_units):
                q, k, lf, lf_scale = gates(c, u)
                q_ref[...] = q
                k_ref[...] = k
                b_ref[...] = _cumsum_rows(lf * (lf_scale * LOG2_E))
                ps = _exact_scores(q_ref, k_ref, b_ref, heads_per_group)
                vs = values(c, u)
                finish(c, u, [oi_ref[c, u * heads_per_unit + n] + _dot(ps[n], vs[n])
                              for n in range(heads_per_unit)])


def _pipelined_mixer_step(x_ref, win_ref, buf_a, buf_b, st_ref, mix, n_slots):
    i = pl.program_id(0)

    @pl.when(i == 0)
    def _():
        st_ref[...] = jnp.zeros_like(st_ref)
        buf_b[...] = jnp.zeros_like(buf_b)

    def step(w_buf, r_buf):
        xb = x_ref[...].astype(BF16)
        width = w_buf.shape[1]
        pending = list(range(0, width, PROJ_SLAB))
        n_slabs = len(pending)
        seen = [0]

        def project_slab():
            lo = pending.pop(0)
            hi = min(lo + PROJ_SLAB, width)
            w_buf[:, lo:hi] = _dot(xb, win_ref[:, lo:hi])

        def slot():
            seen[0] += 1
            while pending and (n_slabs - len(pending)) * n_slots < seen[0] * n_slabs:
                project_slab()

        def flush():
            while pending:
                project_slab()

        mix(r_buf, slot, flush)

    @pl.when(lax.rem(i, 2) == 0)
    def _():
        step(buf_a, buf_b)

    @pl.when(lax.rem(i, 2) == 1)
    def _():
        step(buf_b, buf_a)


def _mixer_call(kern, name, x2d, consts, n_state, n_heads):
    seq, d_model = x2d.shape
    proj_width = consts[0][0].shape[-1]
    n_blocks = seq // MIX_ROWS
    return pl.pallas_call(
        kern,
        grid=(n_blocks + 1,),
        in_specs=[pl.BlockSpec((MIX_ROWS, d_model), lambda i: (jnp.minimum(i, n_blocks - 1), 0))]
        + [spec for _, spec in consts],
        out_specs=pl.BlockSpec((MIX_ROWS, d_model), lambda i: (jnp.maximum(i - 1, 0), 0)),
        out_shape=jax.ShapeDtypeStruct((seq, d_model), BF16),
        scratch_shapes=[
            pltpu.VMEM((MIX_ROWS, proj_width), F32),
            pltpu.VMEM((MIX_ROWS, proj_width), F32),
            pltpu.VMEM((n_state, LANES, LANES), F32),
            pltpu.VMEM((MIX_ROWS // LIN_CHUNK, n_heads, LIN_CHUNK, LANES), F32),
            pltpu.VMEM((LIN_CHUNK, UNIT), F32),
            pltpu.VMEM((LIN_CHUNK, UNIT), F32),
            pltpu.VMEM((LIN_CHUNK, UNIT), F32),
        ],
        compiler_params=pltpu.CompilerParams(
            dimension_semantics=("arbitrary",), vmem_limit_bytes=VMEM_LIMIT),
        name=name,
    )(x2d, *[arr for arr, _ in consts])


def _hgrn_kernel(x_ref, win_ref, lbl_ref, ng_ref, o_ref, buf_a, buf_b, st_ref, oi_ref,
                 eq_ref, ek_ref, eb_ref, *, layer_j, d_model):
    logits = lbl_ref[...]
    e = jnp.exp(logits - jnp.max(logits, axis=0, keepdims=True))
    probs = e / jnp.sum(e, axis=0, keepdims=True)
    lb = jnp.zeros((1, d_model), F32)
    for r in range(1, layer_j + 1):
        lb = lb + probs[r:r + 1, :]
    log_lb = jnp.log(lb)
    log_1m_lb = jnp.log1p(-lb)
    one_m_lb = 1.0 - lb
    rows = x_ref.shape[0]
    heads_per_unit = UNIT // LANES

    def mix(proj_ref, slot, flush):
        def gates(c, u):
            rs = slice(c * LIN_CHUNK, (c + 1) * LIN_CHUNK)
            ls = slice(u * UNIT, (u + 1) * UNIT)
            q = proj_ref[rs, u * UNIT:(u + 1) * UNIT]
            f = proj_ref[rs, d_model + u * UNIT:d_model + (u + 1) * UNIT]
            ef = jnp.exp(-jnp.abs(f))
            one_p_ef = 1.0 + ef
            log_sig = jnp.minimum(f, 0.0) - jnp.log(one_p_ef)
            sig_neg = jnp.where(f >= 0.0, ef, 1.0) / one_p_ef
            a1 = log_lb[:, ls]
            a2 = log_1m_lb[:, ls] + log_sig
            lf = jnp.maximum(a1, a2) + jnp.log(1.0 + jnp.exp(-jnp.abs(a1 - a2)))
            return q * jax.nn.sigmoid(q), one_m_lb[:, ls] * sig_neg, lf, 1.0

        def values(c, u):
            rs = slice(c * LIN_CHUNK, (c + 1) * LIN_CHUNK)
            return [proj_ref[rs, 2 * d_model + h * LANES:2 * d_model + (h + 1) * LANES].astype(BF16)
                    for h in range(u * heads_per_unit, (u + 1) * heads_per_unit)]

        def finish(c, u, outs):
            rs = slice(c * LIN_CHUNK, (c + 1) * LIN_CHUNK)
            ng = ng_ref[...]
            for n, o in enumerate(outs):
                h = u * heads_per_unit + n
                ms = jnp.mean(o * o, axis=-1, keepdims=True)
                g = proj_ref[rs, 3 * d_model + h * LANES:3 * d_model + (h + 1) * LANES]
                y = o * lax.rsqrt(ms + RMS_EPS) * ng * jax.nn.sigmoid(g)
                o_ref[rs, h * LANES:(h + 1) * LANES] = y.astype(o_ref.dtype)

        _recurrent_mixer(rows // LIN_CHUNK, d_model // UNIT, 1, gates, values, finish, st_ref,
                         oi_ref, (eq_ref, ek_ref, eb_ref), lambda c: None, slot, flush)

    n_slots = 2 * (rows // LIN_CHUNK) * (d_model // UNIT)
    _pipelined_mixer_step(x_ref, win_ref, buf_a, buf_b, st_ref, mix, n_slots)


def _hgrn_mixer(x2d, consts, layer_j):
    d_model = x2d.shape[1]
    kern = functools.partial(_hgrn_kernel, layer_j=layer_j, d_model=d_model)
    return _mixer_call(kern, "hgrn_mixer", x2d, consts, d_model // LANES, d_model // LANES)


def _even_kernel(x_ref, win_ref, ws_ref, bs_ref, lng_ref, lnb_ref, wgk_ref, bgk_ref, ng_ref,
                 o_ref, buf_a, buf_b, st_ref, oi_ref, eq_ref, ek_ref, eb_ref, *, a_width, key_width,
                 b_width):
    rows = x_ref.shape[0]
    o_q = 2 * a_width
    o_k = o_q + key_width
    o_v = o_k + key_width
    o_g = o_v + b_width
    o_gk = o_g + b_width
    group_dim = a_width // A_GROUPS
    n_sgu = rows // SGU_CHUNK
    n_chunks = rows // LIN_CHUNK
    n_heads = b_width // LANES
    head_k = key_width // n_heads
    heads_per_group = LANES // head_k
    scale = head_k ** -0.5
    assert key_width == UNIT and n_chunks >= A_GROUPS

    def mix(proj_ref, slot, flush):
        def sgu_group(gi):
            ls = slice(gi * group_dim, (gi + 1) * group_dim)
            v = jax.nn.gelu(proj_ref[:, a_width + gi * group_dim:a_width + (gi + 1) * group_dim])
            vn = _layer_norm(v, lng_ref[:, ls], lnb_ref[:, ls]).astype(BF16)
            rhs = jnp.concatenate(
                [vn[n * SGU_CHUNK:(n + 1) * SGU_CHUNK, :] for n in range(n_sgu)], axis=1)
            w = jnp.where(_causal_mask(SGU_CHUNK), ws_ref[gi], 0.0).astype(BF16)
            mixed = _dot(w, rhs)
            bias = bs_ref[:, ls]
            for n in range(n_sgu):
                rs = slice(n * SGU_CHUNK, (n + 1) * SGU_CHUNK)
                u = jax.nn.gelu(proj_ref[rs, ls])
                y = u * (mixed[:, n * group_dim:(n + 1) * group_dim] + bias)
                o_ref[rs, ls] = y.astype(o_ref.dtype)

        def extra(c):
            if c < A_GROUPS:
                sgu_group(c)
                slot()

        def gates(c, u):
            rs = slice(c * LIN_CHUNK, (c + 1) * LIN_CHUNK)
            gk_low = proj_ref[rs, o_gk:o_gk + LANES].astype(BF16)
            lf = _log_sigmoid(_dot(gk_low, wgk_ref[...]) + bgk_ref[...])
            q = proj_ref[rs, o_q:o_q + key_width] * scale
            k = proj_ref[rs, o_k:o_k + key_width]
            return q, k, lf, 1.0 / GLA_GATE_NORMALIZER

        def values(c, u):
            rs = slice(c * LIN_CHUNK, (c + 1) * LIN_CHUNK)
            return [proj_ref[rs, o_v + h * LANES:o_v + (h + 1) * LANES].astype(BF16)
                    for h in range(n_heads)]

        def finish(c, u, outs):
            rs = slice(c * LIN_CHUNK, (c + 1) * LIN_CHUNK)
            ng = ng_ref[...]
            for h, o in enumerate(outs):
                ms = jnp.mean(o * o, axis=-1, keepdims=True)
                g = proj_ref[rs, o_g + h * LANES:o_g + (h + 1) * LANES]
                y = o * lax.rsqrt(ms + RMS_EPS) * ng * (g * jax.nn.sigmoid(g))
                o_ref[rs, a_width + h * LANES:a_width + (h + 1) * LANES] = y.astype(o_ref.dtype)

        _recurrent_mixer(n_chunks, 1, heads_per_group, gates, values, finish, st_ref, oi_ref,
                         (eq_ref, ek_ref, eb_ref), extra, slot, flush)

    _pipelined_mixer_step(x_ref, win_ref, buf_a, buf_b, st_ref, mix, 2 * n_chunks + A_GROUPS)


def _even_mixer(x2d, consts, a_width, key_width, b_width):
    kern = functools.partial(_even_kernel, a_width=a_width, key_width=key_width, b_width=b_width)
    return _mixer_call(kern, "even_mixer", x2d, consts, key_width // LANES, b_width // LANES)


def _post_kernel(x_ref, y_ref, p_ref, wout_ref, lnmg_ref, lnmb_ref, wfi_ref, wfo_ref,
                 lnfg_ref, lnfb_ref, wpg_ref, wpp_ref, *rest, alpha):
    n_cast = len(rest) // 2
    cast_src, o_ref, cast_dst = rest[:n_cast], rest[n_cast], rest[n_cast + 1:]
    for src, dst in zip(cast_src, cast_dst):
        width = src.shape[1]
        dst[:, :width] = src[...].astype(dst.dtype)
        if dst.shape[1] > width:
            dst[:, width:] = jnp.zeros((dst.shape[0], dst.shape[1] - width), dst.dtype)

    rows = x_ref.shape[0]
    d_ff = wfo_ref.shape[0]
    chunks = list(range(0, d_ff, FF_CHUNK))
    parts = [slice(n * rows // POST_SPLIT, (n + 1) * rows // POST_SPLIT) for n in range(POST_SPLIT)]
    st = [dict() for _ in parts]

    def out_proj(n):
        st[n]["mixed"] = _dot(y_ref[parts[n], :], wout_ref[...])

    def norm_mix(n):
        x1 = _layer_norm(alpha * x_ref[parts[n], :] + st[n].pop("mixed"), lnmg_ref[...], lnmb_ref[...])
        st[n]["x1"], st[n]["x1b"] = x1, x1.astype(BF16)

    def ffn_in(n, c0):
        x1b = st[n]["x1b"]
        st[n]["h", c0] = (_dot(x1b, wfi_ref[:, c0:c0 + FF_CHUNK]),
                          _dot(x1b, wfi_ref[:, d_ff + c0:d_ff + c0 + FF_CHUNK]))

    def ffn_out(n, k):
        if k + 1 < len(chunks):
            ffn_in(n, chunks[k + 1])
        hg, hu = st[n].pop(("h", chunks[k]))
        act = (hg * jax.nn.sigmoid(hg) * hu).astype(BF16)
        part = _dot(act, wfo_ref[chunks[k]:chunks[k] + FF_CHUNK, :])
        st[n]["acc"] = part if k == 0 else st[n]["acc"] + part

    def norm_ffn(n):
        st[n]["x2"] = _layer_norm(alpha * st[n].pop("x1") + st[n].pop("acc"), lnfg_ref[...],
                                  lnfb_ref[...])

    def embed(n):
        x2 = st[n].pop("x2")
        gate = jax.nn.sigmoid(_dot(x2.astype(BF16), wpg_ref[...]))
        emb = _dot(p_ref[parts[n], :].astype(BF16), wpp_ref[...])
        o_ref[parts[n], :] = x2 + gate * emb

    stages = [out_proj, norm_mix, lambda n: ffn_in(n, chunks[0])]
    stages += [functools.partial(ffn_out, k=k) for k in range(len(chunks))]
    stages += [norm_ffn, embed]
    for t in range(len(stages) + POST_LAG * (POST_SPLIT - 1)):
        for n in range(POST_SPLIT):
            k = t - n * POST_LAG
            if 0 <= k < len(stages):
                stages[k](n)


def _post(x2d, y2d, p3d, layer, consts, alpha, to_cast):
    seq, d_model = x2d.shape
    n_steps = seq // POST_ROWS
    cast_in, cast_out, cast_shapes = [], [], []
    for stacked, index, width in to_cast:
        rows, src_width = stacked.shape[1:]
        span = 1
        while rows % (n_steps // span) or (rows // (n_steps // span)) % BF16_ROWS:
            span *= 2
        block_rows = rows // (n_steps // span)
        cast_in.append(pl.BlockSpec((None, block_rows, src_width),
                                    lambda i, index=index, span=span: (index, i // span, 0)))
        cast_out.append(pl.BlockSpec((block_rows, width), lambda i, span=span: (i // span, 0)))
        cast_shapes.append(jax.ShapeDtypeStruct((rows, width), BF16))
    kern = functools.partial(_post_kernel, alpha=alpha)
    outs = pl.pallas_call(
        kern,
        grid=(n_steps,),
        in_specs=[
            pl.BlockSpec((POST_ROWS, d_model), lambda i: (i, 0)),
            pl.BlockSpec((POST_ROWS, d_model), lambda i: (i, 0)),
            pl.BlockSpec((None, POST_ROWS, p3d.shape[2]), lambda i: (layer, i, 0)),
        ] + [spec for _, spec in consts] + cast_in,
        out_specs=[pl.BlockSpec((POST_ROWS, d_model), lambda i: (i, 0))] + cast_out,
        out_shape=[jax.ShapeDtypeStruct((seq, d_model), F32)] + cast_shapes,
        compiler_params=pltpu.CompilerParams(
            dimension_semantics=("arbitrary",), vmem_limit_bytes=VMEM_LIMIT),
        name="post",
    )(x2d, y2d, p3d, *[arr for arr, _ in consts], *[stacked for stacked, _, _ in to_cast])
    return outs[0], outs[1:]


def kernel(x, p, even_w_in, even_w_s, even_b_s, even_sgu_ln_g, even_sgu_ln_b, even_w_gk_up, even_b_gk,
           even_gla_norm_g, even_w_out, odd_w_in, hgrn_lb_logits, odd_hgrn_norm_g, odd_w_out,
           ln_mix_g, ln_mix_b, w_ffn_in, w_ffn_out, ln_ffn_g, ln_ffn_b, w_ple_proj, w_ple_gate):
    batch, seq, d_model = x.shape
    depth = p.shape[0]
    assert batch == 1 and seq % POST_ROWS == 0 and seq % MIX_ROWS == 0
    alpha = (2.0 * depth) ** 0.25
    a_width = even_sgu_ln_g.shape[1]
    b_width = d_model - a_width
    key_width = even_w_gk_up.shape[2]
    assert w_ffn_out.shape[1] % FF_CHUNK == 0
    even_in_pad = -(-even_w_in.shape[2] // LANES) * LANES

    def big_weights(layer):
        j = layer // 2
        w_in, w_out = (even_w_in, even_w_out) if layer % 2 == 0 else (odd_w_in, odd_w_out)
        mats = [(w_in, j), (w_out, j), (w_ffn_in, layer), (w_ffn_out, layer), (w_ple_gate, layer),
                (w_ple_proj, layer)]
        return [(w, idx, -(-w.shape[2] // LANES) * LANES) for w, idx in mats]

    rows3 = lambda a: a.reshape(a.shape[0], 1, -1).astype(F32)
    even_w_gk_b = jnp.pad(even_w_gk_up, ((0, 0), (0, LANES - GK_RANK), (0, 0))).astype(BF16)
    even_bias = jnp.repeat(even_b_s.transpose(0, 2, 1), a_width // A_GROUPS, axis=2)
    even_small = (even_w_s, even_bias, rows3(even_sgu_ln_g), rows3(even_sgu_ln_b), even_w_gk_b,
                  rows3(even_b_gk), rows3(even_gla_norm_g))
    odd_norm_g = rows3(odd_hgrn_norm_g)
    lnm_g, lnm_b, lnf_g, lnf_b = rows3(ln_mix_g), rows3(ln_mix_b), rows3(ln_ffn_g), rows3(ln_ffn_b)

    big = [jnp.pad(w[idx], ((0, 0), (0, width - w.shape[2]))).astype(BF16)
           for w, idx, width in big_weights(0)]
    assert big[0].shape[1] == even_in_pad
    xs = x.reshape(seq, d_model)
    p3 = p.reshape(depth, seq, -1)
    for layer in range(depth):
        j = layer // 2
        const = lambda a: (a, _const_spec(a.shape))
        w_in_b, w_out_b, wfi_b, wfo_b, wpg_b, wpp_b = big
        if layer % 2 == 0:
            y = _even_mixer(xs, [const(w_in_b)] + [_layer_operand(a, j) for a in even_small],
                            a_width, key_width, b_width)
        else:
            y = _hgrn_mixer(xs, [const(w_in_b), const(hgrn_lb_logits.astype(F32)),
                                 _layer_operand(odd_norm_g, j)], j)
        consts = [const(w_out_b), _layer_operand(lnm_g, layer), _layer_operand(lnm_b, layer),
                  const(wfi_b), const(wfo_b), _layer_operand(lnf_g, layer),
                  _layer_operand(lnf_b, layer), const(wpg_b), const(wpp_b)]
        xs, big = _post(xs, y, p3, layer, consts, alpha,
                        big_weights(layer + 1) if layer + 1 < depth else [])
    return xs.reshape(batch, seq, d_model)
```

```python
import functools

import jax
import jax.numpy as jnp
from jax import lax
from jax.experimental import pallas as pl
from jax.experimental.pallas import tpu as pltpu

F32 = jnp.float32
BF16 = jnp.bfloat16

LANES = 128
BF16_ROWS = 16
MXU_WIDTH = 256
LN_EPS = 1e-5
RMS_EPS = 1e-6
LOG2_E = 1.4426950408889634
GLA_GATE_NORMALIZER = 16.0
GK_RANK = 16
A_GROUPS = 4
SGU_CHUNK = 128
LIN_CHUNK = 64
MIX_ROWS = 256
POST_ROWS = 512
POST_SPLIT = 2
POST_LAG = 2
FF_CHUNK = 256
CAST_STEPS = 8
PROJ_SLAB = MXU_WIDTH
UNIT = MXU_WIDTH
MAX_LOG2_SPAN = 100.0
VMEM_LIMIT = 56 * 1024 * 1024


def _dot(a, b):
    return jnp.dot(a, b, preferred_element_type=F32)


def _dot_nt(a, b):
    return lax.dot_general(a, b, (((1,), (1,)), ((), ())), preferred_element_type=F32)


def _dot_tn(a, b):
    return lax.dot_general(a, b, (((0,), (0,)), ((), ())), preferred_element_type=F32)


def _layer_norm(x, g, b):
    mu = jnp.mean(x, axis=-1, keepdims=True)
    xc = x - mu
    var = jnp.mean(xc * xc, axis=-1, keepdims=True)
    return xc * lax.rsqrt(var + LN_EPS) * g + b


def _log_sigmoid(x):
    return jnp.minimum(x, 0.0) - jnp.log(1.0 + jnp.exp(-jnp.abs(x)))


def _const_spec(shape):
    nd = len(shape)
    return pl.BlockSpec(shape, lambda i: (0,) * nd, pipeline_mode=pl.Buffered(1))


def _layer_operand(stacked, layer):
    nd = stacked.ndim
    spec = pl.BlockSpec((None,) + stacked.shape[1:], lambda i: (layer,) + (0,) * (nd - 1),
                        pipeline_mode=pl.Buffered(1))
    return stacked, spec


def _causal_mask(c):
    rows = lax.broadcasted_iota(jnp.int32, (c, c), 0)
    cols = lax.broadcasted_iota(jnp.int32, (c, c), 1)
    return cols <= rows


def _cumsum_rows(x):
    tril_b = jnp.where(_causal_mask(x.shape[0]), 1.0, 0.0).astype(BF16)
    hi = x.astype(BF16)
    lo = (x - hi.astype(F32)).astype(BF16)
    return _dot(tril_b, hi) + _dot(tril_b, lo)


def _recurrence_prepare(q, k, lf, lf_scale):
    c = q.shape[0]
    b = _cumsum_rows(lf * (lf_scale * LOG2_E))
    b_mid = b[c // 2:c // 2 + 1, :]
    b_last = b[c - 1:c, :]
    span = jnp.maximum(b[0:1, :] - b_mid, b_mid - b_last)
    work = dict(
        qe=(q * jnp.exp2(b - b_mid)).astype(BF16),
        ke=(k * jnp.exp2(b_mid - b)).astype(BF16),
        qb=(q * jnp.exp2(b)).astype(BF16),
        kd=(k * jnp.exp2(b_last - b)).astype(BF16),
        dec=jnp.exp2(b_last))
    return work, span


def _only_head(a, h, heads_per_group):
    width = a.shape[1]
    head_k = LANES // heads_per_group
    if heads_per_group == 1:
        zeros = jnp.zeros((a.shape[0], LANES), a.dtype)
        return jnp.concatenate([a[:, g * LANES:(g + 1) * LANES] if g == h else zeros
                                for g in range(width // LANES)], axis=1)
    lane = lax.broadcasted_iota(jnp.int32, a.shape, 1)
    return jnp.where((lane >= h * head_k) & (lane < (h + 1) * head_k), a, jnp.zeros_like(a))


def _block_diagonal(blocks):
    n = len(blocks)
    zeros = jnp.zeros_like(blocks[0])
    return jnp.concatenate(
        [jnp.concatenate([blk if j == i else zeros for j in range(n)], axis=1)
         for i, blk in enumerate(blocks)], axis=0)


def _recurrence_scores(work, v_heads, heads_per_group):
    c = v_heads[0].shape[0]
    n_heads = len(v_heads)
    kblk = jnp.concatenate([_only_head(work["ke"], h, heads_per_group) for h in range(n_heads)],
                           axis=0)
    rows = lax.broadcasted_iota(jnp.int32, (c, n_heads * c), 0)
    cols = lax.broadcasted_iota(jnp.int32, (c, n_heads * c), 1)
    causal = (cols & (c - 1)) <= rows
    work["p"] = jnp.where(causal, _dot_nt(work["qe"], kblk), 0.0).astype(BF16)
    work["upd"] = []
    for gi in range(work["kd"].shape[1] // LANES):
        upd = None
        for s in range(heads_per_group):
            h = gi * heads_per_group + s
            kd_h = _only_head(work["kd"], h, heads_per_group)[:, gi * LANES:(gi + 1) * LANES]
            u = _dot_tn(v_heads[h], kd_h)
            upd = u if upd is None else upd + u
        work["upd"].append(upd)


def _recurrence_outputs(work, v_heads, st_ref, g0, heads_per_group):
    n_heads = len(v_heads)
    n_groups = n_heads // heads_per_group
    states = [st_ref[g0 + gi] for gi in range(n_groups)]
    st_unit = jnp.concatenate([st.astype(BF16) for st in states], axis=1)
    stblk = jnp.concatenate([_only_head(st_unit, h, heads_per_group) for h in range(n_heads)],
                            axis=0)
    o_carried = _dot_nt(work["qb"], stblk)
    o_intra = _dot(work["p"], _block_diagonal(v_heads))
    for gi, st in enumerate(states):
        st_ref[g0 + gi] = st * work["dec"][:, gi * LANES:(gi + 1) * LANES] + work["upd"][gi]
    return [(o_carried[:, h * LANES:(h + 1) * LANES], o_intra[:, h * LANES:(h + 1) * LANES])
            for h in range(n_heads)]


def _exact_scores(q_ref, k_ref, b_ref, heads_per_group):
    c, width = q_ref.shape
    head_k = LANES // heads_per_group
    q = q_ref[...]
    b = b_ref[...]
    col = lax.broadcasted_iota(jnp.int32, (c, c), 1)
    lane = lax.broadcasted_iota(jnp.int32, (c, LANES), 1)
    n_heads = (width // LANES) * heads_per_group

    def body(s, ps):
        k_s = k_ref[pl.ds(s, 1), :]
        b_s = b_ref[pl.ds(s, 1), :]
        w = q * k_s * jnp.exp2(jnp.minimum(b - b_s, 0.0))
        new = []
        for h in range(n_heads):
            gi, sub = divmod(h, heads_per_group)
            wg = w[:, gi * LANES:(gi + 1) * LANES]
            if heads_per_group > 1:
                wg = jnp.where((lane >= sub * head_k) & (lane < (sub + 1) * head_k), wg, 0.0)
            new.append(jnp.where(col == s, jnp.sum(wg, axis=-1, keepdims=True), ps[h]))
        return tuple(new)

    ps = lax.fori_loop(0, c, body, tuple(jnp.zeros((c, c), F32) for _ in range(n_heads)))
    causal = _causal_mask(c)
    return [jnp.where(causal, p, 0.0).astype(BF16) for p in ps]


def _run_recurrence(n_chunks, n_units, prepare, scores, outputs, finish, extra, slot):
    for u in range(n_units):
        prepare(0, u)
    for c in range(n_chunks):
        for u in range(n_units):
            if c + 1 < n_chunks:
                prepare(c + 1, u)
            scores(c, u)
            slot()
        extra(c)
        for u in range(n_units):
            outputs(c, u)
            if u > 0:
                finish(c, u - 1)
            slot()
        finish(c, n_units - 1)


def _recurrent_mixer(n_chunks, n_units, heads_per_group, gates, values, finish, st_ref, oi_ref,
                     exact_refs, extra, slot, flush):
    heads_per_unit = (UNIT // LANES) * heads_per_group
    work, spans = {}, []

    def prepare(c, u):
        q, k, lf, lf_scale = gates(c, u)
        work[c, u], span = _recurrence_prepare(q, k, lf, lf_scale)
        spans.append(span)

    def scores(c, u):
        _recurrence_scores(work[c, u], values(c, u), heads_per_group)

    def outputs(c, u):
        outs = []
        pairs = _recurrence_outputs(work[c, u], values(c, u), st_ref, u * (UNIT // LANES),
                                    heads_per_group)
        for n, (o_carried, o_intra) in enumerate(pairs):
            oi_ref[c, u * heads_per_unit + n] = o_carried
            outs.append(o_carried + o_intra)
        work[c, u] = outs

    _run_recurrence(n_chunks, n_units, prepare, scores, outputs,
                    lambda c, u: finish(c, u, work.pop((c, u))), extra, slot)
    flush()

    worst = functools.reduce(jnp.maximum, spans)

    @pl.when(jnp.max(worst) > MAX_LOG2_SPAN)
    def _():
        q_ref, k_ref, b_ref = exact_refs
        for c in range(n_chunks):
            for u in range(n_units):
                q, k, lf, lf_scale = gates(c, u)
                q_ref[...] = q
                k_ref[...] = k
                b_ref[...] = _cumsum_rows(lf * (lf_scale * LOG2_E))
                ps = _exact_scores(q_ref, k_ref, b_ref, heads_per_group)
                vs = values(c, u)
                finish(c, u, [oi_ref[c, u * heads_per_unit + n] + _dot(ps[n], vs[n])
                              for n in range(heads_per_unit)])


def _pipelined_mixer_step(x_ref, win_ref, buf_a, buf_b, st_ref, mix, n_slots):
    i = pl.program_id(0)

    @pl.when(i == 0)
    def _():
        st_ref[...] = jnp.zeros_like(st_ref)
        buf_b[...] = jnp.zeros_like(buf_b)

    def step(w_buf, r_buf):
        xb = x_ref[...].astype(BF16)
        width = w_buf.shape[1]
        pending = list(range(0, width, PROJ_SLAB))
        n_slabs = len(pending)
        seen = [0]

        def project_slab():
            lo = pending.pop(0)
            hi = min(lo + PROJ_SLAB, width)
            w_buf[:, lo:hi] = _dot(xb, win_ref[:, lo:hi])

        def slot():
            seen[0] += 1
            while pending and (n_slabs - len(pending)) * n_slots < seen[0] * n_slabs:
                project_slab()

        def flush():
            while pending:
                project_slab()

        mix(r_buf, slot, flush)

    @pl.when(lax.rem(i, 2) == 0)
    def _():
        step(buf_a, buf_b)

    @pl.when(lax.rem(i, 2) == 1)
    def _():
        step(buf_b, buf_a)


def _mixer_call(kern, name, x2d, consts, n_state, n_heads):
    seq, d_model = x2d.shape
    proj_width = consts[0][0].shape[-1]
    n_blocks = seq // MIX_ROWS
    return pl.pallas_call(
        kern,
        grid=(n_blocks + 1,),
        in_specs=[pl.BlockSpec((MIX_ROWS, d_model), lambda i: (jnp.minimum(i, n_blocks - 1), 0))]
        + [spec for _, spec in consts],
        out_specs=pl.BlockSpec((MIX_ROWS, d_model), lambda i: (jnp.maximum(i - 1, 0), 0)),
        out_shape=jax.ShapeDtypeStruct((seq, d_model), BF16),
        scratch_shapes=[
            pltpu.VMEM((MIX_ROWS, proj_width), F32),
            pltpu.VMEM((MIX_ROWS, proj_width), F32),
            pltpu.VMEM((n_state, LANES, LANES), F32),
            pltpu.VMEM((MIX_ROWS // LIN_CHUNK, n_heads, LIN_CHUNK, LANES), F32),
            pltpu.VMEM((LIN_CHUNK, UNIT), F32),
            pltpu.VMEM((LIN_CHUNK, UNIT), F32),
            pltpu.VMEM((LIN_CHUNK, UNIT), F32),
        ],
        compiler_params=pltpu.CompilerParams(
            dimension_semantics=("arbitrary",), vmem_limit_bytes=VMEM_LIMIT),
        name=name,
    )(x2d, *[arr for arr, _ in consts])


def _hgrn_kernel(x_ref, win_ref, lbl_ref, ng_ref, o_ref, buf_a, buf_b, st_ref, oi_ref,
                 eq_ref, ek_ref, eb_ref, *, layer_j, d_model):
    logits = lbl_ref[...]
    e = jnp.exp(logits - jnp.max(logits, axis=0, keepdims=True))
    probs = e / jnp.sum(e, axis=0, keepdims=True)
    lb = jnp.zeros((1, d_model), F32)
    for r in range(1, layer_j + 1):
        lb = lb + probs[r:r + 1, :]
    log_lb = jnp.log(lb)
    log_1m_lb = jnp.log1p(-lb)
    one_m_lb = 1.0 - lb
    rows = x_ref.shape[0]
    heads_per_unit = UNIT // LANES

    def mix(proj_ref, slot, flush):
        def gates(c, u):
            rs = slice(c * LIN_CHUNK, (c + 1) * LIN_CHUNK)
            ls = slice(u * UNIT, (u + 1) * UNIT)
            q = proj_ref[rs, u * UNIT:(u + 1) * UNIT]
            f = proj_ref[rs, d_model + u * UNIT:d_model + (u + 1) * UNIT]
            ef = jnp.exp(-jnp.abs(f))
            one_p_ef = 1.0 + ef
            log_sig = jnp.minimum(f, 0.0) - jnp.log(one_p_ef)
            sig_neg = jnp.where(f >= 0.0, ef, 1.0) / one_p_ef
            a1 = log_lb[:, ls]
            a2 = log_1m_lb[:, ls] + log_sig
            lf = jnp.maximum(a1, a2) + jnp.log(1.0 + jnp.exp(-jnp.abs(a1 - a2)))
            return q * jax.nn.sigmoid(q), one_m_lb[:, ls] * sig_neg, lf, 1.0

        def values(c, u):
            rs = slice(c * LIN_CHUNK, (c + 1) * LIN_CHUNK)
            return [proj_ref[rs, 2 * d_model + h * LANES:2 * d_model + (h + 1) * LANES].astype(BF16)
                    for h in range(u * heads_per_unit, (u + 1) * heads_per_unit)]

        def finish(c, u, outs):
            rs = slice(c * LIN_CHUNK, (c + 1) * LIN_CHUNK)
            ng = ng_ref[...]
            for n, o in enumerate(outs):
                h = u * heads_per_unit + n
                ms = jnp.mean(o * o, axis=-1, keepdims=True)
                g = proj_ref[rs, 3 * d_model + h * LANES:3 * d_model + (h + 1) * LANES]
                y = o * lax.rsqrt(ms + RMS_EPS) * ng * jax.nn.sigmoid(g)
                o_ref[rs, h * LANES:(h + 1) * LANES] = y.astype(o_ref.dtype)

        _recurrent_mixer(rows // LIN_CHUNK, d_model // UNIT, 1, gates, values, finish, st_ref,
                         oi_ref, (eq_ref, ek_ref, eb_ref), lambda c: None, slot, flush)

    n_slots = 2 * (rows // LIN_CHUNK) * (d_model // UNIT)
    _pipelined_mixer_step(x_ref, win_ref, buf_a, buf_b, st_ref, mix, n_slots)


def _hgrn_mixer(x2d, consts, layer_j):
    d_model = x2d.shape[1]
    kern = functools.partial(_hgrn_kernel, layer_j=layer_j, d_model=d_model)
    return _mixer_call(kern, "hgrn_mixer", x2d, consts, d_model // LANES, d_model // LANES)


def _even_kernel(x_ref, win_ref, ws_ref, bs_ref, lng_ref, lnb_ref, wgk_ref, bgk_ref, ng_ref,
                 o_ref, buf_a, buf_b, st_ref, oi_ref, eq_ref, ek_ref, eb_ref, *, a_width, key_width,
                 b_width):
    rows = x_ref.shape[0]
    o_q = 2 * a_width
    o_k = o_q + key_width
    o_v = o_k + key_width
    o_g = o_v + b_width
    o_gk = o_g + b_width
    group_dim = a_width // A_GROUPS
    n_sgu = rows // SGU_CHUNK
    n_chunks = rows // LIN_CHUNK
    n_heads = b_width // LANES
    head_k = key_width // n_heads
    heads_per_group = LANES // head_k
    scale = head_k ** -0.5
    assert key_width == UNIT and n_chunks >= A_GROUPS

    def mix(proj_ref, slot, flush):
        def sgu_group(gi):
            ls = slice(gi * group_dim, (gi + 1) * group_dim)
            v = jax.nn.gelu(proj_ref[:, a_width + gi * group_dim:a_width + (gi + 1) * group_dim])
            vn = _layer_norm(v, lng_ref[:, ls], lnb_ref[:, ls]).astype(BF16)
            rhs = jnp.concatenate(
                [vn[n * SGU_CHUNK:(n + 1) * SGU_CHUNK, :] for n in range(n_sgu)], axis=1)
            w = jnp.where(_causal_mask(SGU_CHUNK), ws_ref[gi], 0.0).astype(BF16)
            mixed = _dot(w, rhs)
            bias = bs_ref[:, ls]
            for n in range(n_sgu):
                rs = slice(n * SGU_CHUNK, (n + 1) * SGU_CHUNK)
                u = jax.nn.gelu(proj_ref[rs, ls])
                y = u * (mixed[:, n * group_dim:(n + 1) * group_dim] + bias)
                o_ref[rs, ls] = y.astype(o_ref.dtype)

        def extra(c):
            if c < A_GROUPS:
                sgu_group(c)
                slot()

        def gates(c, u):
            rs = slice(c * LIN_CHUNK, (c + 1) * LIN_CHUNK)
            gk_low = proj_ref[rs, o_gk:o_gk + LANES].astype(BF16)
            lf = _log_sigmoid(_dot(gk_low, wgk_ref[...]) + bgk_ref[...])
            q = proj_ref[rs, o_q:o_q + key_width] * scale
            k = proj_ref[rs, o_k:o_k + key_width]
            return q, k, lf, 1.0 / GLA_GATE_NORMALIZER

        def values(c, u):
            rs = slice(c * LIN_CHUNK, (c + 1) * LIN_CHUNK)
            return [proj_ref[rs, o_v + h * LANES:o_v + (h + 1) * LANES].astype(BF16)
                    for h in range(n_heads)]

        def finish(c, u, outs):
            rs = slice(c * LIN_CHUNK, (c + 1) * LIN_CHUNK)
            ng = ng_ref[...]
            for h, o in enumerate(outs):
                ms = jnp.mean(o * o, axis=-1, keepdims=True)
                g = proj_ref[rs, o_g + h * LANES:o_g + (h + 1) * LANES]
                y = o * lax.rsqrt(ms + RMS_EPS) * ng * (g * jax.nn.sigmoid(g))
                o_ref[rs, a_width + h * LANES:a_width + (h + 1) * LANES] = y.astype(o_ref.dtype)

        _recurrent_mixer(n_chunks, 1, heads_per_group, gates, values, finish, st_ref, oi_ref,
                         (eq_ref, ek_ref, eb_ref), extra, slot, flush)

    _pipelined_mixer_step(x_ref, win_ref, buf_a, buf_b, st_ref, mix, 2 * n_chunks + A_GROUPS)


def _even_mixer(x2d, consts, a_width, key_width, b_width):
    kern = functools.partial(_even_kernel, a_width=a_width, key_width=key_width, b_width=b_width)
    return _mixer_call(kern, "even_mixer", x2d, consts, key_width // LANES, b_width // LANES)


def _cast_specs(to_cast, n_steps):
    in_specs, out_specs, out_shapes = [], [], []
    for stacked, index, width in to_cast:
        rows, src_width = stacked.shape[1:]
        span = 1
        while rows % (n_steps // span) or (rows // (n_steps // span)) % BF16_ROWS:
            span *= 2
        block_rows = rows // (n_steps // span)
        in_specs.append(pl.BlockSpec((None, block_rows, src_width),
                                     lambda i, index=index, span=span: (index, i // span, 0)))
        out_specs.append(pl.BlockSpec((block_rows, width), lambda i, span=span: (i // span, 0)))
        out_shapes.append(jax.ShapeDtypeStruct((rows, width), BF16))
    return in_specs, out_specs, out_shapes


def _cast_rows(srcs, dsts):
    for src, dst in zip(srcs, dsts):
        width = src.shape[1]
        dst[:, :width] = src[...].astype(dst.dtype)
        if dst.shape[1] > width:
            dst[:, width:] = jnp.zeros((dst.shape[0], dst.shape[1] - width), dst.dtype)


def _cast_kernel(*refs):
    _cast_rows(refs[:len(refs) // 2], refs[len(refs) // 2:])


def _cast_weights(to_cast):
    in_specs, out_specs, out_shapes = _cast_specs(to_cast, CAST_STEPS)
    return pl.pallas_call(
        _cast_kernel,
        grid=(CAST_STEPS,),
        in_specs=in_specs,
        out_specs=out_specs,
        out_shape=out_shapes,
        compiler_params=pltpu.CompilerParams(
            dimension_semantics=("arbitrary",), vmem_limit_bytes=VMEM_LIMIT),
        name="cast_weights",
    )(*[stacked for stacked, _, _ in to_cast])


def _post_kernel(x_ref, y_ref, p_ref, wout_ref, lnmg_ref, lnmb_ref, wfi_ref, wfo_ref,
                 lnfg_ref, lnfb_ref, wpg_ref, wpp_ref, *rest, alpha):
    n_cast = len(rest) // 2
    cast_src, o_ref, cast_dst = rest[:n_cast], rest[n_cast], rest[n_cast + 1:]
    _cast_rows(cast_src, cast_dst)

    rows = x_ref.shape[0]
    d_ff = wfo_ref.shape[0]
    chunks = list(range(0, d_ff, FF_CHUNK))
    parts = [slice(n * rows // POST_SPLIT, (n + 1) * rows // POST_SPLIT) for n in range(POST_SPLIT)]
    st = [dict() for _ in parts]

    def out_proj(n):
        st[n]["mixed"] = _dot(y_ref[parts[n], :], wout_ref[...])

    def norm_mix(n):
        x1 = _layer_norm(alpha * x_ref[parts[n], :] + st[n].pop("mixed"), lnmg_ref[...], lnmb_ref[...])
        st[n]["x1"], st[n]["x1b"] = x1, x1.astype(BF16)

    def ffn_in(n, c0):
        x1b = st[n]["x1b"]
        st[n]["h", c0] = (_dot(x1b, wfi_ref[:, c0:c0 + FF_CHUNK]),
                          _dot(x1b, wfi_ref[:, d_ff + c0:d_ff + c0 + FF_CHUNK]))

    def ffn_out(n, k):
        if k + 1 < len(chunks):
            ffn_in(n, chunks[k + 1])
        hg, hu = st[n].pop(("h", chunks[k]))
        act = (hg * jax.nn.sigmoid(hg) * hu).astype(BF16)
        part = _dot(act, wfo_ref[chunks[k]:chunks[k] + FF_CHUNK, :])
        st[n]["acc"] = part if k == 0 else st[n]["acc"] + part

    def norm_ffn(n):
        st[n]["x2"] = _layer_norm(alpha * st[n].pop("x1") + st[n].pop("acc"), lnfg_ref[...],
                                  lnfb_ref[...])

    def embed(n):
        x2 = st[n].pop("x2")
        gate = jax.nn.sigmoid(_dot(x2.astype(BF16), wpg_ref[...]))
        emb = _dot(p_ref[parts[n], :].astype(BF16), wpp_ref[...])
        o_ref[parts[n], :] = x2 + gate * emb

    stages = [out_proj, norm_mix, lambda n: ffn_in(n, chunks[0])]
    stages += [functools.partial(ffn_out, k=k) for k in range(len(chunks))]
    stages += [norm_ffn, embed]
    for t in range(len(stages) + POST_LAG * (POST_SPLIT - 1)):
        for n in range(POST_SPLIT):
            k = t - n * POST_LAG
            if 0 <= k < len(stages):
                stages[k](n)


def _post(x2d, y2d, p3d, layer, consts, alpha, to_cast):
    seq, d_model = x2d.shape
    n_steps = seq // POST_ROWS
    cast_in, cast_out, cast_shapes = _cast_specs(to_cast, n_steps)
    kern = functools.partial(_post_kernel, alpha=alpha)
    outs = pl.pallas_call(
        kern,
        grid=(n_steps,),
        in_specs=[
            pl.BlockSpec((POST_ROWS, d_model), lambda i: (i, 0)),
            pl.BlockSpec((POST_ROWS, d_model), lambda i: (i, 0)),
            pl.BlockSpec((None, POST_ROWS, p3d.shape[2]), lambda i: (layer, i, 0)),
        ] + [spec for _, spec in consts] + cast_in,
        out_specs=[pl.BlockSpec((POST_ROWS, d_model), lambda i: (i, 0))] + cast_out,
        out_shape=[jax.ShapeDtypeStruct((seq, d_model), F32)] + cast_shapes,
        compiler_params=pltpu.CompilerParams(
            dimension_semantics=("arbitrary",), vmem_limit_bytes=VMEM_LIMIT),
        name="post",
    )(x2d, y2d, p3d, *[arr for arr, _ in consts], *[stacked for stacked, _, _ in to_cast])
    return outs[0], outs[1:]


def kernel(x, p, even_w_in, even_w_s, even_b_s, even_sgu_ln_g, even_sgu_ln_b, even_w_gk_up, even_b_gk,
           even_gla_norm_g, even_w_out, odd_w_in, hgrn_lb_logits, odd_hgrn_norm_g, odd_w_out,
           ln_mix_g, ln_mix_b, w_ffn_in, w_ffn_out, ln_ffn_g, ln_ffn_b, w_ple_proj, w_ple_gate):
    batch, seq, d_model = x.shape
    depth = p.shape[0]
    assert batch == 1 and seq % POST_ROWS == 0 and seq % MIX_ROWS == 0
    alpha = (2.0 * depth) ** 0.25
    a_width = even_sgu_ln_g.shape[1]
    b_width = d_model - a_width
    key_width = even_w_gk_up.shape[2]
    assert w_ffn_out.shape[1] % FF_CHUNK == 0
    even_in_pad = -(-even_w_in.shape[2] // LANES) * LANES

    def big_weights(layer):
        j = layer // 2
        w_in, w_out = (even_w_in, even_w_out) if layer % 2 == 0 else (odd_w_in, odd_w_out)
        mats = [(w_in, j), (w_out, j), (w_ffn_in, layer), (w_ffn_out, layer), (w_ple_gate, layer),
                (w_ple_proj, layer)]
        return [(w, idx, -(-w.shape[2] // LANES) * LANES) for w, idx in mats]

    rows3 = lambda a: a.reshape(a.shape[0], 1, -1).astype(F32)
    even_w_gk_b = jnp.pad(even_w_gk_up, ((0, 0), (0, LANES - GK_RANK), (0, 0))).astype(BF16)
    even_bias = jnp.repeat(even_b_s.transpose(0, 2, 1), a_width // A_GROUPS, axis=2)
    even_small = (even_w_s, even_bias, rows3(even_sgu_ln_g), rows3(even_sgu_ln_b), even_w_gk_b,
                  rows3(even_b_gk), rows3(even_gla_norm_g))
    odd_norm_g = rows3(odd_hgrn_norm_g)
    lnm_g, lnm_b, lnf_g, lnf_b = rows3(ln_mix_g), rows3(ln_mix_b), rows3(ln_ffn_g), rows3(ln_ffn_b)

    big = _cast_weights(big_weights(0))
    assert big[0].shape[1] == even_in_pad
    xs = x.reshape(seq, d_model)
    p3 = p.reshape(depth, seq, -1)
    for layer in range(depth):
        j = layer // 2
        const = lambda a: (a, _const_spec(a.shape))
        w_in_b, w_out_b, wfi_b, wfo_b, wpg_b, wpp_b = big
        if layer % 2 == 0:
            y = _even_mixer(xs, [const(w_in_b)] + [_layer_operand(a, j) for a in even_small],
                            a_width, key_width, b_width)
        else:
            y = _hgrn_mixer(xs, [const(w_in_b), const(hgrn_lb_logits.astype(F32)),
                                 _layer_operand(odd_norm_g, j)], j)
        consts = [const(w_out_b), _layer_operand(lnm_g, layer), _layer_operand(lnm_b, layer),
                  const(wfi_b), const(wfo_b), _layer_operand(lnf_g, layer),
                  _layer_operand(lnf_b, layer), const(wpg_b), const(wpp_b)]
        xs, big = _post(xs, y, p3, layer, consts, alpha,
                        big_weights(layer + 1) if layer + 1 < depth else [])
    return xs.reshape(batch, seq, d_model)
```

```python
import functools

import jax
import jax.numpy as jnp
from jax import lax
from jax.experimental import pallas as pl
from jax.experimental.pallas import tpu as pltpu

F32 = jnp.float32
BF16 = jnp.bfloat16

LANES = 128
BF16_ROWS = 16
MXU_WIDTH = 256
LN_EPS = 1e-5
RMS_EPS = 1e-6
LOG2_E = 1.4426950408889634
GLA_GATE_NORMALIZER = 16.0
GK_RANK = 16
A_GROUPS = 4
SGU_CHUNK = 128
LIN_CHUNK = 64
MIX_ROWS = 256
POST_ROWS = 512
POST_PARTS = (256, 256)
POST_LAG = 2
FF_CHUNK = 256
CAST_STEPS = 8
PROJ_SLAB = MXU_WIDTH
UNIT = MXU_WIDTH
MAX_LOG2_SPAN = 100.0
VMEM_LIMIT = 56 * 1024 * 1024


def _dot(a, b):
    return jnp.dot(a, b, preferred_element_type=F32)


def _dot_nt(a, b):
    return lax.dot_general(a, b, (((1,), (1,)), ((), ())), preferred_element_type=F32)


def _dot_tn(a, b):
    return lax.dot_general(a, b, (((0,), (0,)), ((), ())), preferred_element_type=F32)


def _layer_norm(x, g, b):
    mu = jnp.mean(x, axis=-1, keepdims=True)
    xc = x - mu
    var = jnp.mean(xc * xc, axis=-1, keepdims=True)
    return xc * lax.rsqrt(var + LN_EPS) * g + b


def _log_sigmoid(x):
    return jnp.minimum(x, 0.0) - jnp.log(1.0 + jnp.exp(-jnp.abs(x)))


def _const_spec(shape):
    nd = len(shape)
    return pl.BlockSpec(shape, lambda i: (0,) * nd, pipeline_mode=pl.Buffered(1))


def _layer_operand(stacked, layer):
    nd = stacked.ndim
    spec = pl.BlockSpec((None,) + stacked.shape[1:], lambda i: (layer,) + (0,) * (nd - 1),
                        pipeline_mode=pl.Buffered(1))
    return stacked, spec


def _causal_mask(c):
    rows = lax.broadcasted_iota(jnp.int32, (c, c), 0)
    cols = lax.broadcasted_iota(jnp.int32, (c, c), 1)
    return cols <= rows


def _cumsum_rows(x):
    tril_b = jnp.where(_causal_mask(x.shape[0]), 1.0, 0.0).astype(BF16)
    hi = x.astype(BF16)
    lo = (x - hi.astype(F32)).astype(BF16)
    return _dot(tril_b, hi) + _dot(tril_b, lo)


def _recurrence_prepare(q, k, lf, lf_scale):
    c = q.shape[0]
    b = _cumsum_rows(lf * (lf_scale * LOG2_E))
    b_mid = b[c // 2:c // 2 + 1, :]
    b_last = b[c - 1:c, :]
    span = jnp.maximum(b[0:1, :] - b_mid, b_mid - b_last)
    work = dict(
        qe=(q * jnp.exp2(b - b_mid)).astype(BF16),
        ke=(k * jnp.exp2(b_mid - b)).astype(BF16),
        qb=(q * jnp.exp2(b)).astype(BF16),
        kd=(k * jnp.exp2(b_last - b)).astype(BF16),
        dec=jnp.exp2(b_last))
    return work, span


def _only_head(a, h, heads_per_group):
    width = a.shape[1]
    head_k = LANES // heads_per_group
    if heads_per_group == 1:
        zeros = jnp.zeros((a.shape[0], LANES), a.dtype)
        return jnp.concatenate([a[:, g * LANES:(g + 1) * LANES] if g == h else zeros
                                for g in range(width // LANES)], axis=1)
    lane = lax.broadcasted_iota(jnp.int32, a.shape, 1)
    return jnp.where((lane >= h * head_k) & (lane < (h + 1) * head_k), a, jnp.zeros_like(a))


def _block_diagonal(blocks):
    n = len(blocks)
    zeros = jnp.zeros_like(blocks[0])
    return jnp.concatenate(
        [jnp.concatenate([blk if j == i else zeros for j in range(n)], axis=1)
         for i, blk in enumerate(blocks)], axis=0)


def _recurrence_scores(work, v_heads, heads_per_group):
    c = v_heads[0].shape[0]
    n_heads = len(v_heads)
    kblk = jnp.concatenate([_only_head(work["ke"], h, heads_per_group) for h in range(n_heads)],
                           axis=0)
    rows = lax.broadcasted_iota(jnp.int32, (c, n_heads * c), 0)
    cols = lax.broadcasted_iota(jnp.int32, (c, n_heads * c), 1)
    causal = (cols & (c - 1)) <= rows
    work["p"] = jnp.where(causal, _dot_nt(work["qe"], kblk), 0.0).astype(BF16)
    work["upd"] = []
    for gi in range(work["kd"].shape[1] // LANES):
        upd = None
        for s in range(heads_per_group):
            h = gi * heads_per_group + s
            kd_h = _only_head(work["kd"], h, heads_per_group)[:, gi * LANES:(gi + 1) * LANES]
            u = _dot_tn(v_heads[h], kd_h)
            upd = u if upd is None else upd + u
        work["upd"].append(upd)


def _recurrence_outputs(work, v_heads, st_ref, g0, heads_per_group):
    n_heads = len(v_heads)
    n_groups = n_heads // heads_per_group
    states = [st_ref[g0 + gi] for gi in range(n_groups)]
    st_unit = jnp.concatenate([st.astype(BF16) for st in states], axis=1)
    stblk = jnp.concatenate([_only_head(st_unit, h, heads_per_group) for h in range(n_heads)],
                            axis=0)
    o_carried = _dot_nt(work["qb"], stblk)
    o_intra = _dot(work["p"], _block_diagonal(v_heads))
    for gi, st in enumerate(states):
        st_ref[g0 + gi] = st * work["dec"][:, gi * LANES:(gi + 1) * LANES] + work["upd"][gi]
    return [(o_carried[:, h * LANES:(h + 1) * LANES], o_intra[:, h * LANES:(h + 1) * LANES])
            for h in range(n_heads)]


def _exact_scores(q_ref, k_ref, b_ref, heads_per_group):
    c, width = q_ref.shape
    head_k = LANES // heads_per_group
    q = q_ref[...]
    b = b_ref[...]
    col = lax.broadcasted_iota(jnp.int32, (c, c), 1)
    lane = lax.broadcasted_iota(jnp.int32, (c, LANES), 1)
    n_heads = (width // LANES) * heads_per_group

    def body(s, ps):
        k_s = k_ref[pl.ds(s, 1), :]
        b_s = b_ref[pl.ds(s, 1), :]
        w = q * k_s * jnp.exp2(jnp.minimum(b - b_s, 0.0))
        new = []
        for h in range(n_heads):
            gi, sub = divmod(h, heads_per_group)
            wg = w[:, gi * LANES:(gi + 1) * LANES]
            if heads_per_group > 1:
                wg = jnp.where((lane >= sub * head_k) & (lane < (sub + 1) * head_k), wg, 0.0)
            new.append(jnp.where(col == s, jnp.sum(wg, axis=-1, keepdims=True), ps[h]))
        return tuple(new)

    ps = lax.fori_loop(0, c, body, tuple(jnp.zeros((c, c), F32) for _ in range(n_heads)))
    causal = _causal_mask(c)
    return [jnp.where(causal, p, 0.0).astype(BF16) for p in ps]


def _run_recurrence(n_chunks, n_units, prepare, scores, outputs, finish, extra, slot):
    for u in range(n_units):
        prepare(0, u)
        if n_units > 1:
            slot()
    for c in range(n_chunks):
        for u in range(n_units):
            if c + 1 < n_chunks:
                prepare(c + 1, u)
            scores(c, u)
            slot()
        extra(c)
        for u in range(n_units):
            outputs(c, u)
            if u > 0:
                finish(c, u - 1)
            slot()
        finish(c, n_units - 1)


def _recurrent_mixer(n_chunks, n_units, heads_per_group, gates, values, finish, st_ref, oi_ref,
                     exact_refs, extra, slot, flush):
    heads_per_unit = (UNIT // LANES) * heads_per_group
    work, spans = {}, []

    def prepare(c, u):
        q, k, lf, lf_scale = gates(c, u)
        work[c, u], span = _recurrence_prepare(q, k, lf, lf_scale)
        spans.append(span)

    def scores(c, u):
        _recurrence_scores(work[c, u], values(c, u), heads_per_group)

    def outputs(c, u):
        outs = []
        pairs = _recurrence_outputs(work[c, u], values(c, u), st_ref, u * (UNIT // LANES),
                                    heads_per_group)
        for n, (o_carried, o_intra) in enumerate(pairs):
            oi_ref[c, u * heads_per_unit + n] = o_carried
            outs.append(o_carried + o_intra)
        work[c, u] = outs

    _run_recurrence(n_chunks, n_units, prepare, scores, outputs,
                    lambda c, u: finish(c, u, work.pop((c, u))), extra, slot)
    flush()

    worst = functools.reduce(jnp.maximum, spans)

    @pl.when(jnp.max(worst) > MAX_LOG2_SPAN)
    def _():
        q_ref, k_ref, b_ref = exact_refs
        for c in range(n_chunks):
            for u in range(n_units):
                q, k, lf, lf_scale = gates(c, u)
                q_ref[...] = q
                k_ref[...] = k
                b_ref[...] = _cumsum_rows(lf * (lf_scale * LOG2_E))
                ps = _exact_scores(q_ref, k_ref, b_ref, heads_per_group)
                vs = values(c, u)
                finish(c, u, [oi_ref[c, u * heads_per_unit + n] + _dot(ps[n], vs[n])
                              for n in range(heads_per_unit)])


def _pipelined_mixer_step(x_ref, win_ref, buf_a, buf_b, st_ref, mix, n_slots):
    i = pl.program_id(0)

    @pl.when(i == 0)
    def _():
        st_ref[...] = jnp.zeros_like(st_ref)
        buf_b[...] = jnp.zeros_like(buf_b)

    def step(w_buf, r_buf):
        xb = x_ref[...].astype(BF16)
        width = w_buf.shape[1]
        pending = list(range(0, width, PROJ_SLAB))
        n_slabs = len(pending)
        seen = [0]

        def project_slab():
            lo = pending.pop(0)
            hi = min(lo + PROJ_SLAB, width)
            w_buf[:, lo:hi] = _dot(xb, win_ref[:, lo:hi])

        def slot():
            seen[0] += 1
            while pending and (n_slabs - len(pending)) * n_slots < seen[0] * n_slabs:
                project_slab()

        def flush():
            while pending:
                project_slab()

        mix(r_buf, slot, flush)

    @pl.when(lax.rem(i, 2) == 0)
    def _():
        step(buf_a, buf_b)

    @pl.when(lax.rem(i, 2) == 1)
    def _():
        step(buf_b, buf_a)


def _mixer_call(kern, name, x2d, consts, n_state, n_heads):
    seq, d_model = x2d.shape
    proj_width = consts[0][0].shape[-1]
    n_blocks = seq // MIX_ROWS
    return pl.pallas_call(
        kern,
        grid=(n_blocks + 1,),
        in_specs=[pl.BlockSpec((MIX_ROWS, d_model), lambda i: (jnp.minimum(i, n_blocks - 1), 0))]
        + [spec for _, spec in consts],
        out_specs=pl.BlockSpec((MIX_ROWS, d_model), lambda i: (jnp.maximum(i - 1, 0), 0)),
        out_shape=jax.ShapeDtypeStruct((seq, d_model), BF16),
        scratch_shapes=[
            pltpu.VMEM((MIX_ROWS, proj_width), F32),
            pltpu.VMEM((MIX_ROWS, proj_width), F32),
            pltpu.VMEM((n_state, LANES, LANES), F32),
            pltpu.VMEM((MIX_ROWS // LIN_CHUNK, n_heads, LIN_CHUNK, LANES), F32),
            pltpu.VMEM((LIN_CHUNK, UNIT), F32),
            pltpu.VMEM((LIN_CHUNK, UNIT), F32),
            pltpu.VMEM((LIN_CHUNK, UNIT), F32),
        ],
        compiler_params=pltpu.CompilerParams(
            dimension_semantics=("arbitrary",), vmem_limit_bytes=VMEM_LIMIT),
        name=name,
    )(x2d, *[arr for arr, _ in consts])


def _hgrn_kernel(x_ref, win_ref, lbl_ref, ng_ref, o_ref, buf_a, buf_b, st_ref, oi_ref,
                 eq_ref, ek_ref, eb_ref, *, layer_j, d_model):
    logits = lbl_ref[...]
    e = jnp.exp(logits - jnp.max(logits, axis=0, keepdims=True))
    probs = e / jnp.sum(e, axis=0, keepdims=True)
    lb = jnp.zeros((1, d_model), F32)
    for r in range(1, layer_j + 1):
        lb = lb + probs[r:r + 1, :]
    one_m_lb = 1.0 - lb
    rows = x_ref.shape[0]
    heads_per_unit = UNIT // LANES

    def mix(proj_ref, slot, flush):
        def gates(c, u):
            rs = slice(c * LIN_CHUNK, (c + 1) * LIN_CHUNK)
            ls = slice(u * UNIT, (u + 1) * UNIT)
            q = proj_ref[rs, u * UNIT:(u + 1) * UNIT]
            f = proj_ref[rs, d_model + u * UNIT:d_model + (u + 1) * UNIT]
            ef = jnp.exp(-jnp.abs(f))
            inv = 1.0 / (1.0 + ef)
            pos = f >= 0.0
            forget = jnp.where(pos, 1.0 + lb[:, ls] * ef, lb[:, ls] + ef) * inv
            lf = jnp.where(forget > 0.0, jnp.log(forget), jnp.minimum(f, 0.0))
            sig_neg = jnp.where(pos, ef, 1.0) * inv
            return q * jax.nn.sigmoid(q), one_m_lb[:, ls] * sig_neg, lf, 1.0

        def values(c, u):
            rs = slice(c * LIN_CHUNK, (c + 1) * LIN_CHUNK)
            return [proj_ref[rs, 2 * d_model + h * LANES:2 * d_model + (h + 1) * LANES].astype(BF16)
                    for h in range(u * heads_per_unit, (u + 1) * heads_per_unit)]

        def finish(c, u, outs):
            rs = slice(c * LIN_CHUNK, (c + 1) * LIN_CHUNK)
            ng = ng_ref[...]
            for n, o in enumerate(outs):
                h = u * heads_per_unit + n
                ms = jnp.mean(o * o, axis=-1, keepdims=True)
                g = proj_ref[rs, 3 * d_model + h * LANES:3 * d_model + (h + 1) * LANES]
                y = o * lax.rsqrt(ms + RMS_EPS) * ng * jax.nn.sigmoid(g)
                o_ref[rs, h * LANES:(h + 1) * LANES] = y.astype(o_ref.dtype)

        _recurrent_mixer(rows // LIN_CHUNK, d_model // UNIT, 1, gates, values, finish, st_ref,
                         oi_ref, (eq_ref, ek_ref, eb_ref), lambda c: None, slot, flush)

    n_slots = (2 * (rows // LIN_CHUNK) + 1) * (d_model // UNIT)
    _pipelined_mixer_step(x_ref, win_ref, buf_a, buf_b, st_ref, mix, n_slots)


def _hgrn_mixer(x2d, consts, layer_j):
    d_model = x2d.shape[1]
    kern = functools.partial(_hgrn_kernel, layer_j=layer_j, d_model=d_model)
    return _mixer_call(kern, "hgrn_mixer", x2d, consts, d_model // LANES, d_model // LANES)


def _even_kernel(x_ref, win_ref, ws_ref, bs_ref, lng_ref, lnb_ref, wgk_ref, bgk_ref, ng_ref,
                 o_ref, buf_a, buf_b, st_ref, oi_ref, eq_ref, ek_ref, eb_ref, *, a_width, key_width,
                 b_width):
    rows = x_ref.shape[0]
    o_q = 2 * a_width
    o_k = o_q + key_width
    o_v = o_k + key_width
    o_g = o_v + b_width
    o_gk = o_g + b_width
    group_dim = a_width // A_GROUPS
    n_sgu = rows // SGU_CHUNK
    n_chunks = rows // LIN_CHUNK
    n_heads = b_width // LANES
    head_k = key_width // n_heads
    heads_per_group = LANES // head_k
    scale = head_k ** -0.5
    assert key_width == UNIT and n_chunks >= A_GROUPS

    def mix(proj_ref, slot, flush):
        def sgu_group(gi):
            ls = slice(gi * group_dim, (gi + 1) * group_dim)
            v = jax.nn.gelu(proj_ref[:, a_width + gi * group_dim:a_width + (gi + 1) * group_dim])
            vn = _layer_norm(v, lng_ref[:, ls], lnb_ref[:, ls]).astype(BF16)
            rhs = jnp.concatenate(
                [vn[n * SGU_CHUNK:(n + 1) * SGU_CHUNK, :] for n in range(n_sgu)], axis=1)
            w = jnp.where(_causal_mask(SGU_CHUNK), ws_ref[gi], 0.0).astype(BF16)
            mixed = _dot(w, rhs)
            bias = bs_ref[:, ls]
            for n in range(n_sgu):
                rs = slice(n * SGU_CHUNK, (n + 1) * SGU_CHUNK)
                u = jax.nn.gelu(proj_ref[rs, ls])
                y = u * (mixed[:, n * group_dim:(n + 1) * group_dim] + bias)
                o_ref[rs, ls] = y.astype(o_ref.dtype)

        def extra(c):
            if c < A_GROUPS:
                sgu_group(c)
                slot()

        def gates(c, u):
            rs = slice(c * LIN_CHUNK, (c + 1) * LIN_CHUNK)
            gk_low = proj_ref[rs, o_gk:o_gk + LANES].astype(BF16)
            lf = _log_sigmoid(_dot(gk_low, wgk_ref[...]) + bgk_ref[...])
            q = proj_ref[rs, o_q:o_q + key_width] * scale
            k = proj_ref[rs, o_k:o_k + key_width]
            return q, k, lf, 1.0 / GLA_GATE_NORMALIZER

        def values(c, u):
            rs = slice(c * LIN_CHUNK, (c + 1) * LIN_CHUNK)
            return [proj_ref[rs, o_v + h * LANES:o_v + (h + 1) * LANES].astype(BF16)
                    for h in range(n_heads)]

        def finish(c, u, outs):
            rs = slice(c * LIN_CHUNK, (c + 1) * LIN_CHUNK)
            ng = ng_ref[...]
            for h, o in enumerate(outs):
                ms = jnp.mean(o * o, axis=-1, keepdims=True)
                g = proj_ref[rs, o_g + h * LANES:o_g + (h + 1) * LANES]
                y = o * lax.rsqrt(ms + RMS_EPS) * ng * (g * jax.nn.sigmoid(g))
                o_ref[rs, a_width + h * LANES:a_width + (h + 1) * LANES] = y.astype(o_ref.dtype)

        _recurrent_mixer(n_chunks, 1, heads_per_group, gates, values, finish, st_ref, oi_ref,
                         (eq_ref, ek_ref, eb_ref), extra, slot, flush)

    _pipelined_mixer_step(x_ref, win_ref, buf_a, buf_b, st_ref, mix, 2 * n_chunks + A_GROUPS)


def _even_mixer(x2d, consts, a_width, key_width, b_width):
    kern = functools.partial(_even_kernel, a_width=a_width, key_width=key_width, b_width=b_width)
    return _mixer_call(kern, "even_mixer", x2d, consts, key_width // LANES, b_width // LANES)


def _cast_specs(to_cast, n_steps):
    in_specs, out_specs, out_shapes = [], [], []
    for stacked, index, width in to_cast:
        rows, src_width = stacked.shape[1:]
        span = 1
        while rows % (n_steps // span) or (rows // (n_steps // span)) % BF16_ROWS:
            span *= 2
        block_rows = rows // (n_steps // span)
        in_specs.append(pl.BlockSpec((None, block_rows, src_width),
                                     lambda i, index=index, span=span: (index, i // span, 0)))
        out_specs.append(pl.BlockSpec((block_rows, width), lambda i, span=span: (i // span, 0)))
        out_shapes.append(jax.ShapeDtypeStruct((rows, width), BF16))
    return in_specs, out_specs, out_shapes


def _cast_rows(srcs, dsts):
    for src, dst in zip(srcs, dsts):
        width = src.shape[1]
        dst[:, :width] = src[...].astype(dst.dtype)
        if dst.shape[1] > width:
            dst[:, width:] = jnp.zeros((dst.shape[0], dst.shape[1] - width), dst.dtype)


def _cast_kernel(*refs):
    _cast_rows(refs[:len(refs) // 2], refs[len(refs) // 2:])


def _cast_weights(to_cast):
    in_specs, out_specs, out_shapes = _cast_specs(to_cast, CAST_STEPS)
    return pl.pallas_call(
        _cast_kernel,
        grid=(CAST_STEPS,),
        in_specs=in_specs,
        out_specs=out_specs,
        out_shape=out_shapes,
        compiler_params=pltpu.CompilerParams(
            dimension_semantics=("arbitrary",), vmem_limit_bytes=VMEM_LIMIT),
        name="cast_weights",
    )(*[stacked for stacked, _, _ in to_cast])


def _post_kernel(x_ref, y_ref, p_ref, wout_ref, lnmg_ref, lnmb_ref, wfi_ref, wfo_ref,
                 lnfg_ref, lnfb_ref, wpg_ref, wpp_ref, *rest, alpha):
    n_cast = len(rest) // 2
    cast_src, o_ref, cast_dst = rest[:n_cast], rest[n_cast], rest[n_cast + 1:]
    _cast_rows(cast_src, cast_dst)

    rows = x_ref.shape[0]
    d_ff = wfo_ref.shape[0]
    chunks = list(range(0, d_ff, FF_CHUNK))
    assert sum(POST_PARTS) == rows
    parts = [slice(sum(POST_PARTS[:n]), sum(POST_PARTS[:n + 1])) for n in range(len(POST_PARTS))]
    st = [dict() for _ in parts]

    def out_proj(n):
        st[n]["mixed"] = _dot(y_ref[parts[n], :], wout_ref[...])

    def norm_mix(n):
        x1 = _layer_norm(alpha * x_ref[parts[n], :] + st[n].pop("mixed"), lnmg_ref[...], lnmb_ref[...])
        st[n]["x1"], st[n]["x1b"] = x1, x1.astype(BF16)

    def ffn_in(n, c0):
        x1b = st[n]["x1b"]
        st[n]["h", c0] = (_dot(x1b, wfi_ref[:, c0:c0 + FF_CHUNK]),
                          _dot(x1b, wfi_ref[:, d_ff + c0:d_ff + c0 + FF_CHUNK]))

    def ffn_out(n, k):
        if k + 1 < len(chunks):
            ffn_in(n, chunks[k + 1])
        hg, hu = st[n].pop(("h", chunks[k]))
        act = (hg * jax.nn.sigmoid(hg) * hu).astype(BF16)
        part = _dot(act, wfo_ref[chunks[k]:chunks[k] + FF_CHUNK, :])
        st[n]["acc"] = part if k == 0 else st[n]["acc"] + part

    def norm_ffn(n):
        st[n]["x2"] = _layer_norm(alpha * st[n].pop("x1") + st[n].pop("acc"), lnfg_ref[...],
                                  lnfb_ref[...])

    def embed(n):
        x2 = st[n].pop("x2")
        gate = jax.nn.sigmoid(_dot(x2.astype(BF16), wpg_ref[...]))
        emb = _dot(p_ref[parts[n], :].astype(BF16), wpp_ref[...])
        o_ref[parts[n], :] = x2 + gate * emb

    stages = [out_proj, norm_mix, lambda n: ffn_in(n, chunks[0])]
    stages += [functools.partial(ffn_out, k=k) for k in range(len(chunks))]
    stages += [norm_ffn, embed]
    for t in range(len(stages) + POST_LAG * (len(parts) - 1)):
        for n in range(len(parts)):
            k = t - n * POST_LAG
            if 0 <= k < len(stages):
                stages[k](n)


def _post(x2d, y2d, p3d, layer, consts, alpha, to_cast):
    seq, d_model = x2d.shape
    n_steps = seq // POST_ROWS
    cast_in, cast_out, cast_shapes = _cast_specs(to_cast, n_steps)
    kern = functools.partial(_post_kernel, alpha=alpha)
    outs = pl.pallas_call(
        kern,
        grid=(n_steps,),
        in_specs=[
            pl.BlockSpec((POST_ROWS, d_model), lambda i: (i, 0)),
            pl.BlockSpec((POST_ROWS, d_model), lambda i: (i, 0)),
            pl.BlockSpec((None, POST_ROWS, p3d.shape[2]), lambda i: (layer, i, 0)),
        ] + [spec for _, spec in consts] + cast_in,
        out_specs=[pl.BlockSpec((POST_ROWS, d_model), lambda i: (i, 0))] + cast_out,
        out_shape=[jax.ShapeDtypeStruct((seq, d_model), F32)] + cast_shapes,
        compiler_params=pltpu.CompilerParams(
            dimension_semantics=("arbitrary",), vmem_limit_bytes=VMEM_LIMIT),
        name="post",
    )(x2d, y2d, p3d, *[arr for arr, _ in consts], *[stacked for stacked, _, _ in to_cast])
    return outs[0], outs[1:]


def kernel(x, p, even_w_in, even_w_s, even_b_s, even_sgu_ln_g, even_sgu_ln_b, even_w_gk_up, even_b_gk,
           even_gla_norm_g, even_w_out, odd_w_in, hgrn_lb_logits, odd_hgrn_norm_g, odd_w_out,
           ln_mix_g, ln_mix_b, w_ffn_in, w_ffn_out, ln_ffn_g, ln_ffn_b, w_ple_proj, w_ple_gate):
    batch, seq, d_model = x.shape
    depth = p.shape[0]
    assert batch == 1 and seq % POST_ROWS == 0 and seq % MIX_ROWS == 0
    alpha = (2.0 * depth) ** 0.25
    a_width = even_sgu_ln_g.shape[1]
    b_width = d_model - a_width
    key_width = even_w_gk_up.shape[2]
    assert w_ffn_out.shape[1] % FF_CHUNK == 0
    even_in_pad = -(-even_w_in.shape[2] // LANES) * LANES

    def big_weights(layer):
        j = layer // 2
        w_in, w_out = (even_w_in, even_w_out) if layer % 2 == 0 else (odd_w_in, odd_w_out)
        mats = [(w_in, j), (w_out, j), (w_ffn_in, layer), (w_ffn_out, layer), (w_ple_gate, layer),
                (w_ple_proj, layer)]
        return [(w, idx, -(-w.shape[2] // LANES) * LANES) for w, idx in mats]

    rows3 = lambda a: a.reshape(a.shape[0], 1, -1).astype(F32)
    even_w_gk_b = jnp.pad(even_w_gk_up, ((0, 0), (0, LANES - GK_RANK), (0, 0))).astype(BF16)
    even_bias = jnp.repeat(even_b_s.transpose(0, 2, 1), a_width // A_GROUPS, axis=2)
    even_small = (even_w_s, even_bias, rows3(even_sgu_ln_g), rows3(even_sgu_ln_b), even_w_gk_b,
                  rows3(even_b_gk), rows3(even_gla_norm_g))
    odd_norm_g = rows3(odd_hgrn_norm_g)
    lnm_g, lnm_b, lnf_g, lnf_b = rows3(ln_mix_g), rows3(ln_mix_b), rows3(ln_ffn_g), rows3(ln_ffn_b)

    big = _cast_weights(big_weights(0))
    assert big[0].shape[1] == even_in_pad
    xs = x.reshape(seq, d_model)
    p3 = p.reshape(depth, seq, -1)
    for layer in range(depth):
        j = layer // 2
        const = lambda a: (a, _const_spec(a.shape))
        w_in_b, w_out_b, wfi_b, wfo_b, wpg_b, wpp_b = big
        if layer % 2 == 0:
            y = _even_mixer(xs, [const(w_in_b)] + [_layer_operand(a, j) for a in even_small],
                            a_width, key_width, b_width)
        else:
            y = _hgrn_mixer(xs, [const(w_in_b), const(hgrn_lb_logits.astype(F32)),
                                 _layer_operand(odd_norm_g, j)], j)
        consts = [const(w_out_b), _layer_operand(lnm_g, layer), _layer_operand(lnm_b, layer),
                  const(wfi_b), const(wfo_b), _layer_operand(lnf_g, layer),
                  _layer_operand(lnf_b, layer), const(wpg_b), const(wpp_b)]
        xs, big = _post(xs, y, p3, layer, consts, alpha,
                        big_weights(layer + 1) if layer + 1 < depth else [])
    return xs.reshape(batch, seq, d_model)
```

```python
import functools

import jax
import jax.numpy as jnp
from jax import lax
from jax.experimental import pallas as pl
from jax.experimental.pallas import tpu as pltpu

F32 = jnp.float32
BF16 = jnp.bfloat16

LANES = 128
BF16_ROWS = 16
MXU_WIDTH = 256
LN_EPS = 1e-5
RMS_EPS = 1e-6
LOG2_E = 1.4426950408889634
GLA_GATE_NORMALIZER = 16.0
GK_RANK = 16
A_GROUPS = 4
SGU_CHUNK = 128
LIN_CHUNK = 64
MIX_ROWS = 512
POST_ROWS = 512
POST_PARTS = (256, 256)
POST_LAG = 2
FF_CHUNK = 256
CAST_STEPS = 8
PROJ_SLAB = MXU_WIDTH
UNIT = MXU_WIDTH
MAX_LOG2_SPAN = 100.0
VMEM_LIMIT = 56 * 1024 * 1024


def _dot(a, b):
    return jnp.dot(a, b, preferred_element_type=F32)


def _dot_nt(a, b):
    return lax.dot_general(a, b, (((1,), (1,)), ((), ())), preferred_element_type=F32)


def _dot_tn(a, b):
    return lax.dot_general(a, b, (((0,), (0,)), ((), ())), preferred_element_type=F32)


def _layer_norm(x, g, b):
    mu = jnp.mean(x, axis=-1, keepdims=True)
    xc = x - mu
    var = jnp.mean(xc * xc, axis=-1, keepdims=True)
    return xc * lax.rsqrt(var + LN_EPS) * g + b


def _const_spec(shape):
    nd = len(shape)
    return pl.BlockSpec(shape, lambda i: (0,) * nd, pipeline_mode=pl.Buffered(1))


def _layer_operand(stacked, layer):
    nd = stacked.ndim
    spec = pl.BlockSpec((None,) + stacked.shape[1:], lambda i: (layer,) + (0,) * (nd - 1),
                        pipeline_mode=pl.Buffered(1))
    return stacked, spec


def _causal_mask(c):
    rows = lax.broadcasted_iota(jnp.int32, (c, c), 0)
    cols = lax.broadcasted_iota(jnp.int32, (c, c), 1)
    return cols <= rows


def _cumsum_rows(x):
    tril_b = jnp.where(_causal_mask(x.shape[0]), 1.0, 0.0).astype(BF16)
    hi = x.astype(BF16)
    lo = (x - hi.astype(F32)).astype(BF16)
    return _dot(tril_b, hi) + _dot(tril_b, lo)


def _recurrence_prepare(q, k, lf2):
    c = q.shape[0]
    b = _cumsum_rows(lf2)
    b_mid = b[c // 2:c // 2 + 1, :]
    b_last = b[c - 1:c, :]
    span = jnp.maximum(b[0:1, :] - b_mid, b_mid - b_last)
    work = dict(
        qe=(q * jnp.exp2(b - b_mid)).astype(BF16),
        ke=(k * jnp.exp2(b_mid - b)).astype(BF16),
        qb=(q * jnp.exp2(b)).astype(BF16),
        kd=(k * jnp.exp2(b_last - b)).astype(BF16),
        dec=jnp.exp2(b_last))
    return work, span


def _only_head(a, h, heads_per_group):
    width = a.shape[1]
    head_k = LANES // heads_per_group
    if heads_per_group == 1:
        zeros = jnp.zeros((a.shape[0], LANES), a.dtype)
        return jnp.concatenate([a[:, g * LANES:(g + 1) * LANES] if g == h else zeros
                                for g in range(width // LANES)], axis=1)
    lane = lax.broadcasted_iota(jnp.int32, a.shape, 1)
    return jnp.where((lane >= h * head_k) & (lane < (h + 1) * head_k), a, jnp.zeros_like(a))


def _block_diagonal(blocks):
    n = len(blocks)
    zeros = jnp.zeros_like(blocks[0])
    return jnp.concatenate(
        [jnp.concatenate([blk if j == i else zeros for j in range(n)], axis=1)
         for i, blk in enumerate(blocks)], axis=0)


def _recurrence_scores(work, v_heads, heads_per_group):
    c = v_heads[0].shape[0]
    n_heads = len(v_heads)
    kblk = jnp.concatenate([_only_head(work["ke"], h, heads_per_group) for h in range(n_heads)],
                           axis=0)
    rows = lax.broadcasted_iota(jnp.int32, (c, n_heads * c), 0)
    cols = lax.broadcasted_iota(jnp.int32, (c, n_heads * c), 1)
    causal = (cols & (c - 1)) <= rows
    work["p"] = jnp.where(causal, _dot_nt(work["qe"], kblk), 0.0).astype(BF16)
    work["upd"] = []
    for gi in range(work["kd"].shape[1] // LANES):
        upd = None
        for s in range(heads_per_group):
            h = gi * heads_per_group + s
            kd_h = _only_head(work["kd"], h, heads_per_group)[:, gi * LANES:(gi + 1) * LANES]
            u = _dot_tn(v_heads[h], kd_h)
            upd = u if upd is None else upd + u
        work["upd"].append(upd)


def _recurrence_outputs(work, v_heads, st_ref, g0, heads_per_group):
    n_heads = len(v_heads)
    n_groups = n_heads // heads_per_group
    states = [st_ref[g0 + gi] for gi in range(n_groups)]
    st_unit = jnp.concatenate([st.astype(BF16) for st in states], axis=1)
    stblk = jnp.concatenate([_only_head(st_unit, h, heads_per_group) for h in range(n_heads)],
                            axis=0)
    o_carried = _dot_nt(work["qb"], stblk)
    o_intra = _dot(work["p"], _block_diagonal(v_heads))
    for gi, st in enumerate(states):
        st_ref[g0 + gi] = st * work["dec"][:, gi * LANES:(gi + 1) * LANES] + work["upd"][gi]
    return [(o_carried[:, h * LANES:(h + 1) * LANES], o_intra[:, h * LANES:(h + 1) * LANES])
            for h in range(n_heads)]


def _exact_scores(q_ref, k_ref, b_ref, heads_per_group):
    c, width = q_ref.shape
    head_k = LANES // heads_per_group
    q = q_ref[...]
    b = b_ref[...]
    col = lax.broadcasted_iota(jnp.int32, (c, c), 1)
    lane = lax.broadcasted_iota(jnp.int32, (c, LANES), 1)
    n_heads = (width // LANES) * heads_per_group

    def body(s, ps):
        k_s = k_ref[pl.ds(s, 1), :]
        b_s = b_ref[pl.ds(s, 1), :]
        w = q * k_s * jnp.exp2(jnp.minimum(b - b_s, 0.0))
        new = []
        for h in range(n_heads):
            gi, sub = divmod(h, heads_per_group)
            wg = w[:, gi * LANES:(gi + 1) * LANES]
            if heads_per_group > 1:
                wg = jnp.where((lane >= sub * head_k) & (lane < (sub + 1) * head_k), wg, 0.0)
            new.append(jnp.where(col == s, jnp.sum(wg, axis=-1, keepdims=True), ps[h]))
        return tuple(new)

    ps = lax.fori_loop(0, c, body, tuple(jnp.zeros((c, c), F32) for _ in range(n_heads)))
    causal = _causal_mask(c)
    return [jnp.where(causal, p, 0.0).astype(BF16) for p in ps]


def _run_recurrence(n_chunks, n_units, prepare, scores, outputs, finish, extra, slot):
    for u in range(n_units):
        prepare(0, u)
        if n_units > 1:
            slot()
    for c in range(n_chunks):
        for u in range(n_units):
            if c + 1 < n_chunks:
                prepare(c + 1, u)
            scores(c, u)
            slot()
        extra(c)
        for u in range(n_units):
            outputs(c, u)
            if u > 0:
                finish(c, u - 1)
            slot()
        finish(c, n_units - 1)


def _recurrent_mixer(n_chunks, n_units, heads_per_group, gates, values, finish, st_ref, oi_ref,
                     exact_refs, extra, slot, flush):
    heads_per_unit = (UNIT // LANES) * heads_per_group
    work, spans = {}, []

    def prepare(c, u):
        work[c, u], span = _recurrence_prepare(*gates(c, u))
        spans.append(span)

    def scores(c, u):
        _recurrence_scores(work[c, u], values(c, u), heads_per_group)

    def outputs(c, u):
        outs = []
        pairs = _recurrence_outputs(work[c, u], values(c, u), st_ref, u * (UNIT // LANES),
                                    heads_per_group)
        for n, (o_carried, o_intra) in enumerate(pairs):
            oi_ref[c, u * heads_per_unit + n] = o_carried
            outs.append(o_carried + o_intra)
        work[c, u] = outs

    _run_recurrence(n_chunks, n_units, prepare, scores, outputs,
                    lambda c, u: finish(c, u, work.pop((c, u))), extra, slot)
    flush()

    worst = functools.reduce(jnp.maximum, spans)

    @pl.when(jnp.max(worst) > MAX_LOG2_SPAN)
    def _():
        q_ref, k_ref, b_ref = exact_refs
        for c in range(n_chunks):
            for u in range(n_units):
                q, k, lf2 = gates(c, u)
                q_ref[...] = q
                k_ref[...] = k
                b_ref[...] = _cumsum_rows(lf2)
                ps = _exact_scores(q_ref, k_ref, b_ref, heads_per_group)
                vs = values(c, u)
                finish(c, u, [oi_ref[c, u * heads_per_unit + n] + _dot(ps[n], vs[n])
                              for n in range(heads_per_unit)])


def _pipelined_mixer_step(x_ref, win_ref, buf_a, buf_b, st_ref, mix, n_slots):
    i = pl.program_id(0)

    @pl.when(i == 0)
    def _():
        st_ref[...] = jnp.zeros_like(st_ref)
        buf_b[...] = jnp.zeros_like(buf_b)

    def step(w_buf, r_buf):
        xb = x_ref[...].astype(BF16)
        width = w_buf.shape[1]
        pending = list(range(0, width, PROJ_SLAB))
        n_slabs = len(pending)
        seen = [0]

        def project_slab():
            lo = pending.pop(0)
            hi = min(lo + PROJ_SLAB, width)
            w_buf[:, lo:hi] = _dot(xb, win_ref[:, lo:hi])

        def slot():
            seen[0] += 1
            while pending and (n_slabs - len(pending)) * n_slots < seen[0] * n_slabs:
                project_slab()

        def flush():
            while pending:
                project_slab()

        mix(r_buf, slot, flush)

    @pl.when(lax.rem(i, 2) == 0)
    def _():
        step(buf_a, buf_b)

    @pl.when(lax.rem(i, 2) == 1)
    def _():
        step(buf_b, buf_a)


def _mixer_call(kern, name, x2d, consts, n_state, n_heads):
    seq, d_model = x2d.shape
    proj_width = consts[0][0].shape[-1]
    n_blocks = seq // MIX_ROWS
    return pl.pallas_call(
        kern,
        grid=(n_blocks + 1,),
        in_specs=[pl.BlockSpec((MIX_ROWS, d_model), lambda i: (jnp.minimum(i, n_blocks - 1), 0))]
        + [spec for _, spec in consts],
        out_specs=pl.BlockSpec((MIX_ROWS, d_model), lambda i: (jnp.maximum(i - 1, 0), 0)),
        out_shape=jax.ShapeDtypeStruct((seq, d_model), BF16),
        scratch_shapes=[
            pltpu.VMEM((MIX_ROWS, proj_width), F32),
            pltpu.VMEM((MIX_ROWS, proj_width), F32),
            pltpu.VMEM((n_state, LANES, LANES), F32),
            pltpu.VMEM((MIX_ROWS // LIN_CHUNK, n_heads, LIN_CHUNK, LANES), F32),
            pltpu.VMEM((LIN_CHUNK, UNIT), F32),
            pltpu.VMEM((LIN_CHUNK, UNIT), F32),
            pltpu.VMEM((LIN_CHUNK, UNIT), F32),
        ],
        compiler_params=pltpu.CompilerParams(
            dimension_semantics=("arbitrary",), vmem_limit_bytes=VMEM_LIMIT),
        name=name,
    )(x2d, *[arr for arr, _ in consts])


def _hgrn_kernel(x_ref, win_ref, lbl_ref, ng_ref, o_ref, buf_a, buf_b, st_ref, oi_ref,
                 eq_ref, ek_ref, eb_ref, *, layer_j, d_model):
    logits = lbl_ref[...]
    e = jnp.exp(logits - jnp.max(logits, axis=0, keepdims=True))
    probs = e / jnp.sum(e, axis=0, keepdims=True)
    lb = jnp.zeros((1, d_model), F32)
    for r in range(1, layer_j + 1):
        lb = lb + probs[r:r + 1, :]
    one_m_lb = 1.0 - lb
    rows = x_ref.shape[0]
    heads_per_unit = UNIT // LANES

    def mix(proj_ref, slot, flush):
        def gates(c, u):
            rs = slice(c * LIN_CHUNK, (c + 1) * LIN_CHUNK)
            ls = slice(u * UNIT, (u + 1) * UNIT)
            q = proj_ref[rs, u * UNIT:(u + 1) * UNIT]
            f = proj_ref[rs, d_model + u * UNIT:d_model + (u + 1) * UNIT]
            ef = jnp.exp(-jnp.abs(f))
            inv = 1.0 / (1.0 + ef)
            pos = f >= 0.0
            forget = jnp.where(pos, 1.0 + lb[:, ls] * ef, lb[:, ls] + ef) * inv
            lf2 = jnp.where(forget > 0.0, jnp.log2(forget), jnp.minimum(f, 0.0) * LOG2_E)
            sig_neg = jnp.where(pos, ef, 1.0) * inv
            return q * jax.nn.sigmoid(q), one_m_lb[:, ls] * sig_neg, lf2

        def values(c, u):
            rs = slice(c * LIN_CHUNK, (c + 1) * LIN_CHUNK)
            return [proj_ref[rs, 2 * d_model + h * LANES:2 * d_model + (h + 1) * LANES].astype(BF16)
                    for h in range(u * heads_per_unit, (u + 1) * heads_per_unit)]

        def finish(c, u, outs):
            rs = slice(c * LIN_CHUNK, (c + 1) * LIN_CHUNK)
            ng = ng_ref[...]
            for n, o in enumerate(outs):
                h = u * heads_per_unit + n
                ms = jnp.mean(o * o, axis=-1, keepdims=True)
                g = proj_ref[rs, 3 * d_model + h * LANES:3 * d_model + (h + 1) * LANES]
                y = o * lax.rsqrt(ms + RMS_EPS) * ng * jax.nn.sigmoid(g)
                o_ref[rs, h * LANES:(h + 1) * LANES] = y.astype(o_ref.dtype)

        _recurrent_mixer(rows // LIN_CHUNK, d_model // UNIT, 1, gates, values, finish, st_ref,
                         oi_ref, (eq_ref, ek_ref, eb_ref), lambda c: None, slot, flush)

    n_slots = (2 * (rows // LIN_CHUNK) + 1) * (d_model // UNIT)
    _pipelined_mixer_step(x_ref, win_ref, buf_a, buf_b, st_ref, mix, n_slots)


def _hgrn_mixer(x2d, consts, layer_j):
    d_model = x2d.shape[1]
    kern = functools.partial(_hgrn_kernel, layer_j=layer_j, d_model=d_model)
    return _mixer_call(kern, "hgrn_mixer", x2d, consts, d_model // LANES, d_model // LANES)


def _even_kernel(x_ref, win_ref, ws_ref, bs_ref, lng_ref, lnb_ref, wgk_ref, bgk_ref, ng_ref,
                 o_ref, buf_a, buf_b, st_ref, oi_ref, eq_ref, ek_ref, eb_ref, *, a_width, key_width,
                 b_width):
    rows = x_ref.shape[0]
    o_q = 2 * a_width
    o_k = o_q + key_width
    o_v = o_k + key_width
    o_g = o_v + b_width
    o_gk = o_g + b_width
    group_dim = a_width // A_GROUPS
    n_sgu = rows // SGU_CHUNK
    n_chunks = rows // LIN_CHUNK
    n_heads = b_width // LANES
    head_k = key_width // n_heads
    heads_per_group = LANES // head_k
    scale = head_k ** -0.5
    assert key_width == UNIT and n_chunks >= A_GROUPS

    def mix(proj_ref, slot, flush):
        def sgu_group(gi):
            ls = slice(gi * group_dim, (gi + 1) * group_dim)
            v = jax.nn.gelu(proj_ref[:, a_width + gi * group_dim:a_width + (gi + 1) * group_dim])
            vn = _layer_norm(v, lng_ref[:, ls], lnb_ref[:, ls]).astype(BF16)
            rhs = jnp.concatenate(
                [vn[n * SGU_CHUNK:(n + 1) * SGU_CHUNK, :] for n in range(n_sgu)], axis=1)
            w = jnp.where(_causal_mask(SGU_CHUNK), ws_ref[gi], 0.0).astype(BF16)
            mixed = _dot(w, rhs)
            bias = bs_ref[:, ls]
            for n in range(n_sgu):
                rs = slice(n * SGU_CHUNK, (n + 1) * SGU_CHUNK)
                u = jax.nn.gelu(proj_ref[rs, ls])
                y = u * (mixed[:, n * group_dim:(n + 1) * group_dim] + bias)
                o_ref[rs, ls] = y.astype(o_ref.dtype)

        def extra(c):
            if c < A_GROUPS:
                sgu_group(c)
                slot()

        def gates(c, u):
            rs = slice(c * LIN_CHUNK, (c + 1) * LIN_CHUNK)
            gk_low = proj_ref[rs, o_gk:o_gk + LANES].astype(BF16)
            z = _dot(gk_low, wgk_ref[...]) + bgk_ref[...]
            lf2 = (jnp.minimum(z, 0.0) * LOG2_E - jnp.log2(1.0 + jnp.exp(-jnp.abs(z)))) * (
                1.0 / GLA_GATE_NORMALIZER)
            q = proj_ref[rs, o_q:o_q + key_width] * scale
            k = proj_ref[rs, o_k:o_k + key_width]
            return q, k, lf2

        def values(c, u):
            rs = slice(c * LIN_CHUNK, (c + 1) * LIN_CHUNK)
            return [proj_ref[rs, o_v + h * LANES:o_v + (h + 1) * LANES].astype(BF16)
                    for h in range(n_heads)]

        def finish(c, u, outs):
            rs = slice(c * LIN_CHUNK, (c + 1) * LIN_CHUNK)
            ng = ng_ref[...]
            for h, o in enumerate(outs):
                ms = jnp.mean(o * o, axis=-1, keepdims=True)
                g = proj_ref[rs, o_g + h * LANES:o_g + (h + 1) * LANES]
                y = o * lax.rsqrt(ms + RMS_EPS) * ng * (g * jax.nn.sigmoid(g))
                o_ref[rs, a_width + h * LANES:a_width + (h + 1) * LANES] = y.astype(o_ref.dtype)

        _recurrent_mixer(n_chunks, 1, heads_per_group, gates, values, finish, st_ref, oi_ref,
                         (eq_ref, ek_ref, eb_ref), extra, slot, flush)

    _pipelined_mixer_step(x_ref, win_ref, buf_a, buf_b, st_ref, mix, 2 * n_chunks + A_GROUPS)


def _even_mixer(x2d, consts, a_width, key_width, b_width):
    kern = functools.partial(_even_kernel, a_width=a_width, key_width=key_width, b_width=b_width)
    return _mixer_call(kern, "even_mixer", x2d, consts, key_width // LANES, b_width // LANES)


def _cast_specs(to_cast, n_steps):
    in_specs, out_specs, out_shapes = [], [], []
    for stacked, index, width in to_cast:
        rows, src_width = stacked.shape[1:]
        span = 1
        while rows % (n_steps // span) or (rows // (n_steps // span)) % BF16_ROWS:
            span *= 2
        block_rows = rows // (n_steps // span)
        in_specs.append(pl.BlockSpec((None, block_rows, src_width),
                                     lambda i, index=index, span=span: (index, i // span, 0)))
        out_specs.append(pl.BlockSpec((block_rows, width), lambda i, span=span: (i // span, 0)))
        out_shapes.append(jax.ShapeDtypeStruct((rows, width), BF16))
    return in_specs, out_specs, out_shapes


def _cast_rows(srcs, dsts):
    for src, dst in zip(srcs, dsts):
        width = src.shape[1]
        dst[:, :width] = src[...].astype(dst.dtype)
        if dst.shape[1] > width:
            dst[:, width:] = jnp.zeros((dst.shape[0], dst.shape[1] - width), dst.dtype)


def _cast_kernel(*refs):
    _cast_rows(refs[:len(refs) // 2], refs[len(refs) // 2:])


def _cast_weights(to_cast):
    in_specs, out_specs, out_shapes = _cast_specs(to_cast, CAST_STEPS)
    return pl.pallas_call(
        _cast_kernel,
        grid=(CAST_STEPS,),
        in_specs=in_specs,
        out_specs=out_specs,
        out_shape=out_shapes,
        compiler_params=pltpu.CompilerParams(
            dimension_semantics=("arbitrary",), vmem_limit_bytes=VMEM_LIMIT),
        name="cast_weights",
    )(*[stacked for stacked, _, _ in to_cast])


def _post_kernel(x_ref, y_ref, p_ref, wout_ref, lnmg_ref, lnmb_ref, wfi_ref, wfo_ref,
                 lnfg_ref, lnfb_ref, wpg_ref, wpp_ref, *rest, alpha):
    n_cast = len(rest) // 2
    cast_src, o_ref, cast_dst = rest[:n_cast], rest[n_cast], rest[n_cast + 1:]
    _cast_rows(cast_src, cast_dst)

    rows = x_ref.shape[0]
    d_ff = wfo_ref.shape[0]
    chunks = list(range(0, d_ff, FF_CHUNK))
    assert sum(POST_PARTS) == rows
    parts = [slice(sum(POST_PARTS[:n]), sum(POST_PARTS[:n + 1])) for n in range(len(POST_PARTS))]
    st = [dict() for _ in parts]

    def out_proj(n):
        st[n]["mixed"] = _dot(y_ref[parts[n], :], wout_ref[...])

    def norm_mix(n):
        x1 = _layer_norm(alpha * x_ref[parts[n], :] + st[n].pop("mixed"), lnmg_ref[...], lnmb_ref[...])
        st[n]["x1"], st[n]["x1b"] = x1, x1.astype(BF16)

    def ffn_in(n, c0):
        x1b = st[n]["x1b"]
        st[n]["h", c0] = (_dot(x1b, wfi_ref[:, c0:c0 + FF_CHUNK]),
                          _dot(x1b, wfi_ref[:, d_ff + c0:d_ff + c0 + FF_CHUNK]))

    def ffn_out(n, k):
        if k + 1 < len(chunks):
            ffn_in(n, chunks[k + 1])
        hg, hu = st[n].pop(("h", chunks[k]))
        act = (hg * jax.nn.sigmoid(hg) * hu).astype(BF16)
        part = _dot(act, wfo_ref[chunks[k]:chunks[k] + FF_CHUNK, :])
        st[n]["acc"] = part if k == 0 else st[n]["acc"] + part

    def norm_ffn(n):
        st[n]["x2"] = _layer_norm(alpha * st[n].pop("x1") + st[n].pop("acc"), lnfg_ref[...],
                                  lnfb_ref[...])

    def embed(n):
        x2 = st[n].pop("x2")
        gate = jax.nn.sigmoid(_dot(x2.astype(BF16), wpg_ref[...]))
        emb = _dot(p_ref[parts[n], :].astype(BF16), wpp_ref[...])
        o_ref[parts[n], :] = x2 + gate * emb

    stages = [out_proj, norm_mix, lambda n: ffn_in(n, chunks[0])]
    stages += [functools.partial(ffn_out, k=k) for k in range(len(chunks))]
    stages += [norm_ffn, embed]
    for t in range(len(stages) + POST_LAG * (len(parts) - 1)):
        for n in range(len(parts)):
            k = t - n * POST_LAG
            if 0 <= k < len(stages):
                stages[k](n)


def _post(x2d, y2d, p3d, layer, consts, alpha, to_cast):
    seq, d_model = x2d.shape
    n_steps = seq // POST_ROWS
    cast_in, cast_out, cast_shapes = _cast_specs(to_cast, n_steps)
    kern = functools.partial(_post_kernel, alpha=alpha)
    outs = pl.pallas_call(
        kern,
        grid=(n_steps,),
        in_specs=[
            pl.BlockSpec((POST_ROWS, d_model), lambda i: (i, 0)),
            pl.BlockSpec((POST_ROWS, d_model), lambda i: (i, 0)),
            pl.BlockSpec((None, POST_ROWS, p3d.shape[2]), lambda i: (layer, i, 0)),
        ] + [spec for _, spec in consts] + cast_in,
        out_specs=[pl.BlockSpec((POST_ROWS, d_model), lambda i: (i, 0))] + cast_out,
        out_shape=[jax.ShapeDtypeStruct((seq, d_model), F32)] + cast_shapes,
        compiler_params=pltpu.CompilerParams(
            dimension_semantics=("arbitrary",), vmem_limit_bytes=VMEM_LIMIT),
        name="post",
    )(x2d, y2d, p3d, *[arr for arr, _ in consts], *[stacked for stacked, _, _ in to_cast])
    return outs[0], outs[1:]


def kernel(x, p, even_w_in, even_w_s, even_b_s, even_sgu_ln_g, even_sgu_ln_b, even_w_gk_up, even_b_gk,
           even_gla_norm_g, even_w_out, odd_w_in, hgrn_lb_logits, odd_hgrn_norm_g, odd_w_out,
           ln_mix_g, ln_mix_b, w_ffn_in, w_ffn_out, ln_ffn_g, ln_ffn_b, w_ple_proj, w_ple_gate):
    batch, seq, d_model = x.shape
    depth = p.shape[0]
    assert batch == 1 and seq % POST_ROWS == 0 and seq % MIX_ROWS == 0
    alpha = (2.0 * depth) ** 0.25
    a_width = even_sgu_ln_g.shape[1]
    b_width = d_model - a_width
    key_width = even_w_gk_up.shape[2]
    assert w_ffn_out.shape[1] % FF_CHUNK == 0
    even_in_pad = -(-even_w_in.shape[2] // LANES) * LANES

    def big_weights(layer):
        j = layer // 2
        w_in, w_out = (even_w_in, even_w_out) if layer % 2 == 0 else (odd_w_in, odd_w_out)
        mats = [(w_in, j), (w_out, j), (w_ffn_in, layer), (w_ffn_out, layer), (w_ple_gate, layer),
                (w_ple_proj, layer)]
        return [(w, idx, -(-w.shape[2] // LANES) * LANES) for w, idx in mats]

    rows3 = lambda a: a.reshape(a.shape[0], 1, -1).astype(F32)
    even_w_gk_b = jnp.pad(even_w_gk_up, ((0, 0), (0, LANES - GK_RANK), (0, 0))).astype(BF16)
    even_bias = jnp.repeat(even_b_s.transpose(0, 2, 1), a_width // A_GROUPS, axis=2)
    even_small = (even_w_s, even_bias, rows3(even_sgu_ln_g), rows3(even_sgu_ln_b), even_w_gk_b,
                  rows3(even_b_gk), rows3(even_gla_norm_g))
    odd_norm_g = rows3(odd_hgrn_norm_g)
    lnm_g, lnm_b, lnf_g, lnf_b = rows3(ln_mix_g), rows3(ln_mix_b), rows3(ln_ffn_g), rows3(ln_ffn_b)

    big = _cast_weights(big_weights(0))
    assert big[0].shape[1] == even_in_pad
    xs = x.reshape(seq, d_model)
    p3 = p.reshape(depth, seq, -1)
    for layer in range(depth):
        j = layer // 2
        const = lambda a: (a, _const_spec(a.shape))
        w_in_b, w_out_b, wfi_b, wfo_b, wpg_b, wpp_b = big
        if layer % 2 == 0:
            y = _even_mixer(xs, [const(w_in_b)] + [_layer_operand(a, j) for a in even_small],
                            a_width, key_width, b_width)
        else:
            y = _hgrn_mixer(xs, [const(w_in_b), const(hgrn_lb_logits.astype(F32)),
                                 _layer_operand(odd_norm_g, j)], j)
        consts = [const(w_out_b), _layer_operand(lnm_g, layer), _layer_operand(lnm_b, layer),
                  const(wfi_b), const(wfo_b), _layer_operand(lnf_g, layer),
                  _layer_operand(lnf_b, layer), const(wpg_b), const(wpp_b)]
        xs, big = _post(xs, y, p3, layer, consts, alpha,
                        big_weights(layer + 1) if layer + 1 < depth else [])
    return xs.reshape(batch, seq, d_model)
```

```python
import functools

import jax
import jax.numpy as jnp
from jax import lax
from jax.experimental import pallas as pl
from jax.experimental.pallas import tpu as pltpu

F32 = jnp.float32
BF16 = jnp.bfloat16

LANES = 128
BF16_ROWS = 16
MXU_WIDTH = 256
LN_EPS = 1e-5
RMS_EPS = 1e-6
LOG2_E = 1.4426950408889634
GLA_GATE_NORMALIZER = 16.0
GK_RANK = 16
A_GROUPS = 4
SGU_CHUNK = 128
LIN_CHUNK = 64
MIX_ROWS = 256
POST_ROWS = 512
POST_PARTS = (256, 256)
POST_LAG = 2
FF_CHUNK = 256
CAST_STEPS = 8
PROJ_SLAB = MXU_WIDTH
UNIT = MXU_WIDTH
MAX_LOG2_SPAN = 100.0
VMEM_LIMIT = 56 * 1024 * 1024


def _dot(a, b):
    return jnp.dot(a, b, preferred_element_type=F32)


def _dot_nt(a, b):
    return lax.dot_general(a, b, (((1,), (1,)), ((), ())), preferred_element_type=F32)


def _dot_tn(a, b):
    return lax.dot_general(a, b, (((0,), (0,)), ((), ())), preferred_element_type=F32)


def _layer_norm(x, g, b):
    mu = jnp.mean(x, axis=-1, keepdims=True)
    xc = x - mu
    var = jnp.mean(xc * xc, axis=-1, keepdims=True)
    return xc * lax.rsqrt(var + LN_EPS) * g + b


def _const_spec(shape):
    nd = len(shape)
    return pl.BlockSpec(shape, lambda i: (0,) * nd, pipeline_mode=pl.Buffered(1))


def _layer_operand(stacked, layer):
    nd = stacked.ndim
    spec = pl.BlockSpec((None,) + stacked.shape[1:], lambda i: (layer,) + (0,) * (nd - 1),
                        pipeline_mode=pl.Buffered(1))
    return stacked, spec


def _causal_mask(c):
    rows = lax.broadcasted_iota(jnp.int32, (c, c), 0)
    cols = lax.broadcasted_iota(jnp.int32, (c, c), 1)
    return cols <= rows


def _cumsum_rows(x):
    tril_b = jnp.where(_causal_mask(x.shape[0]), 1.0, 0.0).astype(BF16)
    hi = x.astype(BF16)
    lo = (x - hi.astype(F32)).astype(BF16)
    return _dot(tril_b, hi) + _dot(tril_b, lo)


def _recurrence_prepare(q, k, lf2):
    c = q.shape[0]
    b = _cumsum_rows(lf2)
    b_mid = b[c // 2:c // 2 + 1, :]
    b_last = b[c - 1:c, :]
    span = jnp.maximum(b[0:1, :] - b_mid, b_mid - b_last)
    work = dict(
        qe=(q * jnp.exp2(b - b_mid)).astype(BF16),
        ke=(k * jnp.exp2(b_mid - b)).astype(BF16),
        qb=(q * jnp.exp2(b)).astype(BF16),
        kd=(k * jnp.exp2(b_last - b)).astype(BF16),
        dec=jnp.exp2(b_last))
    return work, span


def _only_head(a, h, heads_per_group):
    width = a.shape[1]
    head_k = LANES // heads_per_group
    if heads_per_group == 1:
        zeros = jnp.zeros((a.shape[0], LANES), a.dtype)
        return jnp.concatenate([a[:, g * LANES:(g + 1) * LANES] if g == h else zeros
                                for g in range(width // LANES)], axis=1)
    lane = lax.broadcasted_iota(jnp.int32, a.shape, 1)
    return jnp.where((lane >= h * head_k) & (lane < (h + 1) * head_k), a, jnp.zeros_like(a))


def _block_diagonal(blocks):
    n = len(blocks)
    zeros = jnp.zeros_like(blocks[0])
    return jnp.concatenate(
        [jnp.concatenate([blk if j == i else zeros for j in range(n)], axis=1)
         for i, blk in enumerate(blocks)], axis=0)


def _recurrence_scores(work, v_heads, heads_per_group):
    c = v_heads[0].shape[0]
    n_heads = len(v_heads)
    kblk = jnp.concatenate([_only_head(work["ke"], h, heads_per_group) for h in range(n_heads)],
                           axis=0)
    rows = lax.broadcasted_iota(jnp.int32, (c, n_heads * c), 0)
    cols = lax.broadcasted_iota(jnp.int32, (c, n_heads * c), 1)
    causal = (cols & (c - 1)) <= rows
    work["p"] = jnp.where(causal, _dot_nt(work["qe"], kblk), 0.0).astype(BF16)
    work["upd"] = []
    for gi in range(work["kd"].shape[1] // LANES):
        upd = None
        for s in range(heads_per_group):
            h = gi * heads_per_group + s
            kd_h = _only_head(work["kd"], h, heads_per_group)[:, gi * LANES:(gi + 1) * LANES]
            u = _dot_tn(v_heads[h], kd_h)
            upd = u if upd is None else upd + u
        work["upd"].append(upd)


def _recurrence_outputs(work, v_heads, st_ref, g0, heads_per_group):
    n_heads = len(v_heads)
    n_groups = n_heads // heads_per_group
    states = [st_ref[g0 + gi] for gi in range(n_groups)]
    st_unit = jnp.concatenate([st.astype(BF16) for st in states], axis=1)
    stblk = jnp.concatenate([_only_head(st_unit, h, heads_per_group) for h in range(n_heads)],
                            axis=0)
    o_carried = _dot_nt(work["qb"], stblk)
    o_intra = _dot(work["p"], _block_diagonal(v_heads))
    for gi, st in enumerate(states):
        st_ref[g0 + gi] = st * work["dec"][:, gi * LANES:(gi + 1) * LANES] + work["upd"][gi]
    return [(o_carried[:, h * LANES:(h + 1) * LANES], o_intra[:, h * LANES:(h + 1) * LANES])
            for h in range(n_heads)]


def _exact_scores(q_ref, k_ref, b_ref, heads_per_group):
    c, width = q_ref.shape
    head_k = LANES // heads_per_group
    q = q_ref[...]
    b = b_ref[...]
    col = lax.broadcasted_iota(jnp.int32, (c, c), 1)
    lane = lax.broadcasted_iota(jnp.int32, (c, LANES), 1)
    n_heads = (width // LANES) * heads_per_group

    def body(s, ps):
        k_s = k_ref[pl.ds(s, 1), :]
        b_s = b_ref[pl.ds(s, 1), :]
        w = q * k_s * jnp.exp2(jnp.minimum(b - b_s, 0.0))
        new = []
        for h in range(n_heads):
            gi, sub = divmod(h, heads_per_group)
            wg = w[:, gi * LANES:(gi + 1) * LANES]
            if heads_per_group > 1:
                wg = jnp.where((lane >= sub * head_k) & (lane < (sub + 1) * head_k), wg, 0.0)
            new.append(jnp.where(col == s, jnp.sum(wg, axis=-1, keepdims=True), ps[h]))
        return tuple(new)

    ps = lax.fori_loop(0, c, body, tuple(jnp.zeros((c, c), F32) for _ in range(n_heads)))
    causal = _causal_mask(c)
    return [jnp.where(causal, p, 0.0).astype(BF16) for p in ps]


def _run_recurrence(n_chunks, n_units, prepare, scores, outputs, finish, extra, slot):
    for u in range(n_units):
        prepare(0, u)
        if n_units > 1:
            slot()
    for c in range(n_chunks):
        for u in range(n_units):
            if c + 1 < n_chunks:
                prepare(c + 1, u)
            scores(c, u)
            slot()
        extra(c)
        for u in range(n_units):
            outputs(c, u)
            if u > 0:
                finish(c, u - 1)
            slot()
        finish(c, n_units - 1)


def _recurrent_mixer(n_chunks, n_units, heads_per_group, gates, values, finish, st_ref, oi_ref,
                     exact_refs, extra, slot, flush):
    heads_per_unit = (UNIT // LANES) * heads_per_group
    work, spans = {}, []

    def prepare(c, u):
        work[c, u], span = _recurrence_prepare(*gates(c, u))
        spans.append(span)

    def scores(c, u):
        _recurrence_scores(work[c, u], values(c, u), heads_per_group)

    def outputs(c, u):
        outs = []
        pairs = _recurrence_outputs(work[c, u], values(c, u), st_ref, u * (UNIT // LANES),
                                    heads_per_group)
        for n, (o_carried, o_intra) in enumerate(pairs):
            oi_ref[c, u * heads_per_unit + n] = o_carried
            outs.append(o_carried + o_intra)
        work[c, u] = outs

    _run_recurrence(n_chunks, n_units, prepare, scores, outputs,
                    lambda c, u: finish(c, u, work.pop((c, u))), extra, slot)
    flush()

    worst = functools.reduce(jnp.maximum, spans)

    @pl.when(jnp.max(worst) > MAX_LOG2_SPAN)
    def _():
        q_ref, k_ref, b_ref = exact_refs
        for c in range(n_chunks):
            for u in range(n_units):
                q, k, lf2 = gates(c, u)
                q_ref[...] = q
                k_ref[...] = k
                b_ref[...] = _cumsum_rows(lf2)
                ps = _exact_scores(q_ref, k_ref, b_ref, heads_per_group)
                vs = values(c, u)
                finish(c, u, [oi_ref[c, u * heads_per_unit + n] + _dot(ps[n], vs[n])
                              for n in range(heads_per_unit)])


def _pipelined_mixer_step(x_ref, win_ref, buf_a, buf_b, st_ref, mix, n_slots):
    i = pl.program_id(0)

    @pl.when(i == 0)
    def _():
        st_ref[...] = jnp.zeros_like(st_ref)
        buf_b[...] = jnp.zeros_like(buf_b)

    def step(w_buf, r_buf):
        xb = x_ref[...].astype(BF16)
        width = w_buf.shape[1]
        pending = list(range(0, width, PROJ_SLAB))
        n_slabs = len(pending)
        seen = [0]

        def project_slab():
            lo = pending.pop(0)
            hi = min(lo + PROJ_SLAB, width)
            w_buf[:, lo:hi] = _dot(xb, win_ref[:, lo:hi])

        def slot():
            seen[0] += 1
            while pending and (n_slabs - len(pending)) * n_slots < seen[0] * n_slabs:
                project_slab()

        def flush():
            while pending:
                project_slab()

        mix(r_buf, slot, flush)

    @pl.when(lax.rem(i, 2) == 0)
    def _():
        step(buf_a, buf_b)

    @pl.when(lax.rem(i, 2) == 1)
    def _():
        step(buf_b, buf_a)


def _mixer_call(kern, name, x2d, consts, n_state, n_heads):
    seq, d_model = x2d.shape
    proj_width = consts[0][0].shape[-1]
    n_blocks = seq // MIX_ROWS
    return pl.pallas_call(
        kern,
        grid=(n_blocks + 1,),
        in_specs=[pl.BlockSpec((MIX_ROWS, d_model), lambda i: (jnp.minimum(i, n_blocks - 1), 0))]
        + [spec for _, spec in consts],
        out_specs=pl.BlockSpec((MIX_ROWS, d_model), lambda i: (jnp.maximum(i - 1, 0), 0)),
        out_shape=jax.ShapeDtypeStruct((seq, d_model), BF16),
        scratch_shapes=[
            pltpu.VMEM((MIX_ROWS, proj_width), F32),
            pltpu.VMEM((MIX_ROWS, proj_width), F32),
            pltpu.VMEM((n_state, LANES, LANES), F32),
            pltpu.VMEM((MIX_ROWS // LIN_CHUNK, n_heads, LIN_CHUNK, LANES), F32),
            pltpu.VMEM((LIN_CHUNK, UNIT), F32),
            pltpu.VMEM((LIN_CHUNK, UNIT), F32),
            pltpu.VMEM((LIN_CHUNK, UNIT), F32),
        ],
        compiler_params=pltpu.CompilerParams(
            dimension_semantics=("arbitrary",), vmem_limit_bytes=VMEM_LIMIT),
        name=name,
    )(x2d, *[arr for arr, _ in consts])


def _hgrn_kernel(x_ref, win_ref, lbl_ref, ng_ref, o_ref, buf_a, buf_b, st_ref, oi_ref,
                 eq_ref, ek_ref, eb_ref, *, layer_j, d_model):
    logits = lbl_ref[...]
    e = jnp.exp(logits - jnp.max(logits, axis=0, keepdims=True))
    probs = e / jnp.sum(e, axis=0, keepdims=True)
    lb = jnp.zeros((1, d_model), F32)
    for r in range(1, layer_j + 1):
        lb = lb + probs[r:r + 1, :]
    one_m_lb = 1.0 - lb
    rows = x_ref.shape[0]
    heads_per_unit = UNIT // LANES

    def mix(proj_ref, slot, flush):
        def gates(c, u):
            rs = slice(c * LIN_CHUNK, (c + 1) * LIN_CHUNK)
            ls = slice(u * UNIT, (u + 1) * UNIT)
            q = proj_ref[rs, u * UNIT:(u + 1) * UNIT]
            f = proj_ref[rs, d_model + u * UNIT:d_model + (u + 1) * UNIT]
            ef = jnp.exp(-jnp.abs(f))
            inv = 1.0 / (1.0 + ef)
            pos = f >= 0.0
            forget = jnp.where(pos, 1.0 + lb[:, ls] * ef, lb[:, ls] + ef) * inv
            lf2 = jnp.where(forget > 0.0, jnp.log2(forget), jnp.minimum(f, 0.0) * LOG2_E)
            sig_neg = jnp.where(pos, ef, 1.0) * inv
            return q * jax.nn.sigmoid(q), one_m_lb[:, ls] * sig_neg, lf2

        def values(c, u):
            rs = slice(c * LIN_CHUNK, (c + 1) * LIN_CHUNK)
            return [proj_ref[rs, 2 * d_model + h * LANES:2 * d_model + (h + 1) * LANES].astype(BF16)
                    for h in range(u * heads_per_unit, (u + 1) * heads_per_unit)]

        def finish(c, u, outs):
            rs = slice(c * LIN_CHUNK, (c + 1) * LIN_CHUNK)
            ng = ng_ref[...]
            for n, o in enumerate(outs):
                h = u * heads_per_unit + n
                ms = jnp.mean(o * o, axis=-1, keepdims=True)
                g = proj_ref[rs, 3 * d_model + h * LANES:3 * d_model + (h + 1) * LANES]
                y = o * lax.rsqrt(ms + RMS_EPS) * ng * jax.nn.sigmoid(g)
                o_ref[rs, h * LANES:(h + 1) * LANES] = y.astype(o_ref.dtype)

        _recurrent_mixer(rows // LIN_CHUNK, d_model // UNIT, 1, gates, values, finish, st_ref,
                         oi_ref, (eq_ref, ek_ref, eb_ref), lambda c: None, slot, flush)

    n_slots = (2 * (rows // LIN_CHUNK) + 1) * (d_model // UNIT)
    _pipelined_mixer_step(x_ref, win_ref, buf_a, buf_b, st_ref, mix, n_slots)


def _hgrn_mixer(x2d, consts, layer_j):
    d_model = x2d.shape[1]
    kern = functools.partial(_hgrn_kernel, layer_j=layer_j, d_model=d_model)
    return _mixer_call(kern, "hgrn_mixer", x2d, consts, d_model // LANES, d_model // LANES)


def _even_kernel(x_ref, win_ref, ws_ref, bs_ref, lng_ref, lnb_ref, wgk_ref, bgk_ref, ng_ref,
                 o_ref, buf_a, buf_b, st_ref, oi_ref, eq_ref, ek_ref, eb_ref, *, a_width, key_width,
                 b_width):
    rows = x_ref.shape[0]
    o_q = 2 * a_width
    o_k = o_q + key_width
    o_v = o_k + key_width
    o_g = o_v + b_width
    o_gk = o_g + b_width
    group_dim = a_width // A_GROUPS
    n_sgu = rows // SGU_CHUNK
    n_chunks = rows // LIN_CHUNK
    n_heads = b_width // LANES
    head_k = key_width // n_heads
    heads_per_group = LANES // head_k
    scale = head_k ** -0.5
    assert key_width == UNIT and n_chunks >= A_GROUPS

    def mix(proj_ref, slot, flush):
        def sgu_group(gi):
            ls = slice(gi * group_dim, (gi + 1) * group_dim)
            v = jax.nn.gelu(proj_ref[:, a_width + gi * group_dim:a_width + (gi + 1) * group_dim])
            vn = _layer_norm(v, lng_ref[:, ls], lnb_ref[:, ls]).astype(BF16)
            rhs = jnp.concatenate(
                [vn[n * SGU_CHUNK:(n + 1) * SGU_CHUNK, :] for n in range(n_sgu)], axis=1)
            w = jnp.where(_causal_mask(SGU_CHUNK), ws_ref[gi], 0.0).astype(BF16)
            mixed = _dot(w, rhs)
            bias = bs_ref[:, ls]
            for n in range(n_sgu):
                rs = slice(n * SGU_CHUNK, (n + 1) * SGU_CHUNK)
                u = jax.nn.gelu(proj_ref[rs, ls])
                y = u * (mixed[:, n * group_dim:(n + 1) * group_dim] + bias)
                o_ref[rs, ls] = y.astype(o_ref.dtype)

        def extra(c):
            if c < A_GROUPS:
                sgu_group(c)
                slot()

        def gates(c, u):
            rs = slice(c * LIN_CHUNK, (c + 1) * LIN_CHUNK)
            gk_low = proj_ref[rs, o_gk:o_gk + LANES].astype(BF16)
            z = _dot(gk_low, wgk_ref[...]) + bgk_ref[...]
            lf2 = (jnp.minimum(z, 0.0) * LOG2_E - jnp.log2(1.0 + jnp.exp(-jnp.abs(z)))) * (
                1.0 / GLA_GATE_NORMALIZER)
            q = proj_ref[rs, o_q:o_q + key_width] * scale
            k = proj_ref[rs, o_k:o_k + key_width]
            return q, k, lf2

        def values(c, u):
            rs = slice(c * LIN_CHUNK, (c + 1) * LIN_CHUNK)
            return [proj_ref[rs, o_v + h * LANES:o_v + (h + 1) * LANES].astype(BF16)
                    for h in range(n_heads)]

        def finish(c, u, outs):
            rs = slice(c * LIN_CHUNK, (c + 1) * LIN_CHUNK)
            ng = ng_ref[...]
            for h, o in enumerate(outs):
                ms = jnp.mean(o * o, axis=-1, keepdims=True)
                g = proj_ref[rs, o_g + h * LANES:o_g + (h + 1) * LANES]
                y = o * lax.rsqrt(ms + RMS_EPS) * ng * (g * jax.nn.sigmoid(g))
                o_ref[rs, a_width + h * LANES:a_width + (h + 1) * LANES] = y.astype(o_ref.dtype)

        _recurrent_mixer(n_chunks, 1, heads_per_group, gates, values, finish, st_ref, oi_ref,
                         (eq_ref, ek_ref, eb_ref), extra, slot, flush)

    _pipelined_mixer_step(x_ref, win_ref, buf_a, buf_b, st_ref, mix, 2 * n_chunks + A_GROUPS)


def _even_mixer(x2d, consts, a_width, key_width, b_width):
    kern = functools.partial(_even_kernel, a_width=a_width, key_width=key_width, b_width=b_width)
    return _mixer_call(kern, "even_mixer", x2d, consts, key_width // LANES, b_width // LANES)


def _cast_specs(to_cast, n_steps):
    in_specs, out_specs, out_shapes = [], [], []
    for stacked, index, width in to_cast:
        rows, src_width = stacked.shape[1:]
        span = 1
        while rows % (n_steps // span) or (rows // (n_steps // span)) % BF16_ROWS:
            span *= 2
        block_rows = rows // (n_steps // span)
        in_specs.append(pl.BlockSpec((None, block_rows, src_width),
                                     lambda i, index=index, span=span: (index, i // span, 0)))
        out_specs.append(pl.BlockSpec((block_rows, width), lambda i, span=span: (i // span, 0)))
        out_shapes.append(jax.ShapeDtypeStruct((rows, width), BF16))
    return in_specs, out_specs, out_shapes


def _cast_rows(srcs, dsts):
    for src, dst in zip(srcs, dsts):
        width = src.shape[1]
        dst[:, :width] = src[...].astype(dst.dtype)
        if dst.shape[1] > width:
            dst[:, width:] = jnp.zeros((dst.shape[0], dst.shape[1] - width), dst.dtype)


def _cast_kernel(*refs):
    _cast_rows(refs[:len(refs) // 2], refs[len(refs) // 2:])


def _cast_weights(to_cast):
    in_specs, out_specs, out_shapes = _cast_specs(to_cast, CAST_STEPS)
    return pl.pallas_call(
        _cast_kernel,
        grid=(CAST_STEPS,),
        in_specs=in_specs,
        out_specs=out_specs,
        out_shape=out_shapes,
        compiler_params=pltpu.CompilerParams(
            dimension_semantics=("arbitrary",), vmem_limit_bytes=VMEM_LIMIT),
        name="cast_weights",
    )(*[stacked for stacked, _, _ in to_cast])


def _post_kernel(x_ref, y_ref, p_ref, wout_ref, lnmg_ref, lnmb_ref, wfi_ref, wfo_ref,
                 lnfg_ref, lnfb_ref, wpg_ref, wpp_ref, *rest, alpha):
    n_cast = len(rest) // 2
    cast_src, o_ref, cast_dst = rest[:n_cast], rest[n_cast], rest[n_cast + 1:]
    _cast_rows(cast_src, cast_dst)

    rows = x_ref.shape[0]
    d_ff = wfo_ref.shape[0]
    chunks = list(range(0, d_ff, FF_CHUNK))
    assert sum(POST_PARTS) == rows
    parts = [slice(sum(POST_PARTS[:n]), sum(POST_PARTS[:n + 1])) for n in range(len(POST_PARTS))]
    st = [dict() for _ in parts]

    def out_proj(n):
        st[n]["mixed"] = _dot(y_ref[parts[n], :], wout_ref[...])

    def norm_mix(n):
        x1 = _layer_norm(alpha * x_ref[parts[n], :] + st[n].pop("mixed"), lnmg_ref[...], lnmb_ref[...])
        st[n]["x1"], st[n]["x1b"] = x1, x1.astype(BF16)

    def ffn_in(n, c0):
        x1b = st[n]["x1b"]
        st[n]["h", c0] = (_dot(x1b, wfi_ref[:, c0:c0 + FF_CHUNK]),
                          _dot(x1b, wfi_ref[:, d_ff + c0:d_ff + c0 + FF_CHUNK]))

    def ffn_out(n, k):
        if k + 1 < len(chunks):
            ffn_in(n, chunks[k + 1])
        hg, hu = st[n].pop(("h", chunks[k]))
        act = (hg * jax.nn.sigmoid(hg) * hu).astype(BF16)
        part = _dot(act, wfo_ref[chunks[k]:chunks[k] + FF_CHUNK, :])
        st[n]["acc"] = part if k == 0 else st[n]["acc"] + part

    def norm_ffn(n):
        st[n]["x2"] = _layer_norm(alpha * st[n].pop("x1") + st[n].pop("acc"), lnfg_ref[...],
                                  lnfb_ref[...])

    def embed(n):
        x2 = st[n].pop("x2")
        gate = jax.nn.sigmoid(_dot(x2.astype(BF16), wpg_ref[...]))
        emb = _dot(p_ref[parts[n], :].astype(BF16), wpp_ref[...])
        o_ref[parts[n], :] = x2 + gate * emb

    stages = [out_proj, norm_mix, lambda n: ffn_in(n, chunks[0])]
    stages += [functools.partial(ffn_out, k=k) for k in range(len(chunks))]
    stages += [norm_ffn, embed]
    for t in range(len(stages) + POST_LAG * (len(parts) - 1)):
        for n in range(len(parts)):
            k = t - n * POST_LAG
            if 0 <= k < len(stages):
                stages[k](n)


def _post(x2d, y2d, p3d, layer, consts, alpha, to_cast):
    seq, d_model = x2d.shape
    n_steps = seq // POST_ROWS
    cast_in, cast_out, cast_shapes = _cast_specs(to_cast, n_steps)
    kern = functools.partial(_post_kernel, alpha=alpha)
    outs = pl.pallas_call(
        kern,
        grid=(n_steps,),
        in_specs=[
            pl.BlockSpec((POST_ROWS, d_model), lambda i: (i, 0)),
            pl.BlockSpec((POST_ROWS, d_model), lambda i: (i, 0)),
            pl.BlockSpec((None, POST_ROWS, p3d.shape[2]), lambda i: (layer, i, 0)),
        ] + [spec for _, spec in consts] + cast_in,
        out_specs=[pl.BlockSpec((POST_ROWS, d_model), lambda i: (i, 0))] + cast_out,
        out_shape=[jax.ShapeDtypeStruct((seq, d_model), F32)] + cast_shapes,
        compiler_params=pltpu.CompilerParams(
            dimension_semantics=("arbitrary",), vmem_limit_bytes=VMEM_LIMIT),
        name="post",
    )(x2d, y2d, p3d, *[arr for arr, _ in consts], *[stacked for stacked, _, _ in to_cast])
    return outs[0], outs[1:]


def kernel(x, p, even_w_in, even_w_s, even_b_s, even_sgu_ln_g, even_sgu_ln_b, even_w_gk_up, even_b_gk,
           even_gla_norm_g, even_w_out, odd_w_in, hgrn_lb_logits, odd_hgrn_norm_g, odd_w_out,
           ln_mix_g, ln_mix_b, w_ffn_in, w_ffn_out, ln_ffn_g, ln_ffn_b, w_ple_proj, w_ple_gate):
    batch, seq, d_model = x.shape
    depth = p.shape[0]
    assert batch == 1 and seq % POST_ROWS == 0 and seq % MIX_ROWS == 0
    alpha = (2.0 * depth) ** 0.25
    a_width = even_sgu_ln_g.shape[1]
    b_width = d_model - a_width
    key_width = even_w_gk_up.shape[2]
    assert w_ffn_out.shape[1] % FF_CHUNK == 0
    even_in_pad = -(-even_w_in.shape[2] // LANES) * LANES

    def big_weights(layer):
        j = layer // 2
        w_in, w_out = (even_w_in, even_w_out) if layer % 2 == 0 else (odd_w_in, odd_w_out)
        mats = [(w_in, j), (w_out, j), (w_ffn_in, layer), (w_ffn_out, layer), (w_ple_gate, layer),
                (w_ple_proj, layer)]
        return [(w, idx, -(-w.shape[2] // LANES) * LANES) for w, idx in mats]

    rows3 = lambda a: a.reshape(a.shape[0], 1, -1).astype(F32)
    even_w_gk_b = jnp.pad(even_w_gk_up, ((0, 0), (0, LANES - GK_RANK), (0, 0))).astype(BF16)
    even_bias = jnp.repeat(even_b_s.transpose(0, 2, 1), a_width // A_GROUPS, axis=2)
    even_small = (even_w_s, even_bias, rows3(even_sgu_ln_g), rows3(even_sgu_ln_b), even_w_gk_b,
                  rows3(even_b_gk), rows3(even_gla_norm_g))
    odd_norm_g = rows3(odd_hgrn_norm_g)
    lnm_g, lnm_b, lnf_g, lnf_b = rows3(ln_mix_g), rows3(ln_mix_b), rows3(ln_ffn_g), rows3(ln_ffn_b)

    big = _cast_weights(big_weights(0))
    assert big[0].shape[1] == even_in_pad
    xs = x.reshape(seq, d_model)
    p3 = p.reshape(depth, seq, -1)
    for layer in range(depth):
        j = layer // 2
        const = lambda a: (a, _const_spec(a.shape))
        w_in_b, w_out_b, wfi_b, wfo_b, wpg_b, wpp_b = big
        if layer % 2 == 0:
            y = _even_mixer(xs, [const(w_in_b)] + [_layer_operand(a, j) for a in even_small],
                            a_width, key_width, b_width)
        else:
            y = _hgrn_mixer(xs, [const(w_in_b), const(hgrn_lb_logits.astype(F32)),
                                 _layer_operand(odd_norm_g, j)], j)
        consts = [const(w_out_b), _layer_operand(lnm_g, layer), _layer_operand(lnm_b, layer),
                  const(wfi_b), const(wfo_b), _layer_operand(lnf_g, layer),
                  _layer_operand(lnf_b, layer), const(wpg_b), const(wpp_b)]
        xs, big = _post(xs, y, p3, layer, consts, alpha,
                        big_weights(layer + 1) if layer + 1 < depth else [])
    return xs.reshape(batch, seq, d_model)
```

```python
import functools

import jax
import jax.numpy as jnp
from jax import lax
from jax.experimental import pallas as pl
from jax.experimental.pallas import tpu as pltpu

F32 = jnp.float32
BF16 = jnp.bfloat16

LANES = 128
BF16_ROWS = 16
MXU_WIDTH = 256
LN_EPS = 1e-5
RMS_EPS = 1e-6
LOG2_E = 1.4426950408889634
GLA_GATE_NORMALIZER = 16.0
GK_RANK = 16
A_GROUPS = 4
SGU_CHUNK = 128
LIN_CHUNK = 64
MIX_ROWS = 256
POST_ROWS = 512
POST_PARTS = (256, 256)
POST_LAG = 1
FF_CHUNK = 256
CAST_STEPS = 8
PROJ_SLAB = MXU_WIDTH
UNIT = MXU_WIDTH
MAX_LOG2_SPAN = 100.0
VMEM_LIMIT = 56 * 1024 * 1024


def _dot(a, b):
    return jnp.dot(a, b, preferred_element_type=F32)


def _dot_nt(a, b):
    return lax.dot_general(a, b, (((1,), (1,)), ((), ())), preferred_element_type=F32)


def _dot_tn(a, b):
    return lax.dot_general(a, b, (((0,), (0,)), ((), ())), preferred_element_type=F32)


def _layer_norm(x, g, b):
    mu = jnp.mean(x, axis=-1, keepdims=True)
    xc = x - mu
    var = jnp.mean(xc * xc, axis=-1, keepdims=True)
    return xc * lax.rsqrt(var + LN_EPS) * g + b


def _const_spec(shape):
    nd = len(shape)
    return pl.BlockSpec(shape, lambda i: (0,) * nd, pipeline_mode=pl.Buffered(1))


def _layer_operand(stacked, layer):
    nd = stacked.ndim
    spec = pl.BlockSpec((None,) + stacked.shape[1:], lambda i: (layer,) + (0,) * (nd - 1),
                        pipeline_mode=pl.Buffered(1))
    return stacked, spec


def _causal_mask(c):
    rows = lax.broadcasted_iota(jnp.int32, (c, c), 0)
    cols = lax.broadcasted_iota(jnp.int32, (c, c), 1)
    return cols <= rows


def _cumsum_rows(x):
    tril_b = jnp.where(_causal_mask(x.shape[0]), 1.0, 0.0).astype(BF16)
    hi = x.astype(BF16)
    lo = (x - hi.astype(F32)).astype(BF16)
    return _dot(tril_b, hi) + _dot(tril_b, lo)


def _recurrence_prepare(q, k, lf2):
    c = q.shape[0]
    b = _cumsum_rows(lf2)
    b_mid = b[c // 2:c // 2 + 1, :]
    b_last = b[c - 1:c, :]
    span = jnp.maximum(b[0:1, :] - b_mid, b_mid - b_last)
    work = dict(
        qe=(q * jnp.exp2(b - b_mid)).astype(BF16),
        ke=(k * jnp.exp2(b_mid - b)).astype(BF16),
        qb=(q * jnp.exp2(b)).astype(BF16),
        kd=(k * jnp.exp2(b_last - b)).astype(BF16),
        dec=jnp.exp2(b_last))
    return work, span


def _only_head(a, h, heads_per_group):
    width = a.shape[1]
    head_k = LANES // heads_per_group
    if heads_per_group == 1:
        zeros = jnp.zeros((a.shape[0], LANES), a.dtype)
        return jnp.concatenate([a[:, g * LANES:(g + 1) * LANES] if g == h else zeros
                                for g in range(width // LANES)], axis=1)
    lane = lax.broadcasted_iota(jnp.int32, a.shape, 1)
    return jnp.where((lane >= h * head_k) & (lane < (h + 1) * head_k), a, jnp.zeros_like(a))


def _block_diagonal(blocks):
    n = len(blocks)
    zeros = jnp.zeros_like(blocks[0])
    return jnp.concatenate(
        [jnp.concatenate([blk if j == i else zeros for j in range(n)], axis=1)
         for i, blk in enumerate(blocks)], axis=0)


def _recurrence_scores(work, v_heads, heads_per_group):
    c = v_heads[0].shape[0]
    n_heads = len(v_heads)
    kblk = jnp.concatenate([_only_head(work["ke"], h, heads_per_group) for h in range(n_heads)],
                           axis=0)
    rows = lax.broadcasted_iota(jnp.int32, (c, n_heads * c), 0)
    cols = lax.broadcasted_iota(jnp.int32, (c, n_heads * c), 1)
    causal = (cols & (c - 1)) <= rows
    work["p"] = jnp.where(causal, _dot_nt(work["qe"], kblk), 0.0).astype(BF16)
    work["upd"] = []
    for gi in range(work["kd"].shape[1] // LANES):
        upd = None
        for s in range(heads_per_group):
            h = gi * heads_per_group + s
            kd_h = _only_head(work["kd"], h, heads_per_group)[:, gi * LANES:(gi + 1) * LANES]
            u = _dot_tn(v_heads[h], kd_h)
            upd = u if upd is None else upd + u
        work["upd"].append(upd)


def _recurrence_outputs(work, v_heads, st_ref, g0, heads_per_group):
    n_heads = len(v_heads)
    n_groups = n_heads // heads_per_group
    states = [st_ref[g0 + gi] for gi in range(n_groups)]
    st_unit = jnp.concatenate([st.astype(BF16) for st in states], axis=1)
    stblk = jnp.concatenate([_only_head(st_unit, h, heads_per_group) for h in range(n_heads)],
                            axis=0)
    o_carried = _dot_nt(work["qb"], stblk)
    o_intra = _dot(work["p"], _block_diagonal(v_heads))
    for gi, st in enumerate(states):
        st_ref[g0 + gi] = st * work["dec"][:, gi * LANES:(gi + 1) * LANES] + work["upd"][gi]
    return [(o_carried[:, h * LANES:(h + 1) * LANES], o_intra[:, h * LANES:(h + 1) * LANES])
            for h in range(n_heads)]


def _exact_scores(q_ref, k_ref, b_ref, heads_per_group):
    c, width = q_ref.shape
    head_k = LANES // heads_per_group
    q = q_ref[...]
    b = b_ref[...]
    col = lax.broadcasted_iota(jnp.int32, (c, c), 1)
    lane = lax.broadcasted_iota(jnp.int32, (c, LANES), 1)
    n_heads = (width // LANES) * heads_per_group

    def body(s, ps):
        k_s = k_ref[pl.ds(s, 1), :]
        b_s = b_ref[pl.ds(s, 1), :]
        w = q * k_s * jnp.exp2(jnp.minimum(b - b_s, 0.0))
        new = []
        for h in range(n_heads):
            gi, sub = divmod(h, heads_per_group)
            wg = w[:, gi * LANES:(gi + 1) * LANES]
            if heads_per_group > 1:
                wg = jnp.where((lane >= sub * head_k) & (lane < (sub + 1) * head_k), wg, 0.0)
            new.append(jnp.where(col == s, jnp.sum(wg, axis=-1, keepdims=True), ps[h]))
        return tuple(new)

    ps = lax.fori_loop(0, c, body, tuple(jnp.zeros((c, c), F32) for _ in range(n_heads)))
    causal = _causal_mask(c)
    return [jnp.where(causal, p, 0.0).astype(BF16) for p in ps]


def _run_recurrence(n_chunks, n_units, prepare, scores, outputs, finish, extra, slot):
    for u in range(n_units):
        prepare(0, u)
        if n_units > 1:
            slot()
    for c in range(n_chunks):
        for u in range(n_units):
            if c + 1 < n_chunks:
                prepare(c + 1, u)
            scores(c, u)
            slot()
        extra(c)
        for u in range(n_units):
            outputs(c, u)
            if u > 0:
                finish(c, u - 1)
            slot()
        finish(c, n_units - 1)


def _recurrent_mixer(n_chunks, n_units, heads_per_group, gates, values, finish, st_ref, oi_ref,
                     exact_refs, extra, slot, flush):
    heads_per_unit = (UNIT // LANES) * heads_per_group
    work, spans = {}, []

    def prepare(c, u):
        work[c, u], span = _recurrence_prepare(*gates(c, u))
        spans.append(span)

    def scores(c, u):
        _recurrence_scores(work[c, u], values(c, u), heads_per_group)

    def outputs(c, u):
        outs = []
        pairs = _recurrence_outputs(work[c, u], values(c, u), st_ref, u * (UNIT // LANES),
                                    heads_per_group)
        for n, (o_carried, o_intra) in enumerate(pairs):
            oi_ref[c, u * heads_per_unit + n] = o_carried
            outs.append(o_carried + o_intra)
        work[c, u] = outs

    _run_recurrence(n_chunks, n_units, prepare, scores, outputs,
                    lambda c, u: finish(c, u, work.pop((c, u))), extra, slot)
    flush()

    worst = functools.reduce(jnp.maximum, spans)

    @pl.when(jnp.max(worst) > MAX_LOG2_SPAN)
    def _():
        q_ref, k_ref, b_ref = exact_refs
        for c in range(n_chunks):
            for u in range(n_units):
                q, k, lf2 = gates(c, u)
                q_ref[...] = q
                k_ref[...] = k
                b_ref[...] = _cumsum_rows(lf2)
                ps = _exact_scores(q_ref, k_ref, b_ref, heads_per_group)
                vs = values(c, u)
                finish(c, u, [oi_ref[c, u * heads_per_unit + n] + _dot(ps[n], vs[n])
                              for n in range(heads_per_unit)])


def _pipelined_mixer_step(x_ref, win_ref, buf_a, buf_b, st_ref, mix, n_slots):
    i = pl.program_id(0)

    @pl.when(i == 0)
    def _():
        st_ref[...] = jnp.zeros_like(st_ref)
        buf_b[...] = jnp.zeros_like(buf_b)

    def step(w_buf, r_buf):
        xb = x_ref[...].astype(BF16)
        width = w_buf.shape[1]
        pending = list(range(0, width, PROJ_SLAB))
        n_slabs = len(pending)
        seen = [0]

        def project_slab():
            lo = pending.pop(0)
            hi = min(lo + PROJ_SLAB, width)
            w_buf[:, lo:hi] = _dot(xb, win_ref[:, lo:hi])

        def slot():
            seen[0] += 1
            while pending and (n_slabs - len(pending)) * n_slots < seen[0] * n_slabs:
                project_slab()

        def flush():
            while pending:
                project_slab()

        mix(r_buf, slot, flush)

    @pl.when(lax.rem(i, 2) == 0)
    def _():
        step(buf_a, buf_b)

    @pl.when(lax.rem(i, 2) == 1)
    def _():
        step(buf_b, buf_a)


def _mixer_call(kern, name, x2d, consts, n_state, n_heads):
    seq, d_model = x2d.shape
    proj_width = consts[0][0].shape[-1]
    n_blocks = seq // MIX_ROWS
    return pl.pallas_call(
        kern,
        grid=(n_blocks + 1,),
        in_specs=[pl.BlockSpec((MIX_ROWS, d_model), lambda i: (jnp.minimum(i, n_blocks - 1), 0))]
        + [spec for _, spec in consts],
        out_specs=pl.BlockSpec((MIX_ROWS, d_model), lambda i: (jnp.maximum(i - 1, 0), 0)),
        out_shape=jax.ShapeDtypeStruct((seq, d_model), BF16),
        scratch_shapes=[
            pltpu.VMEM((MIX_ROWS, proj_width), F32),
            pltpu.VMEM((MIX_ROWS, proj_width), F32),
            pltpu.VMEM((n_state, LANES, LANES), F32),
            pltpu.VMEM((MIX_ROWS // LIN_CHUNK, n_heads, LIN_CHUNK, LANES), F32),
            pltpu.VMEM((LIN_CHUNK, UNIT), F32),
            pltpu.VMEM((LIN_CHUNK, UNIT), F32),
            pltpu.VMEM((LIN_CHUNK, UNIT), F32),
        ],
        compiler_params=pltpu.CompilerParams(
            dimension_semantics=("arbitrary",), vmem_limit_bytes=VMEM_LIMIT),
        name=name,
    )(x2d, *[arr for arr, _ in consts])


def _hgrn_kernel(x_ref, win_ref, lbl_ref, ng_ref, o_ref, buf_a, buf_b, st_ref, oi_ref,
                 eq_ref, ek_ref, eb_ref, *, layer_j, d_model):
    logits = lbl_ref[...]
    e = jnp.exp(logits - jnp.max(logits, axis=0, keepdims=True))
    probs = e / jnp.sum(e, axis=0, keepdims=True)
    lb = jnp.zeros((1, d_model), F32)
    for r in range(1, layer_j + 1):
        lb = lb + probs[r:r + 1, :]
    one_m_lb = 1.0 - lb
    rows = x_ref.shape[0]
    heads_per_unit = UNIT // LANES

    def mix(proj_ref, slot, flush):
        def gates(c, u):
            rs = slice(c * LIN_CHUNK, (c + 1) * LIN_CHUNK)
            ls = slice(u * UNIT, (u + 1) * UNIT)
            q = proj_ref[rs, u * UNIT:(u + 1) * UNIT]
            f = proj_ref[rs, d_model + u * UNIT:d_model + (u + 1) * UNIT]
            ef = jnp.exp(-jnp.abs(f))
            inv = 1.0 / (1.0 + ef)
            pos = f >= 0.0
            forget = jnp.where(pos, 1.0 + lb[:, ls] * ef, lb[:, ls] + ef) * inv
            lf2 = jnp.where(forget > 0.0, jnp.log2(forget), jnp.minimum(f, 0.0) * LOG2_E)
            sig_neg = jnp.where(pos, ef, 1.0) * inv
            return q * jax.nn.sigmoid(q), one_m_lb[:, ls] * sig_neg, lf2

        def values(c, u):
            rs = slice(c * LIN_CHUNK, (c + 1) * LIN_CHUNK)
            return [proj_ref[rs, 2 * d_model + h * LANES:2 * d_model + (h + 1) * LANES].astype(BF16)
                    for h in range(u * heads_per_unit, (u + 1) * heads_per_unit)]

        def finish(c, u, outs):
            rs = slice(c * LIN_CHUNK, (c + 1) * LIN_CHUNK)
            ng = ng_ref[...]
            for n, o in enumerate(outs):
                h = u * heads_per_unit + n
                ms = jnp.mean(o * o, axis=-1, keepdims=True)
                g = proj_ref[rs, 3 * d_model + h * LANES:3 * d_model + (h + 1) * LANES]
                y = o * lax.rsqrt(ms + RMS_EPS) * ng * jax.nn.sigmoid(g)
                o_ref[rs, h * LANES:(h + 1) * LANES] = y.astype(o_ref.dtype)

        _recurrent_mixer(rows // LIN_CHUNK, d_model // UNIT, 1, gates, values, finish, st_ref,
                         oi_ref, (eq_ref, ek_ref, eb_ref), lambda c: None, slot, flush)

    n_slots = (2 * (rows // LIN_CHUNK) + 1) * (d_model // UNIT)
    _pipelined_mixer_step(x_ref, win_ref, buf_a, buf_b, st_ref, mix, n_slots)


def _hgrn_mixer(x2d, consts, layer_j):
    d_model = x2d.shape[1]
    kern = functools.partial(_hgrn_kernel, layer_j=layer_j, d_model=d_model)
    return _mixer_call(kern, "hgrn_mixer", x2d, consts, d_model // LANES, d_model // LANES)


def _even_kernel(x_ref, win_ref, ws_ref, bs_ref, lng_ref, lnb_ref, wgk_ref, bgk_ref, ng_ref,
                 o_ref, buf_a, buf_b, st_ref, oi_ref, eq_ref, ek_ref, eb_ref, *, a_width, key_width,
                 b_width):
    rows = x_ref.shape[0]
    o_q = 2 * a_width
    o_k = o_q + key_width
    o_v = o_k + key_width
    o_g = o_v + b_width
    o_gk = o_g + b_width
    group_dim = a_width // A_GROUPS
    n_sgu = rows // SGU_CHUNK
    n_chunks = rows // LIN_CHUNK
    n_heads = b_width // LANES
    head_k = key_width // n_heads
    heads_per_group = LANES // head_k
    scale = head_k ** -0.5
    assert key_width == UNIT and n_chunks >= A_GROUPS

    def mix(proj_ref, slot, flush):
        def sgu_group(gi):
            ls = slice(gi * group_dim, (gi + 1) * group_dim)
            v = jax.nn.gelu(proj_ref[:, a_width + gi * group_dim:a_width + (gi + 1) * group_dim])
            vn = _layer_norm(v, lng_ref[:, ls], lnb_ref[:, ls]).astype(BF16)
            rhs = jnp.concatenate(
                [vn[n * SGU_CHUNK:(n + 1) * SGU_CHUNK, :] for n in range(n_sgu)], axis=1)
            w = jnp.where(_causal_mask(SGU_CHUNK), ws_ref[gi], 0.0).astype(BF16)
            mixed = _dot(w, rhs)
            bias = bs_ref[:, ls]
            for n in range(n_sgu):
                rs = slice(n * SGU_CHUNK, (n + 1) * SGU_CHUNK)
                u = jax.nn.gelu(proj_ref[rs, ls])
                y = u * (mixed[:, n * group_dim:(n + 1) * group_dim] + bias)
                o_ref[rs, ls] = y.astype(o_ref.dtype)

        def extra(c):
            if c < A_GROUPS:
                sgu_group(c)
                slot()

        def gates(c, u):
            rs = slice(c * LIN_CHUNK, (c + 1) * LIN_CHUNK)
            gk_low = proj_ref[rs, o_gk:o_gk + LANES].astype(BF16)
            z = _dot(gk_low, wgk_ref[...]) + bgk_ref[...]
            lf2 = (jnp.minimum(z, 0.0) * LOG2_E - jnp.log2(1.0 + jnp.exp(-jnp.abs(z)))) * (
                1.0 / GLA_GATE_NORMALIZER)
            q = proj_ref[rs, o_q:o_q + key_width] * scale
            k = proj_ref[rs, o_k:o_k + key_width]
            return q, k, lf2

        def values(c, u):
            rs = slice(c * LIN_CHUNK, (c + 1) * LIN_CHUNK)
            return [proj_ref[rs, o_v + h * LANES:o_v + (h + 1) * LANES].astype(BF16)
                    for h in range(n_heads)]

        def finish(c, u, outs):
            rs = slice(c * LIN_CHUNK, (c + 1) * LIN_CHUNK)
            ng = ng_ref[...]
            for h, o in enumerate(outs):
                ms = jnp.mean(o * o, axis=-1, keepdims=True)
                g = proj_ref[rs, o_g + h * LANES:o_g + (h + 1) * LANES]
                y = o * lax.rsqrt(ms + RMS_EPS) * ng * (g * jax.nn.sigmoid(g))
                o_ref[rs, a_width + h * LANES:a_width + (h + 1) * LANES] = y.astype(o_ref.dtype)

        _recurrent_mixer(n_chunks, 1, heads_per_group, gates, values, finish, st_ref, oi_ref,
                         (eq_ref, ek_ref, eb_ref), extra, slot, flush)

    _pipelined_mixer_step(x_ref, win_ref, buf_a, buf_b, st_ref, mix, 2 * n_chunks + A_GROUPS)


def _even_mixer(x2d, consts, a_width, key_width, b_width):
    kern = functools.partial(_even_kernel, a_width=a_width, key_width=key_width, b_width=b_width)
    return _mixer_call(kern, "even_mixer", x2d, consts, key_width // LANES, b_width // LANES)


def _cast_specs(to_cast, n_steps):
    in_specs, out_specs, out_shapes = [], [], []
    for stacked, index, width in to_cast:
        rows, src_width = stacked.shape[1:]
        span = 1
        while rows % (n_steps // span) or (rows // (n_steps // span)) % BF16_ROWS:
            span *= 2
        block_rows = rows // (n_steps // span)
        in_specs.append(pl.BlockSpec((None, block_rows, src_width),
                                     lambda i, index=index, span=span: (index, i // span, 0)))
        out_specs.append(pl.BlockSpec((block_rows, width), lambda i, span=span: (i // span, 0)))
        out_shapes.append(jax.ShapeDtypeStruct((rows, width), BF16))
    return in_specs, out_specs, out_shapes


def _cast_rows(srcs, dsts):
    for src, dst in zip(srcs, dsts):
        width = src.shape[1]
        dst[:, :width] = src[...].astype(dst.dtype)
        if dst.shape[1] > width:
            dst[:, width:] = jnp.zeros((dst.shape[0], dst.shape[1] - width), dst.dtype)


def _cast_kernel(*refs):
    _cast_rows(refs[:len(refs) // 2], refs[len(refs) // 2:])


def _cast_weights(to_cast):
    in_specs, out_specs, out_shapes = _cast_specs(to_cast, CAST_STEPS)
    return pl.pallas_call(
        _cast_kernel,
        grid=(CAST_STEPS,),
        in_specs=in_specs,
        out_specs=out_specs,
        out_shape=out_shapes,
        compiler_params=pltpu.CompilerParams(
            dimension_semantics=("arbitrary",), vmem_limit_bytes=VMEM_LIMIT),
        name="cast_weights",
    )(*[stacked for stacked, _, _ in to_cast])


def _post_kernel(x_ref, y_ref, p_ref, wout_ref, lnmg_ref, lnmb_ref, wfi_ref, wfo_ref,
                 lnfg_ref, lnfb_ref, wpg_ref, wpp_ref, *rest, alpha):
    n_cast = len(rest) // 2
    cast_src, o_ref, cast_dst = rest[:n_cast], rest[n_cast], rest[n_cast + 1:]
    _cast_rows(cast_src, cast_dst)

    rows = x_ref.shape[0]
    d_ff = wfo_ref.shape[0]
    chunks = list(range(0, d_ff, FF_CHUNK))
    assert sum(POST_PARTS) == rows
    parts = [slice(sum(POST_PARTS[:n]), sum(POST_PARTS[:n + 1])) for n in range(len(POST_PARTS))]
    st = [dict() for _ in parts]

    def out_proj(n):
        st[n]["mixed"] = _dot(y_ref[parts[n], :], wout_ref[...])

    def norm_mix(n):
        x1 = _layer_norm(alpha * x_ref[parts[n], :] + st[n].pop("mixed"), lnmg_ref[...], lnmb_ref[...])
        st[n]["x1"], st[n]["x1b"] = x1, x1.astype(BF16)

    def ffn_in(n, c0):
        x1b = st[n]["x1b"]
        st[n]["h", c0] = (_dot(x1b, wfi_ref[:, c0:c0 + FF_CHUNK]),
                          _dot(x1b, wfi_ref[:, d_ff + c0:d_ff + c0 + FF_CHUNK]))

    def ffn_out(n, k):
        if k + 1 < len(chunks):
            ffn_in(n, chunks[k + 1])
        hg, hu = st[n].pop(("h", chunks[k]))
        act = (hg * jax.nn.sigmoid(hg) * hu).astype(BF16)
        part = _dot(act, wfo_ref[chunks[k]:chunks[k] + FF_CHUNK, :])
        st[n]["acc"] = part if k == 0 else st[n]["acc"] + part

    def norm_ffn(n):
        st[n]["x2"] = _layer_norm(alpha * st[n].pop("x1") + st[n].pop("acc"), lnfg_ref[...],
                                  lnfb_ref[...])

    def embed(n):
        x2 = st[n].pop("x2")
        gate = jax.nn.sigmoid(_dot(x2.astype(BF16), wpg_ref[...]))
        emb = _dot(p_ref[parts[n], :].astype(BF16), wpp_ref[...])
        o_ref[parts[n], :] = x2 + gate * emb

    stages = [out_proj, norm_mix, lambda n: ffn_in(n, chunks[0])]
    stages += [functools.partial(ffn_out, k=k) for k in range(len(chunks))]
    stages += [norm_ffn, embed]
    for t in range(len(stages) + POST_LAG * (len(parts) - 1)):
        for n in range(len(parts)):
            k = t - n * POST_LAG
            if 0 <= k < len(stages):
                stages[k](n)


def _post(x2d, y2d, p3d, layer, consts, alpha, to_cast):
    seq, d_model = x2d.shape
    n_steps = seq // POST_ROWS
    cast_in, cast_out, cast_shapes = _cast_specs(to_cast, n_steps)
    kern = functools.partial(_post_kernel, alpha=alpha)
    outs = pl.pallas_call(
        kern,
        grid=(n_steps,),
        in_specs=[
            pl.BlockSpec((POST_ROWS, d_model), lambda i: (i, 0)),
            pl.BlockSpec((POST_ROWS, d_model), lambda i: (i, 0)),
            pl.BlockSpec((None, POST_ROWS, p3d.shape[2]), lambda i: (layer, i, 0)),
        ] + [spec for _, spec in consts] + cast_in,
        out_specs=[pl.BlockSpec((POST_ROWS, d_model), lambda i: (i, 0))] + cast_out,
        out_shape=[jax.ShapeDtypeStruct((seq, d_model), F32)] + cast_shapes,
        compiler_params=pltpu.CompilerParams(
            dimension_semantics=("arbitrary",), vmem_limit_bytes=VMEM_LIMIT),
        name="post",
    )(x2d, y2d, p3d, *[arr for arr, _ in consts], *[stacked for stacked, _, _ in to_cast])
    return outs[0], outs[1:]


def kernel(x, p, even_w_in, even_w_s, even_b_s, even_sgu_ln_g, even_sgu_ln_b, even_w_gk_up, even_b_gk,
           even_gla_norm_g, even_w_out, odd_w_in, hgrn_lb_logits, odd_hgrn_norm_g, odd_w_out,
           ln_mix_g, ln_mix_b, w_ffn_in, w_ffn_out, ln_ffn_g, ln_ffn_b, w_ple_proj, w_ple_gate):
    batch, seq, d_model = x.shape
    depth = p.shape[0]
    assert batch == 1 and seq % POST_ROWS == 0 and seq % MIX_ROWS == 0
    alpha = (2.0 * depth) ** 0.25
    a_width = even_sgu_ln_g.shape[1]
    b_width = d_model - a_width
    key_width = even_w_gk_up.shape[2]
    assert w_ffn_out.shape[1] % FF_CHUNK == 0
    even_in_pad = -(-even_w_in.shape[2] // LANES) * LANES

    def big_weights(layer):
        j = layer // 2
        w_in, w_out = (even_w_in, even_w_out) if layer % 2 == 0 else (odd_w_in, odd_w_out)
        mats = [(w_in, j), (w_out, j), (w_ffn_in, layer), (w_ffn_out, layer), (w_ple_gate, layer),
                (w_ple_proj, layer)]
        return [(w, idx, -(-w.shape[2] // LANES) * LANES) for w, idx in mats]

    rows3 = lambda a: a.reshape(a.shape[0], 1, -1).astype(F32)
    even_w_gk_b = jnp.pad(even_w_gk_up, ((0, 0), (0, LANES - GK_RANK), (0, 0))).astype(BF16)
    even_bias = jnp.repeat(even_b_s.transpose(0, 2, 1), a_width // A_GROUPS, axis=2)
    even_small = (even_w_s, even_bias, rows3(even_sgu_ln_g), rows3(even_sgu_ln_b), even_w_gk_b,
                  rows3(even_b_gk), rows3(even_gla_norm_g))
    odd_norm_g = rows3(odd_hgrn_norm_g)
    lnm_g, lnm_b, lnf_g, lnf_b = rows3(ln_mix_g), rows3(ln_mix_b), rows3(ln_ffn_g), rows3(ln_ffn_b)

    big = _cast_weights(big_weights(0))
    assert big[0].shape[1] == even_in_pad
    xs = x.reshape(seq, d_model)
    p3 = p.reshape(depth, seq, -1)
    for layer in range(depth):
        j = layer // 2
        const = lambda a: (a, _const_spec(a.shape))
        w_in_b, w_out_b, wfi_b, wfo_b, wpg_b, wpp_b = big
        if layer % 2 == 0:
            y = _even_mixer(xs, [const(w_in_b)] + [_layer_operand(a, j) for a in even_small],
                            a_width, key_width, b_width)
        else:
            y = _hgrn_mixer(xs, [const(w_in_b), const(hgrn_lb_logits.astype(F32)),
                                 _layer_operand(odd_norm_g, j)], j)
        consts = [const(w_out_b), _layer_operand(lnm_g, layer), _layer_operand(lnm_b, layer),
                  const(wfi_b), const(wfo_b), _layer_operand(lnf_g, layer),
                  _layer_operand(lnf_b, layer), const(wpg_b), const(wpp_b)]
        xs, big = _post(xs, y, p3, layer, consts, alpha,
                        big_weights(layer + 1) if layer + 1 < depth else [])
    return xs.reshape(batch, seq, d_model)
```

```python
import functools

import jax
import jax.numpy as jnp
from jax import lax
from jax.experimental import pallas as pl
from jax.experimental.pallas import tpu as pltpu

F32 = jnp.float32
BF16 = jnp.bfloat16

LANES = 128
BF16_ROWS = 16
MXU_WIDTH = 256
LN_EPS = 1e-5
RMS_EPS = 1e-6
LOG2_E = 1.4426950408889634
GLA_GATE_NORMALIZER = 16.0
GK_RANK = 16
A_GROUPS = 4
SGU_CHUNK = 128
LIN_CHUNK = 64
MIX_ROWS = 256
POST_ROWS = 512
POST_PARTS = (256, 256)
POST_LAG = 1
FF_CHUNK = 256
CAST_STEPS = 8
PROJ_SLAB = MXU_WIDTH
UNIT = MXU_WIDTH
MAX_LOG2_SPAN = 100.0
MAX_SCORE = 3.0e38
VMEM_LIMIT = 56 * 1024 * 1024


def _dot(a, b):
    return jnp.dot(a, b, preferred_element_type=F32)


def _dot_nt(a, b):
    return lax.dot_general(a, b, (((1,), (1,)), ((), ())), preferred_element_type=F32)


def _dot_tn(a, b):
    return lax.dot_general(a, b, (((0,), (0,)), ((), ())), preferred_element_type=F32)


def _layer_norm(x, g, b):
    mu = jnp.mean(x, axis=-1, keepdims=True)
    xc = x - mu
    var = jnp.mean(xc * xc, axis=-1, keepdims=True)
    return xc * lax.rsqrt(var + LN_EPS) * g + b


def _const_spec(shape):
    nd = len(shape)
    return pl.BlockSpec(shape, lambda i: (0,) * nd, pipeline_mode=pl.Buffered(1))


def _layer_operand(stacked, layer):
    nd = stacked.ndim
    spec = pl.BlockSpec((None,) + stacked.shape[1:], lambda i: (layer,) + (0,) * (nd - 1),
                        pipeline_mode=pl.Buffered(1))
    return stacked, spec


def _causal_mask(c):
    rows = lax.broadcasted_iota(jnp.int32, (c, c), 0)
    cols = lax.broadcasted_iota(jnp.int32, (c, c), 1)
    return cols <= rows


def _cumsum_rows(x):
    tril_b = jnp.where(_causal_mask(x.shape[0]), 1.0, 0.0).astype(BF16)
    hi = x.astype(BF16)
    lo = (x - hi.astype(F32)).astype(BF16)
    return _dot(tril_b, hi) + _dot(tril_b, lo)


def _recurrence_prepare(q, k, lf2):
    c = q.shape[0]
    b = _cumsum_rows(lf2)
    b_mid = b[c // 2:c // 2 + 1, :]
    b_last = b[c - 1:c, :]
    span = jnp.maximum(b[0:1, :] - b_mid, b_mid - b_last)
    work = dict(
        qe=(q * jnp.exp2(b - b_mid)).astype(BF16),
        ke=(k * jnp.exp2(b_mid - b)).astype(BF16),
        qb=(q * jnp.exp2(b)).astype(BF16),
        kd=(k * jnp.exp2(b_last - b)).astype(BF16),
        dec=jnp.exp2(b_last))
    return work, span


def _only_head(a, h, heads_per_group):
    width = a.shape[1]
    head_k = LANES // heads_per_group
    if heads_per_group == 1:
        zeros = jnp.zeros((a.shape[0], LANES), a.dtype)
        return jnp.concatenate([a[:, g * LANES:(g + 1) * LANES] if g == h else zeros
                                for g in range(width // LANES)], axis=1)
    lane = lax.broadcasted_iota(jnp.int32, a.shape, 1)
    return jnp.where((lane >= h * head_k) & (lane < (h + 1) * head_k), a, jnp.zeros_like(a))


def _block_diagonal(blocks):
    n = len(blocks)
    zeros = jnp.zeros_like(blocks[0])
    return jnp.concatenate(
        [jnp.concatenate([blk if j == i else zeros for j in range(n)], axis=1)
         for i, blk in enumerate(blocks)], axis=0)


def _recurrence_scores(work, v_heads, heads_per_group):
    c = v_heads[0].shape[0]
    n_heads = len(v_heads)
    kblk = jnp.concatenate([_only_head(work["ke"], h, heads_per_group) for h in range(n_heads)],
                           axis=0)
    rows = lax.broadcasted_iota(jnp.int32, (c, n_heads * c), 0)
    cols = lax.broadcasted_iota(jnp.int32, (c, n_heads * c), 1)
    causal = (cols & (c - 1)) <= rows
    p = jnp.where(causal, _dot_nt(work["qe"], kblk), 0.0)
    work["bad"] = jnp.where(jnp.abs(p) < MAX_SCORE, 0.0, 1.0)
    work["p"] = p.astype(BF16)
    work["upd"] = []
    for gi in range(work["kd"].shape[1] // LANES):
        upd = None
        for s in range(heads_per_group):
            h = gi * heads_per_group + s
            kd_h = _only_head(work["kd"], h, heads_per_group)[:, gi * LANES:(gi + 1) * LANES]
            u = _dot_tn(v_heads[h], kd_h)
            upd = u if upd is None else upd + u
        work["upd"].append(upd)


def _recurrence_outputs(work, v_heads, st_ref, g0, heads_per_group):
    n_heads = len(v_heads)
    n_groups = n_heads // heads_per_group
    states = [st_ref[g0 + gi] for gi in range(n_groups)]
    st_unit = jnp.concatenate([st.astype(BF16) for st in states], axis=1)
    stblk = jnp.concatenate([_only_head(st_unit, h, heads_per_group) for h in range(n_heads)],
                            axis=0)
    o_carried = _dot_nt(work["qb"], stblk)
    o_intra = _dot(work["p"], _block_diagonal(v_heads))
    for gi, st in enumerate(states):
        st_ref[g0 + gi] = st * work["dec"][:, gi * LANES:(gi + 1) * LANES] + work["upd"][gi]
    return [(o_carried[:, h * LANES:(h + 1) * LANES], o_intra[:, h * LANES:(h + 1) * LANES])
            for h in range(n_heads)]


def _exact_scores(q_ref, k_ref, b_ref, heads_per_group):
    c, width = q_ref.shape
    head_k = LANES // heads_per_group
    q = q_ref[...]
    b = b_ref[...]
    col = lax.broadcasted_iota(jnp.int32, (c, c), 1)
    lane = lax.broadcasted_iota(jnp.int32, (c, LANES), 1)
    n_heads = (width // LANES) * heads_per_group

    def body(s, ps):
        k_s = k_ref[pl.ds(s, 1), :]
        b_s = b_ref[pl.ds(s, 1), :]
        w = q * k_s * jnp.exp2(jnp.minimum(b - b_s, 0.0))
        new = []
        for h in range(n_heads):
            gi, sub = divmod(h, heads_per_group)
            wg = w[:, gi * LANES:(gi + 1) * LANES]
            if heads_per_group > 1:
                wg = jnp.where((lane >= sub * head_k) & (lane < (sub + 1) * head_k), wg, 0.0)
            new.append(jnp.where(col == s, jnp.sum(wg, axis=-1, keepdims=True), ps[h]))
        return tuple(new)

    ps = lax.fori_loop(0, c, body, tuple(jnp.zeros((c, c), F32) for _ in range(n_heads)))
    causal = _causal_mask(c)
    return [jnp.where(causal, p, 0.0).astype(BF16) for p in ps]


def _run_recurrence(n_chunks, n_units, prepare, scores, outputs, finish, extra, slot):
    if n_units == 1:
        slot()
    for u in range(n_units):
        prepare(0, u)
        if n_units > 1:
            slot()
    for c in range(n_chunks):
        for u in range(n_units):
            if c + 1 < n_chunks:
                prepare(c + 1, u)
            scores(c, u)
            slot()
        extra(c)
        for u in range(n_units):
            outputs(c, u)
            if u > 0:
                finish(c, u - 1)
            slot()
        finish(c, n_units - 1)


def _recurrent_mixer(n_chunks, n_units, heads_per_group, gates, values, finish, st_ref, oi_ref,
                     exact_refs, extra, slot, flush):
    heads_per_unit = (UNIT // LANES) * heads_per_group
    work, spans, bads = {}, [], []

    def prepare(c, u):
        work[c, u], span = _recurrence_prepare(*gates(c, u))
        spans.append(span)

    def scores(c, u):
        _recurrence_scores(work[c, u], values(c, u), heads_per_group)
        bads.append(work[c, u].pop("bad"))

    def outputs(c, u):
        outs = []
        pairs = _recurrence_outputs(work[c, u], values(c, u), st_ref, u * (UNIT // LANES),
                                    heads_per_group)
        for n, (o_carried, o_intra) in enumerate(pairs):
            oi_ref[c, u * heads_per_unit + n] = o_carried
            outs.append(o_carried + o_intra)
        work[c, u] = outs

    _run_recurrence(n_chunks, n_units, prepare, scores, outputs,
                    lambda c, u: finish(c, u, work.pop((c, u))), extra, slot)
    flush()

    worst = functools.reduce(jnp.maximum, spans)
    bad = functools.reduce(jnp.maximum, bads)

    @pl.when((jnp.max(worst) > MAX_LOG2_SPAN) | (jnp.max(bad) > 0.0))
    def _():
        q_ref, k_ref, b_ref = exact_refs
        for c in range(n_chunks):
            for u in range(n_units):
                q, k, lf2 = gates(c, u)
                q_ref[...] = q
                k_ref[...] = k
                b_ref[...] = _cumsum_rows(lf2)
                ps = _exact_scores(q_ref, k_ref, b_ref, heads_per_group)
                vs = values(c, u)
                finish(c, u, [oi_ref[c, u * heads_per_unit + n] + _dot(ps[n], vs[n])
                              for n in range(heads_per_unit)])


def _pipelined_mixer_step(x_ref, win_ref, buf_a, buf_b, st_ref, mix, n_slots):
    i = pl.program_id(0)

    @pl.when(i == 0)
    def _():
        st_ref[...] = jnp.zeros_like(st_ref)
        buf_b[...] = jnp.zeros_like(buf_b)

    def step(w_buf, r_buf):
        xb = x_ref[...].astype(BF16)
        width = w_buf.shape[1]
        pending = list(range(0, width, PROJ_SLAB))
        n_slabs = len(pending)
        seen = [0]

        def project_slab():
            lo = pending.pop(0)
            hi = min(lo + PROJ_SLAB, width)
            w_buf[:, lo:hi] = _dot(xb, win_ref[:, lo:hi])

        def slot():
            seen[0] += 1
            while pending and (n_slabs - len(pending)) * n_slots < seen[0] * n_slabs:
                project_slab()

        def flush():
            while pending:
                project_slab()

        mix(r_buf, slot, flush)

    @pl.when(lax.rem(i, 2) == 0)
    def _():
        step(buf_a, buf_b)

    @pl.when(lax.rem(i, 2) == 1)
    def _():
        step(buf_b, buf_a)


def _mixer_call(kern, name, x2d, consts, n_state, n_heads):
    seq, d_model = x2d.shape
    proj_width = consts[0][0].shape[-1]
    n_blocks = seq // MIX_ROWS
    return pl.pallas_call(
        kern,
        grid=(n_blocks + 1,),
        in_specs=[pl.BlockSpec((MIX_ROWS, d_model), lambda i: (jnp.minimum(i, n_blocks - 1), 0))]
        + [spec for _, spec in consts],
        out_specs=pl.BlockSpec((MIX_ROWS, d_model), lambda i: (jnp.maximum(i - 1, 0), 0)),
        out_shape=jax.ShapeDtypeStruct((seq, d_model), BF16),
        scratch_shapes=[
            pltpu.VMEM((MIX_ROWS, proj_width), F32),
            pltpu.VMEM((MIX_ROWS, proj_width), F32),
            pltpu.VMEM((n_state, LANES, LANES), F32),
            pltpu.VMEM((MIX_ROWS // LIN_CHUNK, n_heads, LIN_CHUNK, LANES), F32),
            pltpu.VMEM((LIN_CHUNK, UNIT), F32),
            pltpu.VMEM((LIN_CHUNK, UNIT), F32),
            pltpu.VMEM((LIN_CHUNK, UNIT), F32),
        ],
        compiler_params=pltpu.CompilerParams(
            dimension_semantics=("arbitrary",), vmem_limit_bytes=VMEM_LIMIT),
        name=name,
    )(x2d, *[arr for arr, _ in consts])


def _hgrn_kernel(x_ref, win_ref, lbl_ref, ng_ref, o_ref, buf_a, buf_b, st_ref, oi_ref,
                 eq_ref, ek_ref, eb_ref, *, layer_j, d_model):
    logits = lbl_ref[...]
    e = jnp.exp(logits - jnp.max(logits, axis=0, keepdims=True))
    probs = e / jnp.sum(e, axis=0, keepdims=True)
    lb = jnp.zeros((1, d_model), F32)
    for r in range(1, layer_j + 1):
        lb = lb + probs[r:r + 1, :]
    one_m_lb = 1.0 - lb
    rows = x_ref.shape[0]
    heads_per_unit = UNIT // LANES

    def mix(proj_ref, slot, flush):
        def gates(c, u):
            rs = slice(c * LIN_CHUNK, (c + 1) * LIN_CHUNK)
            ls = slice(u * UNIT, (u + 1) * UNIT)
            q = proj_ref[rs, u * UNIT:(u + 1) * UNIT]
            f = proj_ref[rs, d_model + u * UNIT:d_model + (u + 1) * UNIT]
            ef = jnp.exp(-jnp.abs(f))
            inv = 1.0 / (1.0 + ef)
            pos = f >= 0.0
            forget = jnp.where(pos, 1.0 + lb[:, ls] * ef, lb[:, ls] + ef) * inv
            lf2 = jnp.where(forget > 0.0, jnp.log2(forget), jnp.minimum(f, 0.0) * LOG2_E)
            sig_neg = jnp.where(pos, ef, 1.0) * inv
            return q * jax.nn.sigmoid(q), one_m_lb[:, ls] * sig_neg, lf2

        def values(c, u):
            rs = slice(c * LIN_CHUNK, (c + 1) * LIN_CHUNK)
            return [proj_ref[rs, 2 * d_model + h * LANES:2 * d_model + (h + 1) * LANES].astype(BF16)
                    for h in range(u * heads_per_unit, (u + 1) * heads_per_unit)]

        def finish(c, u, outs):
            rs = slice(c * LIN_CHUNK, (c + 1) * LIN_CHUNK)
            ng = ng_ref[...]
            for n, o in enumerate(outs):
                h = u * heads_per_unit + n
                ms = jnp.mean(o * o, axis=-1, keepdims=True)
                g = proj_ref[rs, 3 * d_model + h * LANES:3 * d_model + (h + 1) * LANES]
                y = o * lax.rsqrt(ms + RMS_EPS) * ng * jax.nn.sigmoid(g)
                o_ref[rs, h * LANES:(h + 1) * LANES] = y.astype(o_ref.dtype)

        _recurrent_mixer(rows // LIN_CHUNK, d_model // UNIT, 1, gates, values, finish, st_ref,
                         oi_ref, (eq_ref, ek_ref, eb_ref), lambda c: None, slot, flush)

    n_slots = (2 * (rows // LIN_CHUNK) + 1) * (d_model // UNIT)
    _pipelined_mixer_step(x_ref, win_ref, buf_a, buf_b, st_ref, mix, n_slots)


def _hgrn_mixer(x2d, consts, layer_j):
    d_model = x2d.shape[1]
    kern = functools.partial(_hgrn_kernel, layer_j=layer_j, d_model=d_model)
    return _mixer_call(kern, "hgrn_mixer", x2d, consts, d_model // LANES, d_model // LANES)


def _even_kernel(x_ref, win_ref, ws_ref, bs_ref, lng_ref, lnb_ref, wgk_ref, bgk_ref, ng_ref,
                 o_ref, buf_a, buf_b, st_ref, oi_ref, eq_ref, ek_ref, eb_ref, *, a_width, key_width,
                 b_width):
    rows = x_ref.shape[0]
    o_q = 2 * a_width
    o_k = o_q + key_width
    o_v = o_k + key_width
    o_g = o_v + b_width
    o_gk = o_g + b_width
    group_dim = a_width // A_GROUPS
    n_sgu = rows // SGU_CHUNK
    n_chunks = rows // LIN_CHUNK
    n_heads = b_width // LANES
    head_k = key_width // n_heads
    heads_per_group = LANES // head_k
    scale = head_k ** -0.5
    assert key_width == UNIT and n_chunks >= A_GROUPS

    def mix(proj_ref, slot, flush):
        def sgu_group(gi):
            ls = slice(gi * group_dim, (gi + 1) * group_dim)
            v = jax.nn.gelu(proj_ref[:, a_width + gi * group_dim:a_width + (gi + 1) * group_dim])
            vn = _layer_norm(v, lng_ref[:, ls], lnb_ref[:, ls]).astype(BF16)
            rhs = jnp.concatenate(
                [vn[n * SGU_CHUNK:(n + 1) * SGU_CHUNK, :] for n in range(n_sgu)], axis=1)
            w = jnp.where(_causal_mask(SGU_CHUNK), ws_ref[gi], 0.0).astype(BF16)
            mixed = _dot(w, rhs)
            bias = bs_ref[:, ls]
            for n in range(n_sgu):
                rs = slice(n * SGU_CHUNK, (n + 1) * SGU_CHUNK)
                u = jax.nn.gelu(proj_ref[rs, ls])
                y = u * (mixed[:, n * group_dim:(n + 1) * group_dim] + bias)
                o_ref[rs, ls] = y.astype(o_ref.dtype)

        def extra(c):
            if c < A_GROUPS:
                sgu_group(c)
                slot()

        def gates(c, u):
            rs = slice(c * LIN_CHUNK, (c + 1) * LIN_CHUNK)
            gk_low = proj_ref[rs, o_gk:o_gk + LANES].astype(BF16)
            z = _dot(gk_low, wgk_ref[...]) + bgk_ref[...]
            lf2 = (jnp.minimum(z, 0.0) * LOG2_E - jnp.log2(1.0 + jnp.exp(-jnp.abs(z)))) * (
                1.0 / GLA_GATE_NORMALIZER)
            q = proj_ref[rs, o_q:o_q + key_width] * scale
            k = proj_ref[rs, o_k:o_k + key_width]
            return q, k, lf2

        def values(c, u):
            rs = slice(c * LIN_CHUNK, (c + 1) * LIN_CHUNK)
            return [proj_ref[rs, o_v + h * LANES:o_v + (h + 1) * LANES].astype(BF16)
                    for h in range(n_heads)]

        def finish(c, u, outs):
            rs = slice(c * LIN_CHUNK, (c + 1) * LIN_CHUNK)
            ng = ng_ref[...]
            for h, o in enumerate(outs):
                ms = jnp.mean(o * o, axis=-1, keepdims=True)
                g = proj_ref[rs, o_g + h * LANES:o_g + (h + 1) * LANES]
                y = o * lax.rsqrt(ms + RMS_EPS) * ng * (g * jax.nn.sigmoid(g))
                o_ref[rs, a_width + h * LANES:a_width + (h + 1) * LANES] = y.astype(o_ref.dtype)

        _recurrent_mixer(n_chunks, 1, heads_per_group, gates, values, finish, st_ref, oi_ref,
                         (eq_ref, ek_ref, eb_ref), extra, slot, flush)

    _pipelined_mixer_step(x_ref, win_ref, buf_a, buf_b, st_ref, mix, 2 * n_chunks + A_GROUPS + 1)


def _even_mixer(x2d, consts, a_width, key_width, b_width):
    kern = functools.partial(_even_kernel, a_width=a_width, key_width=key_width, b_width=b_width)
    return _mixer_call(kern, "even_mixer", x2d, consts, key_width // LANES, b_width // LANES)


def _cast_specs(to_cast, n_steps):
    in_specs, out_specs, out_shapes = [], [], []
    for stacked, index, width in to_cast:
        rows, src_width = stacked.shape[1:]
        span = 1
        while rows % (n_steps // span) or (rows // (n_steps // span)) % BF16_ROWS:
            span *= 2
        block_rows = rows // (n_steps // span)
        in_specs.append(pl.BlockSpec((None, block_rows, src_width),
                                     lambda i, index=index, span=span: (index, i // span, 0)))
        out_specs.append(pl.BlockSpec((block_rows, width), lambda i, span=span: (i // span, 0)))
        out_shapes.append(jax.ShapeDtypeStruct((rows, width), BF16))
    return in_specs, out_specs, out_shapes


def _cast_rows(srcs, dsts):
    for src, dst in zip(srcs, dsts):
        width = src.shape[1]
        dst[:, :width] = src[...].astype(dst.dtype)
        if dst.shape[1] > width:
            dst[:, width:] = jnp.zeros((dst.shape[0], dst.shape[1] - width), dst.dtype)


def _cast_kernel(*refs):
    _cast_rows(refs[:len(refs) // 2], refs[len(refs) // 2:])


def _cast_weights(to_cast):
    in_specs, out_specs, out_shapes = _cast_specs(to_cast, CAST_STEPS)
    return pl.pallas_call(
        _cast_kernel,
        grid=(CAST_STEPS,),
        in_specs=in_specs,
        out_specs=out_specs,
        out_shape=out_shapes,
        compiler_params=pltpu.CompilerParams(
            dimension_semantics=("arbitrary",), vmem_limit_bytes=VMEM_LIMIT),
        name="cast_weights",
    )(*[stacked for stacked, _, _ in to_cast])


def _post_kernel(x_ref, y_ref, p_ref, wout_ref, lnmg_ref, lnmb_ref, wfi_ref, wfo_ref,
                 lnfg_ref, lnfb_ref, wpg_ref, wpp_ref, *rest, alpha):
    n_cast = len(rest) // 2
    cast_src, o_ref, cast_dst = rest[:n_cast], rest[n_cast], rest[n_cast + 1:]
    _cast_rows(cast_src, cast_dst)

    rows = x_ref.shape[0]
    d_ff = wfo_ref.shape[0]
    chunks = list(range(0, d_ff, FF_CHUNK))
    assert sum(POST_PARTS) == rows
    parts = [slice(sum(POST_PARTS[:n]), sum(POST_PARTS[:n + 1])) for n in range(len(POST_PARTS))]
    st = [dict() for _ in parts]

    def out_proj(n):
        st[n]["mixed"] = _dot(y_ref[parts[n], :], wout_ref[...])

    def norm_mix(n):
        x1 = _layer_norm(alpha * x_ref[parts[n], :] + st[n].pop("mixed"), lnmg_ref[...], lnmb_ref[...])
        st[n]["x1"], st[n]["x1b"] = x1, x1.astype(BF16)

    def ffn_in(n, c0):
        x1b = st[n]["x1b"]
        st[n]["h", c0] = (_dot(x1b, wfi_ref[:, c0:c0 + FF_CHUNK]),
                          _dot(x1b, wfi_ref[:, d_ff + c0:d_ff + c0 + FF_CHUNK]))

    def ffn_out(n, k):
        if k + 1 < len(chunks):
            ffn_in(n, chunks[k + 1])
        hg, hu = st[n].pop(("h", chunks[k]))
        act = (hg * jax.nn.sigmoid(hg) * hu).astype(BF16)
        part = _dot(act, wfo_ref[chunks[k]:chunks[k] + FF_CHUNK, :])
        st[n]["acc"] = part if k == 0 else st[n]["acc"] + part

    def norm_ffn(n):
        st[n]["x2"] = _layer_norm(alpha * st[n].pop("x1") + st[n].pop("acc"), lnfg_ref[...],
                                  lnfb_ref[...])

    def embed(n):
        x2 = st[n].pop("x2")
        gate = jax.nn.sigmoid(_dot(x2.astype(BF16), wpg_ref[...]))
        emb = _dot(p_ref[parts[n], :].astype(BF16), wpp_ref[...])
        o_ref[parts[n], :] = x2 + gate * emb

    stages = [out_proj, norm_mix, lambda n: ffn_in(n, chunks[0])]
    stages += [functools.partial(ffn_out, k=k) for k in range(len(chunks))]
    stages += [norm_ffn, embed]
    for t in range(len(stages) + POST_LAG * (len(parts) - 1)):
        for n in range(len(parts)):
            k = t - n * POST_LAG
            if 0 <= k < len(stages):
                stages[k](n)


def _post(x2d, y2d, p3d, layer, consts, alpha, to_cast):
    seq, d_model = x2d.shape
    n_steps = seq // POST_ROWS
    cast_in, cast_out, cast_shapes = _cast_specs(to_cast, n_steps)
    kern = functools.partial(_post_kernel, alpha=alpha)
    outs = pl.pallas_call(
        kern,
        grid=(n_steps,),
        in_specs=[
            pl.BlockSpec((POST_ROWS, d_model), lambda i: (i, 0)),
            pl.BlockSpec((POST_ROWS, d_model), lambda i: (i, 0)),
            pl.BlockSpec((None, POST_ROWS, p3d.shape[2]), lambda i: (layer, i, 0)),
        ] + [spec for _, spec in consts] + cast_in,
        out_specs=[pl.BlockSpec((POST_ROWS, d_model), lambda i: (i, 0))] + cast_out,
        out_shape=[jax.ShapeDtypeStruct((seq, d_model), F32)] + cast_shapes,
        compiler_params=pltpu.CompilerParams(
            dimension_semantics=("arbitrary",), vmem_limit_bytes=VMEM_LIMIT),
        name="post",
    )(x2d, y2d, p3d, *[arr for arr, _ in consts], *[stacked for stacked, _, _ in to_cast])
    return outs[0], outs[1:]


def kernel(x, p, even_w_in, even_w_s, even_b_s, even_sgu_ln_g, even_sgu_ln_b, even_w_gk_up, even_b_gk,
           even_gla_norm_g, even_w_out, odd_w_in, hgrn_lb_logits, odd_hgrn_norm_g, odd_w_out,
           ln_mix_g, ln_mix_b, w_ffn_in, w_ffn_out, ln_ffn_g, ln_ffn_b, w_ple_proj, w_ple_gate):
    batch, seq, d_model = x.shape
    depth = p.shape[0]
    assert batch == 1 and seq % POST_ROWS == 0 and seq % MIX_ROWS == 0
    alpha = (2.0 * depth) ** 0.25
    a_width = even_sgu_ln_g.shape[1]
    b_width = d_model - a_width
    key_width = even_w_gk_up.shape[2]
    assert w_ffn_out.shape[1] % FF_CHUNK == 0
    even_in_pad = -(-even_w_in.shape[2] // LANES) * LANES

    def big_weights(layer):
        j = layer // 2
        w_in, w_out = (even_w_in, even_w_out) if layer % 2 == 0 else (odd_w_in, odd_w_out)
        mats = [(w_in, j), (w_out, j), (w_ffn_in, layer), (w_ffn_out, layer), (w_ple_gate, layer),
                (w_ple_proj, layer)]
        return [(w, idx, -(-w.shape[2] // LANES) * LANES) for w, idx in mats]

    rows3 = lambda a: a.reshape(a.shape[0], 1, -1).astype(F32)
    even_w_gk_b = jnp.pad(even_w_gk_up, ((0, 0), (0, LANES - GK_RANK), (0, 0))).astype(BF16)
    even_bias = jnp.repeat(even_b_s.transpose(0, 2, 1), a_width // A_GROUPS, axis=2)
    even_small = (even_w_s, even_bias, rows3(even_sgu_ln_g), rows3(even_sgu_ln_b), even_w_gk_b,
                  rows3(even_b_gk), rows3(even_gla_norm_g))
    odd_norm_g = rows3(odd_hgrn_norm_g)
    lnm_g, lnm_b, lnf_g, lnf_b = rows3(ln_mix_g), rows3(ln_mix_b), rows3(ln_ffn_g), rows3(ln_ffn_b)

    big = _cast_weights(big_weights(0))
    assert big[0].shape[1] == even_in_pad
    xs = x.reshape(seq, d_model)
    p3 = p.reshape(depth, seq, -1)
    for layer in range(depth):
        j = layer // 2
        const = lambda a: (a, _const_spec(a.shape))
        w_in_b, w_out_b, wfi_b, wfo_b, wpg_b, wpp_b = big
        if layer % 2 == 0:
            y = _even_mixer(xs, [const(w_in_b)] + [_layer_operand(a, j) for a in even_small],
                            a_width, key_width, b_width)
        else:
            y = _hgrn_mixer(xs, [const(w_in_b), const(hgrn_lb_logits.astype(F32)),
                                 _layer_operand(odd_norm_g, j)], j)
        consts = [const(w_out_b), _layer_operand(lnm_g, layer), _layer_operand(lnm_b, layer),
                  const(wfi_b), const(wfo_b), _layer_operand(lnf_g, layer),
                  _layer_operand(lnf_b, layer), const(wpg_b), const(wpp_b)]
        xs, big = _post(xs, y, p3, layer, consts, alpha,
                        big_weights(layer + 1) if layer + 1 < depth else [])
    return xs.reshape(batch, seq, d_model)
```

```python
import functools

import jax
import jax.numpy as jnp
from jax import lax
from jax.experimental import pallas as pl
from jax.experimental.pallas import tpu as pltpu

F32 = jnp.float32
BF16 = jnp.bfloat16

LANES = 128
BF16_ROWS = 16
MXU_WIDTH = 256
LN_EPS = 1e-5
RMS_EPS = 1e-6
LOG2_E = 1.4426950408889634
GLA_GATE_NORMALIZER = 16.0
GK_RANK = 16
A_GROUPS = 4
SGU_CHUNK = 128
LIN_CHUNK = 64
MIX_ROWS = 256
POST_ROWS = 512
POST_PARTS = (256, 256)
POST_LAG = 1
FF_CHUNK = 256
CAST_STEPS = 8
PROJ_SLAB = MXU_WIDTH
UNIT = MXU_WIDTH
MAX_LOG2_SPAN = 100.0
MAX_SCORE = 3.0e38
VMEM_LIMIT = 56 * 1024 * 1024


def _dot(a, b):
    return jnp.dot(a, b, preferred_element_type=F32)


def _dot_nt(a, b):
    return lax.dot_general(a, b, (((1,), (1,)), ((), ())), preferred_element_type=F32)


def _dot_tn(a, b):
    return lax.dot_general(a, b, (((0,), (0,)), ((), ())), preferred_element_type=F32)


def _layer_norm(x, g, b):
    mu = jnp.mean(x, axis=-1, keepdims=True)
    xc = x - mu
    var = jnp.mean(xc * xc, axis=-1, keepdims=True)
    return xc * lax.rsqrt(var + LN_EPS) * g + b


def _const_spec(shape):
    nd = len(shape)
    return pl.BlockSpec(shape, lambda i: (0,) * nd, pipeline_mode=pl.Buffered(1))


def _layer_operand(stacked, layer):
    nd = stacked.ndim
    spec = pl.BlockSpec((None,) + stacked.shape[1:], lambda i: (layer,) + (0,) * (nd - 1),
                        pipeline_mode=pl.Buffered(1))
    return stacked, spec


def _causal_mask(c):
    rows = lax.broadcasted_iota(jnp.int32, (c, c), 0)
    cols = lax.broadcasted_iota(jnp.int32, (c, c), 1)
    return cols <= rows


def _cumsum_rows(x):
    tril_b = jnp.where(_causal_mask(x.shape[0]), 1.0, 0.0).astype(BF16)
    hi = x.astype(BF16)
    lo = (x - hi.astype(F32)).astype(BF16)
    return _dot(tril_b, hi) + _dot(tril_b, lo)


def _recurrence_prepare(q, k, lf2):
    c = q.shape[0]
    b = _cumsum_rows(lf2)
    b_mid = b[c // 2:c // 2 + 1, :]
    b_last = b[c - 1:c, :]
    span = jnp.maximum(b[0:1, :] - b_mid, b_mid - b_last)
    work = dict(
        qe=(q * jnp.exp2(b - b_mid)).astype(BF16),
        ke=(k * jnp.exp2(b_mid - b)).astype(BF16),
        qb=(q * jnp.exp2(b)).astype(BF16),
        kd=(k * jnp.exp2(b_last - b)).astype(BF16),
        dec=jnp.exp2(b_last))
    return work, span


def _only_head(a, h, heads_per_group):
    width = a.shape[1]
    head_k = LANES // heads_per_group
    if heads_per_group == 1:
        zeros = jnp.zeros((a.shape[0], LANES), a.dtype)
        return jnp.concatenate([a[:, g * LANES:(g + 1) * LANES] if g == h else zeros
                                for g in range(width // LANES)], axis=1)
    lane = lax.broadcasted_iota(jnp.int32, a.shape, 1)
    return jnp.where((lane >= h * head_k) & (lane < (h + 1) * head_k), a, jnp.zeros_like(a))


def _block_diagonal(blocks):
    n = len(blocks)
    zeros = jnp.zeros_like(blocks[0])
    return jnp.concatenate(
        [jnp.concatenate([blk if j == i else zeros for j in range(n)], axis=1)
         for i, blk in enumerate(blocks)], axis=0)


def _recurrence_scores(work, v_heads, heads_per_group):
    c = v_heads[0].shape[0]
    n_heads = len(v_heads)
    kblk = jnp.concatenate([_only_head(work["ke"], h, heads_per_group) for h in range(n_heads)],
                           axis=0)
    rows = lax.broadcasted_iota(jnp.int32, (c, n_heads * c), 0)
    cols = lax.broadcasted_iota(jnp.int32, (c, n_heads * c), 1)
    causal = (cols & (c - 1)) <= rows
    p = jnp.where(causal, _dot_nt(work["qe"], kblk), 0.0)
    work["bad"] = jnp.where(jnp.abs(p) < MAX_SCORE, 0.0, 1.0)
    work["p"] = p.astype(BF16)
    work["upd"] = []
    for gi in range(work["kd"].shape[1] // LANES):
        upd = None
        for s in range(heads_per_group):
            h = gi * heads_per_group + s
            kd_h = _only_head(work["kd"], h, heads_per_group)[:, gi * LANES:(gi + 1) * LANES]
            u = _dot_tn(v_heads[h], kd_h)
            upd = u if upd is None else upd + u
        work["upd"].append(upd)


def _recurrence_outputs(work, v_heads, st_ref, g0, heads_per_group):
    n_heads = len(v_heads)
    n_groups = n_heads // heads_per_group
    states = [st_ref[g0 + gi] for gi in range(n_groups)]
    st_unit = jnp.concatenate([st.astype(BF16) for st in states], axis=1)
    stblk = jnp.concatenate([_only_head(st_unit, h, heads_per_group) for h in range(n_heads)],
                            axis=0)
    o_carried = _dot_nt(work["qb"], stblk)
    o_intra = _dot(work["p"], _block_diagonal(v_heads))
    for gi, st in enumerate(states):
        st_ref[g0 + gi] = st * work["dec"][:, gi * LANES:(gi + 1) * LANES] + work["upd"][gi]
    return [(o_carried[:, h * LANES:(h + 1) * LANES], o_intra[:, h * LANES:(h + 1) * LANES])
            for h in range(n_heads)]


def _exact_scores(q_ref, k_ref, b_ref, heads_per_group):
    c, width = q_ref.shape
    head_k = LANES // heads_per_group
    q = q_ref[...]
    b = b_ref[...]
    col = lax.broadcasted_iota(jnp.int32, (c, c), 1)
    lane = lax.broadcasted_iota(jnp.int32, (c, LANES), 1)
    n_heads = (width // LANES) * heads_per_group

    def body(s, ps):
        k_s = k_ref[pl.ds(s, 1), :]
        b_s = b_ref[pl.ds(s, 1), :]
        w = q * k_s * jnp.exp2(jnp.minimum(b - b_s, 0.0))
        new = []
        for h in range(n_heads):
            gi, sub = divmod(h, heads_per_group)
            wg = w[:, gi * LANES:(gi + 1) * LANES]
            if heads_per_group > 1:
                wg = jnp.where((lane >= sub * head_k) & (lane < (sub + 1) * head_k), wg, 0.0)
            new.append(jnp.where(col == s, jnp.sum(wg, axis=-1, keepdims=True), ps[h]))
        return tuple(new)

    ps = lax.fori_loop(0, c, body, tuple(jnp.zeros((c, c), F32) for _ in range(n_heads)))
    causal = _causal_mask(c)
    return [jnp.where(causal, p, 0.0).astype(BF16) for p in ps]


def _run_recurrence(n_chunks, n_units, prepare, scores, outputs, finish, extra, slot):
    if n_units == 1:
        slot()
    for u in range(n_units):
        prepare(0, u)
        if n_units > 1:
            slot()
    for c in range(n_chunks):
        for u in range(n_units):
            if c + 1 < n_chunks:
                prepare(c + 1, u)
            scores(c, u)
            slot()
        extra(c)
        for u in range(n_units):
            outputs(c, u)
            if u > 0:
                finish(c, u - 1)
            slot()
        finish(c, n_units - 1)


def _recurrent_mixer(n_chunks, n_units, heads_per_group, gates, values, finish, st_ref, oi_ref,
                     exact_refs, extra, slot, flush):
    heads_per_unit = (UNIT // LANES) * heads_per_group
    work, spans, bads = {}, [], []

    def prepare(c, u):
        work[c, u], span = _recurrence_prepare(*gates(c, u))
        spans.append(span)

    def scores(c, u):
        _recurrence_scores(work[c, u], values(c, u), heads_per_group)
        bads.append(work[c, u].pop("bad"))

    def outputs(c, u):
        outs = []
        pairs = _recurrence_outputs(work[c, u], values(c, u), st_ref, u * (UNIT // LANES),
                                    heads_per_group)
        for n, (o_carried, o_intra) in enumerate(pairs):
            oi_ref[c, u * heads_per_unit + n] = o_carried
            outs.append(o_carried + o_intra)
        work[c, u] = outs

    _run_recurrence(n_chunks, n_units, prepare, scores, outputs,
                    lambda c, u: finish(c, u, work.pop((c, u))), extra, slot)
    flush()

    worst = functools.reduce(jnp.maximum, spans)
    bad = functools.reduce(jnp.maximum, bads)

    @pl.when((jnp.max(worst) > MAX_LOG2_SPAN) | (jnp.max(bad) > 0.0))
    def _():
        q_ref, k_ref, b_ref = exact_refs
        for c in range(n_chunks):
            for u in range(n_units):
                q, k, lf2 = gates(c, u)
                q_ref[...] = q
                k_ref[...] = k
                b_ref[...] = _cumsum_rows(lf2)
                ps = _exact_scores(q_ref, k_ref, b_ref, heads_per_group)
                vs = values(c, u)
                finish(c, u, [oi_ref[c, u * heads_per_unit + n] + _dot(ps[n], vs[n])
                              for n in range(heads_per_unit)])


def _pipelined_mixer_step(x_ref, win_ref, buf_a, buf_b, st_ref, mix, n_slots):
    i = pl.program_id(0)

    @pl.when(i == 0)
    def _():
        st_ref[...] = jnp.zeros_like(st_ref)
        buf_b[...] = jnp.zeros_like(buf_b)

    def step(w_buf, r_buf):
        xb = x_ref[...].astype(BF16)
        width = w_buf.shape[1]
        pending = list(range(0, width, PROJ_SLAB))
        n_slabs = len(pending)
        seen = [0]

        def project_slab():
            lo = pending.pop(0)
            hi = min(lo + PROJ_SLAB, width)
            w_buf[:, lo:hi] = _dot(xb, win_ref[:, lo:hi])

        def slot():
            seen[0] += 1
            while pending and (n_slabs - len(pending)) * n_slots < seen[0] * n_slabs:
                project_slab()

        def flush():
            while pending:
                project_slab()

        mix(r_buf, slot, flush)

    @pl.when(lax.rem(i, 2) == 0)
    def _():
        step(buf_a, buf_b)

    @pl.when(lax.rem(i, 2) == 1)
    def _():
        step(buf_b, buf_a)


def _mixer_call(kern, name, x2d, consts, n_state, n_heads):
    seq, d_model = x2d.shape
    proj_width = consts[0][0].shape[-1]
    n_blocks = seq // MIX_ROWS
    return pl.pallas_call(
        kern,
        grid=(n_blocks + 1,),
        in_specs=[pl.BlockSpec((MIX_ROWS, d_model), lambda i: (jnp.minimum(i, n_blocks - 1), 0))]
        + [spec for _, spec in consts],
        out_specs=pl.BlockSpec((MIX_ROWS, d_model), lambda i: (jnp.maximum(i - 1, 0), 0)),
        out_shape=jax.ShapeDtypeStruct((seq, d_model), BF16),
        scratch_shapes=[
            pltpu.VMEM((MIX_ROWS, proj_width), F32),
            pltpu.VMEM((MIX_ROWS, proj_width), F32),
            pltpu.VMEM((n_state, LANES, LANES), F32),
            pltpu.VMEM((MIX_ROWS // LIN_CHUNK, n_heads, LIN_CHUNK, LANES), F32),
            pltpu.VMEM((LIN_CHUNK, UNIT), F32),
            pltpu.VMEM((LIN_CHUNK, UNIT), F32),
            pltpu.VMEM((LIN_CHUNK, UNIT), F32),
        ],
        compiler_params=pltpu.CompilerParams(
            dimension_semantics=("arbitrary",), vmem_limit_bytes=VMEM_LIMIT),
        name=name,
    )(x2d, *[arr for arr, _ in consts])


def _hgrn_kernel(x_ref, win_ref, lbl_ref, ng_ref, o_ref, buf_a, buf_b, st_ref, oi_ref,
                 eq_ref, ek_ref, eb_ref, *, layer_j, d_model):
    logits = lbl_ref[...]
    e = jnp.exp(logits - jnp.max(logits, axis=0, keepdims=True))
    probs = e / jnp.sum(e, axis=0, keepdims=True)
    lb = jnp.zeros((1, d_model), F32)
    for r in range(1, layer_j + 1):
        lb = lb + probs[r:r + 1, :]
    one_m_lb = 1.0 - lb
    rows = x_ref.shape[0]
    heads_per_unit = UNIT // LANES

    def mix(proj_ref, slot, flush):
        def gates(c, u):
            rs = slice(c * LIN_CHUNK, (c + 1) * LIN_CHUNK)
            ls = slice(u * UNIT, (u + 1) * UNIT)
            q = proj_ref[rs, u * UNIT:(u + 1) * UNIT]
            f = proj_ref[rs, d_model + u * UNIT:d_model + (u + 1) * UNIT]
            ef = jnp.exp(-jnp.abs(f))
            inv = 1.0 / (1.0 + ef)
            pos = f >= 0.0
            forget = jnp.where(pos, 1.0 + lb[:, ls] * ef, lb[:, ls] + ef) * inv
            lf2 = jnp.where(forget > 0.0, jnp.log2(forget), jnp.minimum(f, 0.0) * LOG2_E)
            sig_neg = jnp.where(pos, ef, 1.0) * inv
            return q * jax.nn.sigmoid(q), one_m_lb[:, ls] * sig_neg, lf2

        def values(c, u):
            rs = slice(c * LIN_CHUNK, (c + 1) * LIN_CHUNK)
            return [proj_ref[rs, 2 * d_model + h * LANES:2 * d_model + (h + 1) * LANES].astype(BF16)
                    for h in range(u * heads_per_unit, (u + 1) * heads_per_unit)]

        def finish(c, u, outs):
            rs = slice(c * LIN_CHUNK, (c + 1) * LIN_CHUNK)
            ng = ng_ref[...]
            for n, o in enumerate(outs):
                h = u * heads_per_unit + n
                ms = jnp.mean(o * o, axis=-1, keepdims=True)
                g = proj_ref[rs, 3 * d_model + h * LANES:3 * d_model + (h + 1) * LANES]
                y = o * lax.rsqrt(ms + RMS_EPS) * ng * jax.nn.sigmoid(g)
                o_ref[rs, h * LANES:(h + 1) * LANES] = y.astype(o_ref.dtype)

        _recurrent_mixer(rows // LIN_CHUNK, d_model // UNIT, 1, gates, values, finish, st_ref,
                         oi_ref, (eq_ref, ek_ref, eb_ref), lambda c: None, slot, flush)

    n_slots = (2 * (rows // LIN_CHUNK) + 1) * (d_model // UNIT)
    _pipelined_mixer_step(x_ref, win_ref, buf_a, buf_b, st_ref, mix, n_slots)


def _hgrn_mixer(x2d, consts, layer_j):
    d_model = x2d.shape[1]
    kern = functools.partial(_hgrn_kernel, layer_j=layer_j, d_model=d_model)
    return _mixer_call(kern, "hgrn_mixer", x2d, consts, d_model // LANES, d_model // LANES)


def _even_kernel(x_ref, win_ref, ws_ref, bs_ref, lng_ref, lnb_ref, wgk_ref, bgk_ref, ng_ref,
                 o_ref, buf_a, buf_b, st_ref, oi_ref, eq_ref, ek_ref, eb_ref, *, a_width, key_width,
                 b_width):
    rows = x_ref.shape[0]
    o_q = 2 * a_width
    o_k = o_q + key_width
    o_v = o_k + key_width
    o_g = o_v + b_width
    o_gk = o_g + b_width
    group_dim = a_width // A_GROUPS
    n_sgu = rows // SGU_CHUNK
    n_chunks = rows // LIN_CHUNK
    n_heads = b_width // LANES
    head_k = key_width // n_heads
    heads_per_group = LANES // head_k
    scale = head_k ** -0.5
    assert key_width == UNIT and n_chunks >= A_GROUPS

    def mix(proj_ref, slot, flush):
        def sgu_group(gi):
            ls = slice(gi * group_dim, (gi + 1) * group_dim)
            v = jax.nn.gelu(proj_ref[:, a_width + gi * group_dim:a_width + (gi + 1) * group_dim])
            vn = _layer_norm(v, lng_ref[:, ls], lnb_ref[:, ls]).astype(BF16)
            rhs = jnp.concatenate(
                [vn[n * SGU_CHUNK:(n + 1) * SGU_CHUNK, :] for n in range(n_sgu)], axis=1)
            w = jnp.where(_causal_mask(SGU_CHUNK), ws_ref[gi], 0.0).astype(BF16)
            mixed = _dot(w, rhs)
            bias = bs_ref[:, ls]
            for n in range(n_sgu):
                rs = slice(n * SGU_CHUNK, (n + 1) * SGU_CHUNK)
                u = jax.nn.gelu(proj_ref[rs, ls])
                y = u * (mixed[:, n * group_dim:(n + 1) * group_dim] + bias)
                o_ref[rs, ls] = y.astype(o_ref.dtype)

        def extra(c):
            if c < A_GROUPS:
                sgu_group(c)
                slot()

        def gates(c, u):
            rs = slice(c * LIN_CHUNK, (c + 1) * LIN_CHUNK)
            gk_low = proj_ref[rs, o_gk:o_gk + LANES].astype(BF16)
            z = _dot(gk_low, wgk_ref[...]) + bgk_ref[...]
            lf2 = (jnp.minimum(z, 0.0) * LOG2_E - jnp.log2(1.0 + jnp.exp(-jnp.abs(z)))) * (
                1.0 / GLA_GATE_NORMALIZER)
            q = proj_ref[rs, o_q:o_q + key_width] * scale
            k = proj_ref[rs, o_k:o_k + key_width]
            return q, k, lf2

        def values(c, u):
            rs = slice(c * LIN_CHUNK, (c + 1) * LIN_CHUNK)
            return [proj_ref[rs, o_v + h * LANES:o_v + (h + 1) * LANES].astype(BF16)
                    for h in range(n_heads)]

        def finish(c, u, outs):
            rs = slice(c * LIN_CHUNK, (c + 1) * LIN_CHUNK)
            ng = ng_ref[...]
            for h, o in enumerate(outs):
                ms = jnp.mean(o * o, axis=-1, keepdims=True)
                g = proj_ref[rs, o_g + h * LANES:o_g + (h + 1) * LANES]
                y = o * lax.rsqrt(ms + RMS_EPS) * ng * (g * jax.nn.sigmoid(g))
                o_ref[rs, a_width + h * LANES:a_width + (h + 1) * LANES] = y.astype(o_ref.dtype)

        _recurrent_mixer(n_chunks, 1, heads_per_group, gates, values, finish, st_ref, oi_ref,
                         (eq_ref, ek_ref, eb_ref), extra, slot, flush)

    _pipelined_mixer_step(x_ref, win_ref, buf_a, buf_b, st_ref, mix, 2 * n_chunks + A_GROUPS + 1)


def _even_mixer(x2d, consts, a_width, key_width, b_width):
    kern = functools.partial(_even_kernel, a_width=a_width, key_width=key_width, b_width=b_width)
    return _mixer_call(kern, "even_mixer", x2d, consts, key_width // LANES, b_width // LANES)


def _cast_specs(to_cast, n_steps):
    in_specs, out_specs, out_shapes = [], [], []
    for stacked, index, width in to_cast:
        rows, src_width = stacked.shape[1:]
        span = 1
        while rows % (n_steps // span) or (rows // (n_steps // span)) % BF16_ROWS:
            span *= 2
        block_rows = rows // (n_steps // span)
        in_specs.append(pl.BlockSpec((None, block_rows, src_width),
                                     lambda i, index=index, span=span: (index, i // span, 0)))
        out_specs.append(pl.BlockSpec((block_rows, width), lambda i, span=span: (i // span, 0)))
        out_shapes.append(jax.ShapeDtypeStruct((rows, width), BF16))
    return in_specs, out_specs, out_shapes


def _cast_rows(srcs, dsts):
    for src, dst in zip(srcs, dsts):
        width = src.shape[1]
        dst[:, :width] = src[...].astype(dst.dtype)
        if dst.shape[1] > width:
            dst[:, width:] = jnp.zeros((dst.shape[0], dst.shape[1] - width), dst.dtype)


def _cast_kernel(*refs):
    _cast_rows(refs[:len(refs) // 2], refs[len(refs) // 2:])


def _cast_weights(to_cast):
    in_specs, out_specs, out_shapes = _cast_specs(to_cast, CAST_STEPS)
    return pl.pallas_call(
        _cast_kernel,
        grid=(CAST_STEPS,),
        in_specs=in_specs,
        out_specs=out_specs,
        out_shape=out_shapes,
        compiler_params=pltpu.CompilerParams(
            dimension_semantics=("arbitrary",), vmem_limit_bytes=VMEM_LIMIT),
        name="cast_weights",
    )(*[stacked for stacked, _, _ in to_cast])


def _post_kernel(x_ref, y_ref, p_ref, wout_ref, lnmg_ref, lnmb_ref, wfi_ref, wfo_ref,
                 lnfg_ref, lnfb_ref, wpg_ref, wpp_ref, *rest, alpha):
    n_main = 2 - len(rest) % 2
    n_cast = (len(rest) - n_main) // 2
    cast_src, o_ref, cast_dst = rest[:n_cast], rest[n_cast], rest[n_cast + n_main:]
    ob_ref = rest[n_cast + 1] if n_main == 2 else None
    _cast_rows(cast_src, cast_dst)

    rows = x_ref.shape[0]
    d_ff = wfo_ref.shape[0]
    chunks = list(range(0, d_ff, FF_CHUNK))
    assert sum(POST_PARTS) == rows
    parts = [slice(sum(POST_PARTS[:n]), sum(POST_PARTS[:n + 1])) for n in range(len(POST_PARTS))]
    st = [dict() for _ in parts]

    def out_proj(n):
        st[n]["mixed"] = _dot(y_ref[parts[n], :], wout_ref[...])

    def norm_mix(n):
        x1 = _layer_norm(alpha * x_ref[parts[n], :] + st[n].pop("mixed"), lnmg_ref[...], lnmb_ref[...])
        st[n]["x1"], st[n]["x1b"] = x1, x1.astype(BF16)

    def ffn_in(n, c0):
        x1b = st[n]["x1b"]
        st[n]["h", c0] = (_dot(x1b, wfi_ref[:, c0:c0 + FF_CHUNK]),
                          _dot(x1b, wfi_ref[:, d_ff + c0:d_ff + c0 + FF_CHUNK]))

    def ffn_out(n, k):
        if k + 1 < len(chunks):
            ffn_in(n, chunks[k + 1])
        hg, hu = st[n].pop(("h", chunks[k]))
        act = (hg * jax.nn.sigmoid(hg) * hu).astype(BF16)
        part = _dot(act, wfo_ref[chunks[k]:chunks[k] + FF_CHUNK, :])
        st[n]["acc"] = part if k == 0 else st[n]["acc"] + part

    def norm_ffn(n):
        st[n]["x2"] = _layer_norm(alpha * st[n].pop("x1") + st[n].pop("acc"), lnfg_ref[...],
                                  lnfb_ref[...])

    def embed(n):
        x2 = st[n].pop("x2")
        gate = jax.nn.sigmoid(_dot(x2.astype(BF16), wpg_ref[...]))
        emb = _dot(p_ref[parts[n], :].astype(BF16), wpp_ref[...])
        res = x2 + gate * emb
        o_ref[parts[n], :] = res
        if ob_ref is not None:
            ob_ref[parts[n], :] = res.astype(ob_ref.dtype)

    stages = [out_proj, norm_mix, lambda n: ffn_in(n, chunks[0])]
    stages += [functools.partial(ffn_out, k=k) for k in range(len(chunks))]
    stages += [norm_ffn, embed]
    for t in range(len(stages) + POST_LAG * (len(parts) - 1)):
        for n in range(len(parts)):
            k = t - n * POST_LAG
            if 0 <= k < len(stages):
                stages[k](n)


def _post(x2d, y2d, p3d, layer, consts, alpha, to_cast):
    seq, d_model = x2d.shape
    n_steps = seq // POST_ROWS
    cast_in, cast_out, cast_shapes = _cast_specs(to_cast, n_steps)
    kern = functools.partial(_post_kernel, alpha=alpha)
    n_main = 2 if to_cast else 1
    outs = pl.pallas_call(
        kern,
        grid=(n_steps,),
        in_specs=[
            pl.BlockSpec((POST_ROWS, d_model), lambda i: (i, 0)),
            pl.BlockSpec((POST_ROWS, d_model), lambda i: (i, 0)),
            pl.BlockSpec((None, POST_ROWS, p3d.shape[2]), lambda i: (layer, i, 0)),
        ] + [spec for _, spec in consts] + cast_in,
        out_specs=[pl.BlockSpec((POST_ROWS, d_model), lambda i: (i, 0))] * n_main + cast_out,
        out_shape=[jax.ShapeDtypeStruct((seq, d_model), dt) for dt in (F32, BF16)[:n_main]]
        + cast_shapes,
        compiler_params=pltpu.CompilerParams(
            dimension_semantics=("arbitrary",), vmem_limit_bytes=VMEM_LIMIT),
        name="post",
    )(x2d, y2d, p3d, *[arr for arr, _ in consts], *[stacked for stacked, _, _ in to_cast])
    return outs[0], (outs[1] if n_main == 2 else None), outs[n_main:]


def kernel(x, p, even_w_in, even_w_s, even_b_s, even_sgu_ln_g, even_sgu_ln_b, even_w_gk_up, even_b_gk,
           even_gla_norm_g, even_w_out, odd_w_in, hgrn_lb_logits, odd_hgrn_norm_g, odd_w_out,
           ln_mix_g, ln_mix_b, w_ffn_in, w_ffn_out, ln_ffn_g, ln_ffn_b, w_ple_proj, w_ple_gate):
    batch, seq, d_model = x.shape
    depth = p.shape[0]
    assert batch == 1 and seq % POST_ROWS == 0 and seq % MIX_ROWS == 0
    alpha = (2.0 * depth) ** 0.25
    a_width = even_sgu_ln_g.shape[1]
    b_width = d_model - a_width
    key_width = even_w_gk_up.shape[2]
    assert w_ffn_out.shape[1] % FF_CHUNK == 0
    even_in_pad = -(-even_w_in.shape[2] // LANES) * LANES

    def big_weights(layer):
        j = layer // 2
        w_in, w_out = (even_w_in, even_w_out) if layer % 2 == 0 else (odd_w_in, odd_w_out)
        mats = [(w_in, j), (w_out, j), (w_ffn_in, layer), (w_ffn_out, layer), (w_ple_gate, layer),
                (w_ple_proj, layer)]
        return [(w, idx, -(-w.shape[2] // LANES) * LANES) for w, idx in mats]

    rows3 = lambda a: a.reshape(a.shape[0], 1, -1).astype(F32)
    even_w_gk_b = jnp.pad(even_w_gk_up, ((0, 0), (0, LANES - GK_RANK), (0, 0))).astype(BF16)
    even_bias = jnp.repeat(even_b_s.transpose(0, 2, 1), a_width // A_GROUPS, axis=2)
    even_small = (even_w_s, even_bias, rows3(even_sgu_ln_g), rows3(even_sgu_ln_b), even_w_gk_b,
                  rows3(even_b_gk), rows3(even_gla_norm_g))
    odd_norm_g = rows3(odd_hgrn_norm_g)
    lnm_g, lnm_b, lnf_g, lnf_b = rows3(ln_mix_g), rows3(ln_mix_b), rows3(ln_ffn_g), rows3(ln_ffn_b)

    big = _cast_weights(big_weights(0))
    assert big[0].shape[1] == even_in_pad
    xs = x.reshape(seq, d_model)
    xm = xs
    p3 = p.reshape(depth, seq, -1)
    for layer in range(depth):
        j = layer // 2
        const = lambda a: (a, _const_spec(a.shape))
        w_in_b, w_out_b, wfi_b, wfo_b, wpg_b, wpp_b = big
        if layer % 2 == 0:
            y = _even_mixer(xm, [const(w_in_b)] + [_layer_operand(a, j) for a in even_small],
                            a_width, key_width, b_width)
        else:
            y = _hgrn_mixer(xm, [const(w_in_b), const(hgrn_lb_logits.astype(F32)),
                                 _layer_operand(odd_norm_g, j)], j)
        consts = [const(w_out_b), _layer_operand(lnm_g, layer), _layer_operand(lnm_b, layer),
                  const(wfi_b), const(wfo_b), _layer_operand(lnf_g, layer),
                  _layer_operand(lnf_b, layer), const(wpg_b), const(wpp_b)]
        xs, xm, big = _post(xs, y, p3, layer, consts, alpha,
                            big_weights(layer + 1) if layer + 1 < depth else [])
    return xs.reshape(batch, seq, d_model)
```
